```python
import math
import jax, jax.numpy as jnp
from jax import lax
import numpy as np

D_MODEL = 2048
BATCH = 4
SEQ = 2048
DEPTH = 2

HEAD_DIM = 64
N_FOX = 8
N_SB = 8
N_MLA = 4
MLA_NOPE = 128
MLA_ROPE = 64
MLA_V = 128
Q_LORA = 512
KV_LORA = 256
N_SGU = 8
SGU_CH = 64
CHUNK = 128
Q_BLOCK = 128
ROPE_THETA = 10000.0
D_FOX = N_FOX * HEAD_DIM
D_SB = N_SB * HEAD_DIM
D_MLA = N_MLA * MLA_V
D_SGU = N_SGU * SGU_CH
N_GROUPS = 4
D_MIX = D_FOX + D_SB + D_MLA + D_SGU
IN_SIZES = (3 * D_FOX, N_FOX, 3 * D_SB, Q_LORA, KV_LORA, MLA_ROPE, 2 * D_SGU)
D_IN = 3 * D_FOX + N_FOX + 3 * D_SB + Q_LORA + KV_LORA + MLA_ROPE + 2 * D_SGU
N_EXPERTS = 32
TOP_K = 4
D_FF = 2048
SWIGLU_LIMIT = 7.0
SWIGLU_ALPHA = 1.702
MOE_BLOCK = 128

kernel_name = "hybrid_fox_sb_mla_sgu_moe"


def rmsnorm(x, g, eps=1e-6):
    xf = x.astype(jnp.float32)
    y = xf * lax.rsqrt(jnp.mean(xf * xf, axis=-1, keepdims=True) + eps)
    return (y * g.astype(jnp.float32)).astype(x.dtype)


def layernorm(x, g, b, eps=1e-5):
    xf = x.astype(jnp.float32)
    mu = jnp.mean(xf, axis=-1, keepdims=True)
    xc = xf - mu
    var = jnp.mean(xc * xc, axis=-1, keepdims=True)
    return (xc * lax.rsqrt(var + eps) * g.astype(jnp.float32) + b.astype(jnp.float32)).astype(x.dtype)


def apply_rope(x, positions):
    half = x.shape[-1] // 2
    inv_freq = ROPE_THETA ** (-jnp.arange(half, dtype=jnp.float32) / half)
    ang = positions.astype(jnp.float32)[..., None] * inv_freq
    ang = ang.reshape(ang.shape[:2] + (1,) * (x.ndim - 3) + (half,))
    cos, sin = jnp.cos(ang), jnp.sin(ang)
    xf = x.astype(jnp.float32)
    x1, x2 = xf[..., :half], xf[..., half:]
    return jnp.concatenate([x1 * cos - x2 * sin, x2 * cos + x1 * sin], axis=-1).astype(x.dtype)


def causal_blocks(block_fn, seq_len):
    outs = [block_fn(i * Q_BLOCK, (i + 1) * Q_BLOCK) for i in range(seq_len // Q_BLOCK)]
    return jnp.concatenate(outs, axis=1)


def block_positions(q0, q1):
    t_idx = jnp.arange(q0, q1)[:, None]
    s_idx = jnp.arange(q1)[None, :]
    return t_idx, s_idx


def forgetting_attention(q, k, v, log_f):
    c = jnp.cumsum(log_f.astype(jnp.float32), axis=1).transpose(0, 2, 1)
    scale = HEAD_DIM ** -0.5

    def block(q0, q1):
        s = jnp.einsum('bthd,bshd->bhts', q[:, q0:q1], k[:, :q1],
                       preferred_element_type=jnp.float32) * scale
        decay = c[:, :, q0:q1, None] - c[:, :, None, :q1]
        t_idx, s_idx = block_positions(q0, q1)
        s = jnp.where(s_idx <= t_idx, s + decay, -jnp.inf)
        p = jax.nn.softmax(s, axis=-1).astype(v.dtype)
        return jnp.einsum('bhts,bshd->bthd', p, v[:, :q1])

    return causal_blocks(block, q.shape[1])


def stick_breaking_attention(q, k, v):
    scale = HEAD_DIM ** -0.5

    def block(q0, q1):
        z = jnp.einsum('bthd,bshd->bhts', q[:, q0:q1], k[:, :q1],
                       preferred_element_type=jnp.float32) * scale
        t_idx, s_idx = block_positions(q0, q1)
        strict = s_idx < t_idx
        log_rem = jnp.where(strict, jax.nn.log_sigmoid(-z), 0.0)
        after = lax.cumsum(log_rem, axis=3, reverse=True) - log_rem
        log_a = jax.nn.log_sigmoid(z) + after
        a = jnp.where(strict, jnp.exp(log_a), 0.0).astype(v.dtype)
        return jnp.einsum('bhts,bshd->bthd', a, v[:, :q1])

    return causal_blocks(block, q.shape[1])


def latent_attention(c_q, c_kv, k_rope_raw, positions, q_norm_g, kv_norm_g, w_q_b, w_kv_b):
    B, S, _ = c_q.shape
    q = (rmsnorm(c_q, q_norm_g) @ w_q_b).reshape(B, S, N_MLA, MLA_NOPE + MLA_ROPE)
    q_nope, q_pe = q[..., :MLA_NOPE], apply_rope(q[..., MLA_NOPE:], positions)
    kv = (rmsnorm(c_kv, kv_norm_g) @ w_kv_b).reshape(B, S, N_MLA, MLA_NOPE + MLA_V)
    k_nope, v = kv[..., :MLA_NOPE], kv[..., MLA_NOPE:]
    k_pe = apply_rope(k_rope_raw, positions)
    scale = (MLA_NOPE + MLA_ROPE) ** -0.5

    def block(q0, q1):
        s = (jnp.einsum('bthd,bshd->bhts', q_nope[:, q0:q1], k_nope[:, :q1],
                        preferred_element_type=jnp.float32)
             + jnp.einsum('bthr,bsr->bhts', q_pe[:, q0:q1], k_pe[:, :q1],
                          preferred_element_type=jnp.float32)) * scale
        t_idx, s_idx = block_positions(q0, q1)
        s = jnp.where(s_idx <= t_idx, s, -jnp.inf)
        p = jax.nn.softmax(s, axis=-1).astype(v.dtype)
        return jnp.einsum('bhts,bshd->bthd', p, v[:, :q1])

    return causal_blocks(block, S)


def spatial_gating(z, ln_g, ln_b, w_s, b_s):
    B, S, _ = z.shape
    z = jax.nn.gelu(z, approximate=False)
    u, v = z[..., :D_SGU], z[..., D_SGU:]
    v = layernorm(v, ln_g, ln_b)
    vc = v.reshape(B, S // CHUNK, CHUNK, N_SGU, SGU_CH)
    tril = jnp.tril(jnp.ones((CHUNK, CHUNK), dtype=bool))
    w = jnp.where(tril[None], w_s, 0.0).astype(v.dtype)
    mixed = jnp.einsum('gts,bcsgd->bctgd', w, vc) + b_s.T[None, None, :, :, None].astype(v.dtype)
    return u * mixed.reshape(B, S, D_SGU)


def moe_ffn(h, router_w, router_b, w_gate, b_gate, w_up, b_up, w_down, b_down):
    B, S, D = h.shape
    T = B * S
    x = h.reshape(T, D)
    logits = (x @ router_w + router_b).astype(jnp.float32)
    top_val, top_idx = lax.top_k(logits, TOP_K)
    gates = jax.nn.softmax(top_val, axis=-1)
    n_assign = T * TOP_K
    flat_e = top_idx.reshape(-1)
    order = jnp.argsort(flat_e)
    sorted_e = flat_e[order]
    src_tok = (order // TOP_K).astype(jnp.int32)
    counts = jnp.bincount(flat_e, length=N_EXPERTS)
    padded = (counts + MOE_BLOCK - 1) // MOE_BLOCK * MOE_BLOCK
    start = jnp.cumsum(counts) - counts
    pend = jnp.cumsum(padded)
    pstart = pend - padded
    dest = pstart[sorted_e] + (jnp.arange(n_assign) - start[sorted_e])
    n_rows = n_assign + N_EXPERTS * MOE_BLOCK
    n_blocks = n_rows // MOE_BLOCK
    rows = jnp.zeros((n_rows, D), x.dtype).at[dest].set(x[src_tok])
    row_tok = jnp.zeros((n_rows,), jnp.int32).at[dest].set(src_tok)
    row_gate = jnp.zeros((n_rows,), jnp.float32).at[dest].set(gates.reshape(-1)[order])
    block_expert = jnp.minimum(
        jnp.searchsorted(pend, jnp.arange(n_blocks) * MOE_BLOCK, side='right'), N_EXPERTS - 1)

    def expert_block(args):
        xb, e = args
        g = jnp.minimum(xb @ w_gate[e] + b_gate[e], SWIGLU_LIMIT)
        u = jnp.clip(xb @ w_up[e] + b_up[e], -SWIGLU_LIMIT, SWIGLU_LIMIT)
        a = g * jax.nn.sigmoid(SWIGLU_ALPHA * g) * (u + 1.0)
        return a @ w_down[e] + b_down[e]

    y = lax.map(expert_block, (rows.reshape(n_blocks, MOE_BLOCK, D), block_expert))
    y = y.reshape(n_rows, D) * row_gate[:, None].astype(y.dtype)
    out = jax.ops.segment_sum(y, row_tok, num_segments=T)
    return out.reshape(B, S, D)


def hybrid_layer(x, positions, norm_mix_g, w_in, b_forget, mla_q_norm_g, mla_kv_norm_g,
                 mla_w_q_b, mla_w_kv_b, sgu_ln_g, sgu_ln_b, sgu_w_s, sgu_b_s, out_norm_g, w_o,
                 norm_ffn_g, router_w, router_b, w_gate, b_gate, w_up, b_up, w_down, b_down):
    B, S, _ = x.shape
    h = rmsnorm(x, norm_mix_g)
    proj = h @ w_in
    splits = np.cumsum(IN_SIZES)[:-1].tolist()
    fox_qkv, fox_gate, sb_qkv, c_q, c_kv, k_rope, sgu_in = jnp.split(proj, splits, axis=-1)

    fox_qkv = fox_qkv.reshape(B, S, 3, N_FOX, HEAD_DIM)
    log_f = jax.nn.log_sigmoid((fox_gate + b_forget).astype(jnp.float32))
    o_fox = forgetting_attention(fox_qkv[:, :, 0], fox_qkv[:, :, 1], fox_qkv[:, :, 2], log_f)

    sb_qkv = sb_qkv.reshape(B, S, 3, N_SB, HEAD_DIM)
    o_sb = stick_breaking_attention(sb_qkv[:, :, 0], sb_qkv[:, :, 1], sb_qkv[:, :, 2])

    o_mla = latent_attention(c_q, c_kv, k_rope, positions, mla_q_norm_g, mla_kv_norm_g,
                             mla_w_q_b, mla_w_kv_b)

    o_sgu = spatial_gating(sgu_in, sgu_ln_g, sgu_ln_b, sgu_w_s, sgu_b_s)

    o = jnp.stack([o_fox.reshape(B, S, D_FOX), o_sb.reshape(B, S, D_SB),
                   o_mla.reshape(B, S, D_MLA), o_sgu], axis=2)
    o = rmsnorm(o, jnp.ones((), o.dtype)).reshape(B, S, D_MIX) * out_norm_g.astype(o.dtype)
    x = x + o @ w_o

    h2 = rmsnorm(x, norm_ffn_g)
    return x + moe_ffn(h2, router_w, router_b, w_gate, b_gate, w_up, b_up, w_down, b_down)


def setup_inputs(seed: int = 0) -> dict:
    key = jax.random.key(seed)
    ks = jax.random.split(key, 32)
    f32 = jnp.float32

    def normal(k, shape, scale):
        return jax.random.normal(k, shape, f32) * scale

    def gain(k, shape):
        return 1.0 + 0.05 * jax.random.normal(k, shape, f32)

    L = DEPTH
    return {
        "x": jax.random.normal(ks[0], (BATCH, SEQ, D_MODEL), f32),
        "positions": jnp.broadcast_to(jnp.arange(SEQ, dtype=jnp.int32), (BATCH, SEQ)),
        "norm_mix_g": gain(ks[1], (L, D_MODEL)),
        "w_in": normal(ks[2], (L, D_MODEL, D_IN), D_MODEL ** -0.5),
        "b_forget": jax.random.uniform(ks[3], (L, N_FOX), f32, 1.0, 4.0),
        "mla_q_norm_g": gain(ks[4], (L, Q_LORA)),
        "mla_kv_norm_g": gain(ks[5], (L, KV_LORA)),
        "mla_w_q_b": normal(ks[6], (L, Q_LORA, N_MLA * (MLA_NOPE + MLA_ROPE)), Q_LORA ** -0.5),
        "mla_w_kv_b": normal(ks[7], (L, KV_LORA, N_MLA * (MLA_NOPE + MLA_V)), KV_LORA ** -0.5),
        "sgu_ln_g": gain(ks[8], (L, D_SGU)),
        "sgu_ln_b": normal(ks[9], (L, D_SGU), 0.02),
        "sgu_w_s": normal(ks[10], (L, N_SGU, CHUNK, CHUNK), CHUNK ** -0.5),
        "sgu_b_s": 1.0 + normal(ks[11], (L, N_SGU, CHUNK), 0.1),
        "out_norm_g": gain(ks[12], (L, D_MIX)),
        "w_o": normal(ks[13], (L, D_MIX, D_MODEL), D_MIX ** -0.5),
        "norm_ffn_g": gain(ks[14], (L, D_MODEL)),
        "router_w": normal(ks[15], (L, D_MODEL, N_EXPERTS), D_MODEL ** -0.5),
        "router_b": normal(ks[16], (L, N_EXPERTS), 0.01),
        "w_gate": normal(ks[17], (L, N_EXPERTS, D_MODEL, D_FF), D_MODEL ** -0.5),
        "b_gate": normal(ks[18], (L, N_EXPERTS, D_FF), 0.01),
        "w_up": normal(ks[19], (L, N_EXPERTS, D_MODEL, D_FF), D_MODEL ** -0.5),
        "b_up": normal(ks[20], (L, N_EXPERTS, D_FF), 0.01),
        "w_down": normal(ks[21], (L, N_EXPERTS, D_FF, D_MODEL), D_FF ** -0.5),
        "b_down": normal(ks[22], (L, N_EXPERTS, D_MODEL), 0.01),
        "final_norm_g": gain(ks[23], (D_MODEL,)),
    }


def reference(x, positions, norm_mix_g, w_in, b_forget, mla_q_norm_g, mla_kv_norm_g, mla_w_q_b,
              mla_w_kv_b, sgu_ln_g, sgu_ln_b, sgu_w_s, sgu_b_s, out_norm_g, w_o, norm_ffn_g,
              router_w, router_b, w_gate, b_gate, w_up, b_up, w_down, b_down, final_norm_g):
    for l in range(DEPTH):
        x = hybrid_layer(x, positions, norm_mix_g[l], w_in[l], b_forget[l], mla_q_norm_g[l],
                         mla_kv_norm_g[l], mla_w_q_b[l], mla_w_kv_b[l], sgu_ln_g[l], sgu_ln_b[l],
                         sgu_w_s[l], sgu_b_s[l], out_norm_g[l], w_o[l], norm_ffn_g[l], router_w[l],
                         router_b[l], w_gate[l], b_gate[l], w_up[l], b_up[l], w_down[l], b_down[l])
    return rmsnorm(x, final_norm_g)
```

```python
import functools

import numpy as np
import jax
import jax.numpy as jnp
from jax import lax
from jax.experimental import pallas as pl
from jax.experimental.pallas import tpu as pltpu

HEAD_DIM = 64
N_FOX = 8
N_SB = 8
N_MLA = 4
MLA_NOPE = 128
MLA_ROPE = 64
MLA_V = 128
Q_LORA = 512
KV_LORA = 256
N_SGU = 8
SGU_CH = 64
CHUNK = 128
ROPE_THETA = 10000.0
D_FOX = N_FOX * HEAD_DIM
D_SB = N_SB * HEAD_DIM
D_MLA = N_MLA * MLA_V
D_SGU = N_SGU * SGU_CH
N_EXPERTS = 32
TOP_K = 4
SWIGLU_LIMIT = 7.0
SWIGLU_ALPHA = 1.702
RMS_EPS = 1e-6
LN_EPS = 1e-5

LANES = 128
VMEM_LIMIT = 56 * 1024 * 1024

ATT_TILE = 256
EXP_ROWS = 1280
EXP_SUB = 256
EXP_FF = 256
GATHER_CH = 2048
COMBINE_TM = 256

F32 = jnp.float32
BF16 = jnp.bfloat16

MISC_U = 0
MISC_V = 512
MISC_CQ = 1024
MISC_CKV = 1536
MISC_KR1 = 1792
MISC_KR2 = 1920
MISC_GATE = 2048
MISC_W = 2176


def _params(sem, vmem=VMEM_LIMIT):
    return pltpu.CompilerParams(dimension_semantics=sem, vmem_limit_bytes=vmem)


def _split3(x):
    hi = x.astype(BF16)
    r1 = x - hi.astype(F32)
    mid = r1.astype(BF16)
    lo = (r1 - mid.astype(F32)).astype(BF16)
    return hi, mid, lo


def _log_sigmoid_pair(z):
    t = jnp.log1p(jnp.exp(-jnp.abs(z)))
    return jnp.minimum(z, 0.0) - t, jnp.minimum(-z, 0.0) - t


def _rms(x):
    return x * lax.rsqrt(jnp.mean(x * x, axis=-1, keepdims=True) + RMS_EPS)


def _norm_matmul_kernel(x_ref, g_ref, w_ref, o_ref, xn_ref):
    @pl.when(pl.program_id(1) == 0)
    def _():
        xn_ref[...] = (_rms(x_ref[...]) * g_ref[...]).astype(BF16)

    o_ref[...] = jnp.dot(xn_ref[...], w_ref[...], preferred_element_type=F32).astype(o_ref.dtype)


def _norm_matmul(x, g, w, out_dtype, tm, tn, name):
    t, d = x.shape
    n = w.shape[1]
    return pl.pallas_call(
        _norm_matmul_kernel,
        out_shape=jax.ShapeDtypeStruct((t, n), out_dtype),
        grid=(t // tm, n // tn),
        in_specs=[
            pl.BlockSpec((tm, d), lambda i, j: (i, 0)),
            pl.BlockSpec((1, d), lambda i, j: (0, 0)),
            pl.BlockSpec((d, tn), lambda i, j: (0, j)),
        ],
        out_specs=pl.BlockSpec((tm, tn), lambda i, j: (i, j)),
        scratch_shapes=[pltpu.VMEM((tm, d), BF16)],
        compiler_params=_params(("parallel", "arbitrary")),
        name=name,
    )(x, g.reshape(1, d), w)


def _fox_prep_kernel(gate_ref, bf_ref, ccol_ref, crow_ref):
    s_len = gate_ref.shape[0]
    bl = ATT_TILE
    r = lax.broadcasted_iota(jnp.int32, (bl, bl), 0)
    c = lax.broadcasted_iota(jnp.int32, (bl, bl), 1)
    tri = (c <= r).astype(BF16)
    carry = jnp.zeros((1, LANES), F32)
    for i in range(s_len // bl):
        z = gate_ref[i * bl:(i + 1) * bl, :] + bf_ref[...]
        lf, _ = _log_sigmoid_pair(z)
        hi, mid, lo = _split3(lf)
        cs = (jnp.dot(tri, hi, preferred_element_type=F32)
              + jnp.dot(tri, mid, preferred_element_type=F32)
              + jnp.dot(tri, lo, preferred_element_type=F32)) + carry
        ccol_ref[i * bl:(i + 1) * bl, :] = cs
        crow_ref[i] = cs.T[0:N_FOX, :]
        carry = cs[bl - 1:bl, :]


def _fox_prep(misc, b_forget, batch, seq):
    t = misc.shape[0]
    nkb = seq // ATT_TILE
    bf = jnp.zeros((1, LANES), F32).at[0, :N_FOX].set(b_forget)
    return pl.pallas_call(
        _fox_prep_kernel,
        out_shape=(jax.ShapeDtypeStruct((t, LANES), F32),
                   jax.ShapeDtypeStruct((batch, nkb, N_FOX, ATT_TILE), F32)),
        grid=(batch,),
        in_specs=[
            pl.BlockSpec((seq, LANES), lambda b: (b, MISC_GATE // LANES)),
            pl.BlockSpec((1, LANES), lambda b: (0, 0)),
        ],
        out_specs=(pl.BlockSpec((seq, LANES), lambda b: (b, 0)),
                   pl.BlockSpec((None, nkb, N_FOX, ATT_TILE), lambda b: (b, 0, 0, 0))),
        compiler_params=_params(("parallel",)),
        name="fox_prep",
    )(misc, bf)


def _causal_mask(tq):
    row = lax.broadcasted_iota(jnp.int32, (tq, tq), 0)
    col = lax.broadcasted_iota(jnp.int32, (tq, tq), 1)
    return col <= row, col < row


def _softmax_sweep(qs, k_of, v_of, bias_of, scale, qi, tq):
    nh = len(qs)
    incl, _ = _causal_mask(tq)

    def step(kb, carry, diag):
        ks = pl.multiple_of(kb * tq, tq)
        out = []
        for h in range(nh):
            m, l, acc = carry[h]
            s = lax.dot_general(qs[h], k_of(h, ks), (((1,), (1,)), ((), ())),
                                preferred_element_type=F32)
            if scale is not None:
                s = s * scale
            b = bias_of(h, kb)
            if b is not None:
                s = s + b
            if diag:
                s = jnp.where(incl, s, -jnp.inf)
            m_new = jnp.maximum(m, jnp.max(s, axis=-1, keepdims=True))
            alpha = jnp.exp(m - m_new)
            p = jnp.exp(s - m_new)
            l = alpha * l + jnp.sum(p, axis=-1, keepdims=True)
            acc = alpha * acc + jnp.dot(p.astype(BF16), v_of(h, ks), preferred_element_type=F32)
            out.append((m_new, l, acc))
        return tuple(out)

    dv = v_of(0, 0).shape[-1]
    init = tuple((jnp.full((tq, 1), -jnp.inf, F32), jnp.zeros((tq, 1), F32),
                  jnp.zeros((tq, dv), F32)) for _ in range(nh))
    carry = lax.fori_loop(0, qi, lambda kb, c: step(kb, c, False), init)
    carry = step(qi, carry, True)
    return [acc / l for (_, l, acc) in carry]


def _fox_attn_kernel(q_ref, k_ref, v_ref, ccol_ref, crow_ref, o_ref):
    tq = q_ref.shape[0]
    hp = pl.program_id(1)
    qi = pl.program_id(2)
    q = q_ref[...] * jnp.asarray(HEAD_DIM ** -0.5, BF16)
    lane = lax.broadcasted_iota(jnp.int32, (tq, LANES), 1)
    ccol = ccol_ref[...]
    qs, cts = [], []
    for j in range(2):
        qs.append(q[:, j * HEAD_DIM:(j + 1) * HEAD_DIM])
        cts.append(jnp.sum(jnp.where(lane == hp * 2 + j, ccol, 0.0), axis=-1, keepdims=True))

    def k_of(h, ks):
        return k_ref[pl.ds(ks, tq), h * HEAD_DIM:(h + 1) * HEAD_DIM]

    def v_of(h, ks):
        return v_ref[pl.ds(ks, tq), h * HEAD_DIM:(h + 1) * HEAD_DIM]

    def bias_of(h, kb):
        return cts[h] - crow_ref[kb][h:h + 1, :]

    outs = _softmax_sweep(qs, k_of, v_of, bias_of, None, qi, tq)
    o_ref[...] = jnp.concatenate(outs, axis=-1)


def _fox_attention(qkv, ccol, crow, batch, seq):
    t = qkv.shape[0]
    tq = ATT_TILE
    nq = seq // tq
    npair = N_FOX // 2
    return pl.pallas_call(
        _fox_attn_kernel,
        out_shape=jax.ShapeDtypeStruct((t, D_FOX), F32),
        grid=(batch, npair, nq),
        in_specs=[
            pl.BlockSpec((tq, LANES), lambda b, h, i: (b * nq + i, h)),
            pl.BlockSpec((seq, LANES), lambda b, h, i: (b, npair + h)),
            pl.BlockSpec((seq, LANES), lambda b, h, i: (b, 2 * npair + h)),
            pl.BlockSpec((tq, LANES), lambda b, h, i: (b * nq + i, 0)),
            pl.BlockSpec((None, None, nq, 2, tq), lambda b, h, i: (b, h, 0, 0, 0)),
        ],
        out_specs=pl.BlockSpec((tq, LANES), lambda b, h, i: (b * nq + i, h)),
        compiler_params=_params(("parallel", "parallel", "arbitrary")),
        name="fox_attn",
    )(qkv, qkv, qkv, ccol, crow)


def _mla_attn_kernel(q_ref, k_ref, v_ref, o_ref):
    tq = q_ref.shape[0]
    qi = pl.program_id(2)
    scale = (MLA_NOPE + MLA_ROPE) ** -0.5
    outs = _softmax_sweep([q_ref[...]],
                          lambda h, ks: k_ref[pl.ds(ks, tq), :],
                          lambda h, ks: v_ref[pl.ds(ks, tq), :],
                          lambda h, kb: None, scale, qi, tq)
    o_ref[...] = outs[0]


def _mla_attention(q, k, v, batch, seq):
    t = q.shape[0]
    tq = ATT_TILE
    nq = seq // tq
    dk = 2 * LANES
    return pl.pallas_call(
        _mla_attn_kernel,
        out_shape=jax.ShapeDtypeStruct((t, D_MLA), F32),
        grid=(batch, N_MLA, nq),
        in_specs=[
            pl.BlockSpec((tq, dk), lambda b, h, i: (b * nq + i, h)),
            pl.BlockSpec((seq, dk), lambda b, h, i: (b, h)),
            pl.BlockSpec((seq, MLA_V), lambda b, h, i: (b, h)),
        ],
        out_specs=pl.BlockSpec((tq, MLA_V), lambda b, h, i: (b * nq + i, h)),
        compiler_params=_params(("parallel", "parallel", "arbitrary")),
        name="mla_attn",
    )(q, k, v)


def _sb_attn_kernel(q_ref, k_ref, v_ref, o_ref):
    tq = q_ref.shape[0]
    qi = pl.program_id(2)
    q = q_ref[...] * jnp.asarray(HEAD_DIM ** -0.5, BF16)
    qs = [q[:, j * HEAD_DIM:(j + 1) * HEAD_DIM] for j in range(2)]
    _, strict = _causal_mask(tq)
    row = lax.broadcasted_iota(jnp.int32, (tq, tq), 0)
    col = lax.broadcasted_iota(jnp.int32, (tq, tq), 1)
    later = (row > col).astype(BF16)

    def step(kb, carry, diag):
        ks = pl.multiple_of(kb * tq, tq)
        out = []
        for h in range(2):
            rem, acc = carry[h]
            k = k_ref[pl.ds(ks, tq), h * HEAD_DIM:(h + 1) * HEAD_DIM]
            v = v_ref[pl.ds(ks, tq), h * HEAD_DIM:(h + 1) * HEAD_DIM]
            z = lax.dot_general(qs[h], k, (((1,), (1,)), ((), ())), preferred_element_type=F32)
            ls, lr = _log_sigmoid_pair(z)
            if diag:
                lr = jnp.where(strict, lr, 0.0)
            hi, mid, lo = _split3(lr)
            after = (jnp.dot(hi, later, preferred_element_type=F32)
                     + jnp.dot(mid, later, preferred_element_type=F32)
                     + jnp.dot(lo, later, preferred_element_type=F32)) + rem
            a = jnp.exp(ls + after)
            if diag:
                a = jnp.where(strict, a, 0.0)
            acc = acc + jnp.dot(a.astype(BF16), v, preferred_element_type=F32)
            rem = after[:, 0:1] + lr[:, 0:1]
            out.append((rem, acc))
        return tuple(out)

    init = tuple((jnp.zeros((tq, 1), F32), jnp.zeros((tq, HEAD_DIM), F32)) for _ in range(2))
    carry = step(qi, init, True)
    carry = lax.fori_loop(0, qi, lambda i, c: step(qi - 1 - i, c, False), carry)
    o_ref[...] = jnp.concatenate([acc for (_, acc) in carry], axis=-1)


def _sb_attention(qkv, batch, seq):
    t = qkv.shape[0]
    tq = ATT_TILE
    nq = seq // tq
    npair = N_SB // 2
    base = 3 * D_FOX // LANES
    return pl.pallas_call(
        _sb_attn_kernel,
        out_shape=jax.ShapeDtypeStruct((t, D_SB), F32),
        grid=(batch, npair, nq),
        in_specs=[
            pl.BlockSpec((tq, LANES), lambda b, h, i: (b * nq + i, base + h)),
            pl.BlockSpec((seq, LANES), lambda b, h, i: (b, base + npair + h)),
            pl.BlockSpec((seq, LANES), lambda b, h, i: (b, base + 2 * npair + h)),
        ],
        out_specs=pl.BlockSpec((tq, LANES), lambda b, h, i: (b * nq + i, h)),
        compiler_params=_params(("parallel", "parallel", "arbitrary")),
        name="sb_attn",
    )(qkv, qkv, qkv)


def _mla_prep_kernel(cq_ref, ckv_ref, kr1_ref, kr2_ref, pos_ref, gq_ref, gkv_ref, wq_ref, wkv_ref,
                     invf_ref, sgn_ref, q_out, k_out, v_out):
    ang = pos_ref[...].astype(F32) * invf_ref[...]
    cosv = jnp.cos(ang)
    sinv = jnp.sin(ang) * sgn_ref[...]
    qn = (_rms(cq_ref[...]) * gq_ref[...]).astype(BF16)
    qa = jnp.dot(qn, wq_ref[...], preferred_element_type=F32)
    for h in range(N_MLA):
        o = h * 3 * LANES
        pe = qa[:, o + LANES:o + 2 * LANES] * cosv + qa[:, o + 2 * LANES:o + 3 * LANES] * sinv
        q_out[:, h * 2 * LANES:h * 2 * LANES + LANES] = qa[:, o:o + LANES].astype(BF16)
        q_out[:, h * 2 * LANES + LANES:(h + 1) * 2 * LANES] = pe.astype(BF16)
    kvn = (_rms(ckv_ref[...]) * gkv_ref[...]).astype(BF16)
    kva = jnp.dot(kvn, wkv_ref[...], preferred_element_type=F32)
    kpe = (kr1_ref[...] * cosv + kr2_ref[...] * sinv).astype(BF16)
    for h in range(N_MLA):
        k_out[:, h * 2 * LANES:h * 2 * LANES + LANES] = kva[:, h * LANES:(h + 1) * LANES].astype(BF16)
        k_out[:, h * 2 * LANES + LANES:(h + 1) * 2 * LANES] = kpe
    v_out[...] = kva[:, N_MLA * MLA_NOPE:].astype(BF16)


def _mla_prep(misc, pos, gq, gkv, wq, wkv, tm=512):
    t = misc.shape[0]
    half = MLA_ROPE // 2
    inv_freq = ROPE_THETA ** (-jnp.arange(half, dtype=F32) / half)
    invf = jnp.tile(inv_freq, LANES // half).reshape(1, LANES)
    sgn = jnp.tile(jnp.concatenate([-jnp.ones((half,), F32), jnp.ones((half,), F32)]),
                   LANES // MLA_ROPE).reshape(1, LANES)
    const = lambda i: (0, 0)
    return pl.pallas_call(
        _mla_prep_kernel,
        out_shape=(jax.ShapeDtypeStruct((t, N_MLA * 2 * LANES), BF16),
                   jax.ShapeDtypeStruct((t, N_MLA * 2 * LANES), BF16),
                   jax.ShapeDtypeStruct((t, D_MLA), BF16)),
        grid=(t // tm,),
        in_specs=[
            pl.BlockSpec((tm, Q_LORA), lambda i: (i, MISC_CQ // Q_LORA)),
            pl.BlockSpec((tm, KV_LORA), lambda i: (i, MISC_CKV // KV_LORA)),
            pl.BlockSpec((tm, LANES), lambda i: (i, MISC_KR1 // LANES)),
            pl.BlockSpec((tm, LANES), lambda i: (i, MISC_KR2 // LANES)),
            pl.BlockSpec((tm, 1), lambda i: (i, 0)),
            pl.BlockSpec((1, Q_LORA), const),
            pl.BlockSpec((1, KV_LORA), const),
            pl.BlockSpec(wq.shape, const),
            pl.BlockSpec(wkv.shape, const),
            pl.BlockSpec((1, LANES), const),
            pl.BlockSpec((1, LANES), const),
        ],
        out_specs=(pl.BlockSpec((tm, N_MLA * 2 * LANES), lambda i: (i, 0)),
                   pl.BlockSpec((tm, N_MLA * 2 * LANES), lambda i: (i, 0)),
                   pl.BlockSpec((tm, D_MLA), lambda i: (i, 0))),
        compiler_params=_params(("parallel",)),
        name="mla_prep",
    )(misc, misc, misc, misc, pos, gq.reshape(1, -1), gkv.reshape(1, -1), wq, wkv, invf, sgn)


def _gelu(x):
    return 0.5 * x * (1.0 + lax.erf(x * np.float32(np.sqrt(0.5))))


def _sgu_kernel(u_ref, v_ref, lng_ref, lnb_ref, w_ref, bias_ref, o_ref):
    tm = u_ref.shape[0]
    v = _gelu(v_ref[...])
    mu = jnp.mean(v, axis=-1, keepdims=True)
    xc = v - mu
    var = jnp.mean(xc * xc, axis=-1, keepdims=True)
    vb = (xc * lax.rsqrt(var + LN_EPS) * lng_ref[...] + lnb_ref[...]).astype(BF16)
    row = lax.broadcasted_iota(jnp.int32, (CHUNK, CHUNK), 0)
    col = lax.broadcasted_iota(jnp.int32, (CHUNK, LANES), 1)
    tril = lax.broadcasted_iota(jnp.int32, (CHUNK, CHUNK), 1) <= row
    first = col < SGU_CH
    ws = [jnp.where(tril, w_ref[g], 0.0).astype(BF16) for g in range(N_SGU)]
    zero = jnp.zeros((CHUNK, LANES), BF16)
    for c in range(tm // CHUNK):
        rows = slice(c * CHUNK, (c + 1) * CHUNK)
        for p in range(N_SGU // 2):
            cols = slice(p * LANES, (p + 1) * LANES)
            vp = vb[rows, cols]
            mixed = (jnp.dot(ws[2 * p], jnp.where(first, vp, zero), preferred_element_type=F32)
                     + jnp.dot(ws[2 * p + 1], jnp.where(first, zero, vp), preferred_element_type=F32))
            o_ref[rows, cols] = _gelu(u_ref[rows, cols]) * (mixed + bias_ref[:, cols])


def _sgu(misc, ln_g, ln_b, w_s, b_s, tm=512):
    t = misc.shape[0]
    bias = jnp.repeat(b_s.T, SGU_CH, axis=1)
    const2 = lambda i: (0, 0)
    return pl.pallas_call(
        _sgu_kernel,
        out_shape=jax.ShapeDtypeStruct((t, D_SGU), F32),
        grid=(t // tm,),
        in_specs=[
            pl.BlockSpec((tm, D_SGU), lambda i: (i, MISC_U // D_SGU)),
            pl.BlockSpec((tm, D_SGU), lambda i: (i, MISC_V // D_SGU)),
            pl.BlockSpec((1, D_SGU), const2),
            pl.BlockSpec((1, D_SGU), const2),
            pl.BlockSpec((N_SGU, CHUNK, CHUNK), lambda i: (0, 0, 0)),
            pl.BlockSpec((CHUNK, D_SGU), const2),
        ],
        out_specs=pl.BlockSpec((tm, D_SGU), lambda i: (i, 0)),
        compiler_params=_params(("parallel",)),
        name="sgu",
    )(misc, misc, ln_g.reshape(1, -1), ln_b.reshape(1, -1), w_s, bias)


def _pack_bf16_pairs(hb):
    n = hb.shape[1] // 2
    bits = pltpu.bitcast(hb.astype(F32), jnp.uint32)
    return (bits[:, n:] & jnp.uint32(0xFFFF0000)) | (bits[:, :n] >> 16)


def _unpack_bf16_pairs(xu):
    lo = pltpu.bitcast(xu << 16, F32).astype(BF16)
    hi = pltpu.bitcast(xu & jnp.uint32(0xFFFF0000), F32).astype(BF16)
    return jnp.concatenate([lo, hi], axis=1)


def _outproj_router_kernel(of_ref, os_ref, om_ref, og_ref, x_ref, ong_ref, wo_ref, nfg_ref, rw_ref,
                           rb_ref, xo_ref, h2_ref, idx_ref, gate_ref, rank_ref, cnt_ref, carry_ref):
    tm = x_ref.shape[0]

    @pl.when(pl.program_id(0) == 0)
    def _():
        carry_ref[...] = jnp.zeros_like(carry_ref)

    o = jnp.concatenate([_rms(r[...]) for r in (of_ref, os_ref, om_ref, og_ref)], axis=-1)
    o = (o * ong_ref[...]).astype(BF16)
    xn = x_ref[...] + jnp.dot(o, wo_ref[...], preferred_element_type=F32)
    xo_ref[...] = xn
    hb = (_rms(xn) * nfg_ref[...]).astype(BF16)
    h2_ref[...] = _pack_bf16_pairs(hb)

    lane = lax.broadcasted_iota(jnp.int32, (tm, LANES), 1)
    logits = jnp.dot(hb, rw_ref[...], preferred_element_type=F32) + rb_ref[...]
    vals = jnp.where(lane < N_EXPERTS, logits, -jnp.inf)
    sels, tops = [], []
    for _ in range(TOP_K):
        m = jnp.max(vals, axis=-1, keepdims=True)
        idx = jnp.min(jnp.where(vals == m, lane, LANES), axis=-1, keepdims=True)
        sel = lane == idx
        vals = jnp.where(sel, -jnp.inf, vals)
        sels.append(sel)
        tops.append((m, idx))
    es = [jnp.exp(m - tops[0][0]) for (m, _) in tops]
    den = es[0] + es[1] + es[2] + es[3]

    multi = sels[0] | sels[1] | sels[2] | sels[3]
    mh = jnp.where(multi, 1.0, 0.0)
    r = lax.broadcasted_iota(jnp.int32, (tm, tm), 0)
    c = lax.broadcasted_iota(jnp.int32, (tm, tm), 1)
    before = (c < r).astype(BF16)
    cnt = jnp.dot(before, mh.astype(BF16), preferred_element_type=F32) + carry_ref[...]
    total = cnt[tm - 1:tm, :] + mh[tm - 1:tm, :]
    carry_ref[...] = total
    cnt_ref[...] = jnp.broadcast_to(total, cnt_ref.shape).astype(jnp.int32)

    idx_o = jnp.zeros((tm, LANES), jnp.int32)
    gate_o = jnp.zeros((tm, LANES), F32)
    rank_o = jnp.zeros((tm, LANES), jnp.int32)
    for k in range(TOP_K):
        rank_k = jnp.sum(jnp.where(sels[k], cnt, 0.0), axis=-1, keepdims=True).astype(jnp.int32)
        idx_o = jnp.where(lane == k, tops[k][1], idx_o)
        gate_o = jnp.where(lane == k, es[k] / den, gate_o)
        rank_o = jnp.where(lane == k, rank_k, rank_o)
    idx_ref[...] = idx_o
    gate_ref[...] = gate_o
    rank_ref[...] = rank_o


def _outproj_router(o_fox, o_sb, o_mla, o_sgu, x, out_norm_g, w_o, norm_ffn_g, router_w, router_b, tm=256):
    t, d = x.shape
    dg = o_fox.shape[1]
    rw = jnp.zeros((d, LANES), BF16).at[:, :N_EXPERTS].set(router_w.astype(BF16))
    rb = jnp.zeros((1, LANES), F32).at[0, :N_EXPERTS].set(router_b)
    const = lambda i: (0, 0)
    grp = pl.BlockSpec((tm, dg), lambda i: (i, 0))
    lanes_out = pl.BlockSpec((tm, LANES), lambda i: (i, 0))
    return pl.pallas_call(
        _outproj_router_kernel,
        out_shape=(jax.ShapeDtypeStruct((t, d), F32),
                   jax.ShapeDtypeStruct((t, d // 2), jnp.uint32),
                   jax.ShapeDtypeStruct((t, LANES), jnp.int32),
                   jax.ShapeDtypeStruct((t, LANES), F32),
                   jax.ShapeDtypeStruct((t, LANES), jnp.int32),
                   jax.ShapeDtypeStruct((8, LANES), jnp.int32)),
        grid=(t // tm,),
        in_specs=[grp, grp, grp, grp,
                  pl.BlockSpec((tm, d), lambda i: (i, 0)),
                  pl.BlockSpec((1, 4 * dg), const),
                  pl.BlockSpec((4 * dg, d), const),
                  pl.BlockSpec((1, d), const),
                  pl.BlockSpec((d, LANES), const),
                  pl.BlockSpec((1, LANES), const)],
        out_specs=(pl.BlockSpec((tm, d), lambda i: (i, 0)),
                   pl.BlockSpec((tm, d // 2), lambda i: (i, 0)),
                   lanes_out, lanes_out, lanes_out,
                   pl.BlockSpec((8, LANES), const)),
        scratch_shapes=[pltpu.VMEM((1, LANES), F32)],
        compiler_params=_params(("arbitrary",)),
        name="outproj_router",
    )(o_fox, o_sb, o_mla, o_sgu, x, out_norm_g.reshape(1, -1), w_o, norm_ffn_g.reshape(1, -1), rw, rb)


def _gather_rows_kernel(nrows_ref, tok_hbm, src_hbm, dst_hbm, tok_smem, zeros, sems):
    i = pl.program_id(0)
    ch = tok_smem.shape[0]
    base = i * ch
    used = base < nrows_ref[0]

    @pl.when(jnp.logical_not(used))
    def _():
        zeros[...] = jnp.zeros_like(zeros)
        nz = zeros.shape[0]
        copies = [pltpu.make_async_copy(zeros, dst_hbm.at[pl.ds(base + j * nz, nz)], sems.at[1])
                  for j in range(ch // nz)]
        for cp in copies:
            cp.start()
        for cp in copies:
            cp.wait()

    @pl.when(used)
    def _():
        idx_copy = pltpu.make_async_copy(tok_hbm.at[pl.ds(base, ch)], tok_smem, sems.at[0])
        idx_copy.start()
        idx_copy.wait()

        def issue(r, _):
            pltpu.make_async_copy(src_hbm.at[pl.ds(tok_smem[r], 1)], dst_hbm.at[pl.ds(base + r, 1)],
                                  sems.at[1]).start()
            return 0

        lax.fori_loop(0, ch, issue, 0, unroll=8)
        pltpu.make_async_copy(src_hbm.at[pl.ds(0, ch)], dst_hbm.at[pl.ds(base, ch)], sems.at[1]).wait()


def _gather_rows(row_tok, src, nrows_used, nrows_max):
    ch = GATHER_CH
    return pl.pallas_call(
        _gather_rows_kernel,
        out_shape=jax.ShapeDtypeStruct((nrows_max, src.shape[1]), src.dtype),
        grid_spec=pltpu.PrefetchScalarGridSpec(
            num_scalar_prefetch=1,
            grid=(nrows_max // ch,),
            in_specs=[pl.BlockSpec(memory_space=pl.ANY), pl.BlockSpec(memory_space=pl.ANY)],
            out_specs=pl.BlockSpec(memory_space=pl.ANY),
            scratch_shapes=[pltpu.SMEM((ch,), jnp.int32), pltpu.VMEM((ch // 8, src.shape[1]), src.dtype),
                            pltpu.SemaphoreType.DMA((2,))],
        ),
        compiler_params=_params(("arbitrary",)),
        name="gather_rows",
    )(nrows_used, row_tok, src)


def _expert_kernel(sbe_ref, nval_ref, nused_ref, x_ref, wg_ref, bg_ref, wu_ref, bu_ref, wd_ref, bd_ref,
                   y_ref, wgb, wub, wdb):
    s = pl.program_id(0)
    f = pl.program_id(1)

    @pl.when(jnp.logical_and(s >= nused_ref[0], f == 0))
    def _():
        y_ref[...] = jnp.zeros_like(y_ref)

    @pl.when(s < nused_ref[0])
    def _():
        wgb[...] = wg_ref[...].astype(BF16)
        wub[...] = wu_ref[...].astype(BF16)
        wdb[...] = wd_ref[...].astype(BF16)

        @pl.when(f == 0)
        def _():
            y_ref[...] = jnp.broadcast_to(bd_ref[...], y_ref.shape)

        nsub = (nval_ref[s] + EXP_SUB - 1) // EXP_SUB

        def body(r, _):
            rows = pl.ds(pl.multiple_of(r * EXP_SUB, EXP_SUB), EXP_SUB)
            x = _unpack_bf16_pairs(x_ref[rows, :])
            g = jnp.minimum(jnp.dot(x, wgb[...], preferred_element_type=F32) + bg_ref[...], SWIGLU_LIMIT)
            u = jnp.clip(jnp.dot(x, wub[...], preferred_element_type=F32) + bu_ref[...],
                         -SWIGLU_LIMIT, SWIGLU_LIMIT)
            a = g * jax.nn.sigmoid(SWIGLU_ALPHA * g) * (u + 1.0)
            y_ref[rows, :] += jnp.dot(a.astype(BF16), wdb[...], preferred_element_type=F32)
            return 0

        lax.fori_loop(0, nsub, body, 0)


def _experts(sb_expert, sb_nvalid, n_used, xs, layer, w_gate, b_gate, w_up, b_up, w_down, b_down, nsb_max):
    d = w_gate.shape[2]
    dff = w_gate.shape[3]
    nf = dff // EXP_FF

    def xmap(s, f, sbe, nval, nused):
        return (jnp.minimum(s, nused[0] - 1), 0)

    def ff(s, f, nused):
        return jnp.where(s < nused[0], f, nf - 1)

    b_gate4 = b_gate.reshape(b_gate.shape[0], N_EXPERTS, 1, dff)
    b_up4 = b_up.reshape(b_up.shape[0], N_EXPERTS, 1, dff)
    b_down4 = b_down.reshape(b_down.shape[0], N_EXPERTS, 1, d)
    return pl.pallas_call(
        _expert_kernel,
        out_shape=jax.ShapeDtypeStruct((nsb_max * EXP_ROWS, d), F32),
        grid_spec=pltpu.PrefetchScalarGridSpec(
            num_scalar_prefetch=3,
            grid=(nsb_max, nf),
            in_specs=[
                pl.BlockSpec((EXP_ROWS, d // 2), xmap),
                pl.BlockSpec((None, None, d, EXP_FF), lambda s, f, sbe, nval, nused: (layer, sbe[s], 0, ff(s, f, nused))),
                pl.BlockSpec((None, None, 1, EXP_FF), lambda s, f, sbe, nval, nused: (layer, sbe[s], 0, ff(s, f, nused))),
                pl.BlockSpec((None, None, d, EXP_FF), lambda s, f, sbe, nval, nused: (layer, sbe[s], 0, ff(s, f, nused))),
                pl.BlockSpec((None, None, 1, EXP_FF), lambda s, f, sbe, nval, nused: (layer, sbe[s], 0, ff(s, f, nused))),
                pl.BlockSpec((None, None, EXP_FF, d), lambda s, f, sbe, nval, nused: (layer, sbe[s], ff(s, f, nused), 0)),
                pl.BlockSpec((None, None, 1, d), lambda s, f, sbe, nval, nused: (layer, sbe[s], 0, 0)),
            ],
            out_specs=pl.BlockSpec((EXP_ROWS, d), lambda s, f, sbe, nval, nused: (s, 0)),
            scratch_shapes=[pltpu.VMEM((d, EXP_FF), BF16), pltpu.VMEM((d, EXP_FF), BF16),
                            pltpu.VMEM((EXP_FF, d), BF16)],
        ),
        compiler_params=_params(("arbitrary", "arbitrary")),
        name="experts",
    )(sb_expert, sb_nvalid, n_used, xs, w_gate, b_gate4, w_up, b_up4, w_down, b_down4)


def _combine_kernel(dest_hbm, y_hbm, x_ref, gate_ref, fg_ref, o_ref, dest_smem, ybuf, sems, *, final_norm):
    i = pl.program_id(0)
    tm = x_ref.shape[0]
    n = tm * TOP_K
    idx_copy = pltpu.make_async_copy(dest_hbm.at[pl.ds(i * n, n)], dest_smem, sems.at[0])
    idx_copy.start()
    idx_copy.wait()

    def issue(t, _):
        for k in range(TOP_K):
            pltpu.make_async_copy(y_hbm.at[pl.ds(dest_smem[t * TOP_K + k], 1)],
                                  ybuf.at[pl.ds(k * tm + t, 1)], sems.at[1]).start()
        return 0

    lax.fori_loop(0, tm, issue, 0, unroll=2)
    pltpu.make_async_copy(y_hbm.at[pl.ds(0, n)], ybuf, sems.at[1]).wait()
    gate = gate_ref[...]
    acc = gate[:, 0:1] * ybuf[0:tm]
    for k in range(1, TOP_K):
        acc = acc + gate[:, k:k + 1] * ybuf[k * tm:(k + 1) * tm]
    out = x_ref[...] + acc
    if final_norm:
        out = _rms(out) * fg_ref[...]
    o_ref[...] = out


def _combine(dest_flat, y, x, gates, final_g, final_norm):
    t, d = x.shape
    tm = COMBINE_TM
    return pl.pallas_call(
        functools.partial(_combine_kernel, final_norm=final_norm),
        out_shape=jax.ShapeDtypeStruct((t, d), F32),
        grid=(t // tm,),
        in_specs=[
            pl.BlockSpec(memory_space=pl.ANY),
            pl.BlockSpec(memory_space=pl.ANY),
            pl.BlockSpec((tm, d), lambda i: (i, 0)),
            pl.BlockSpec((tm, LANES), lambda i: (i, 0)),
            pl.BlockSpec((1, d), lambda i: (0, 0)),
        ],
        out_specs=pl.BlockSpec((tm, d), lambda i: (i, 0)),
        scratch_shapes=[pltpu.SMEM((tm * TOP_K,), jnp.int32), pltpu.VMEM((TOP_K * tm, d), F32),
                        pltpu.SemaphoreType.DMA((2,))],
        compiler_params=_params(("arbitrary",)),
        name="combine",
    )(dest_flat, y, x, gates, final_g.reshape(1, d))


def _in_proj_weights(w_in):
    d = w_in.shape[0]
    o_gate = 3 * D_FOX
    o_sb = o_gate + N_FOX
    o_cq = o_sb + 3 * D_SB
    o_ckv = o_cq + Q_LORA
    o_kr = o_ckv + KV_LORA
    o_sgu = o_kr + MLA_ROPE
    half = MLA_ROPE // 2
    w_qkv = jnp.concatenate([w_in[:, :o_gate], w_in[:, o_sb:o_cq]], axis=1).astype(BF16)
    kr = w_in[:, o_kr:o_sgu]
    zpad = jnp.zeros((d, LANES - MLA_ROPE), w_in.dtype)
    w_misc = jnp.concatenate([
        w_in[:, o_sgu:o_sgu + 2 * D_SGU],
        w_in[:, o_cq:o_ckv],
        w_in[:, o_ckv:o_kr],
        kr, zpad,
        kr[:, half:], kr[:, :half], zpad,
        w_in[:, o_gate:o_sb], jnp.zeros((d, LANES - N_FOX), w_in.dtype),
    ], axis=1).astype(BF16)
    assert w_misc.shape[1] == MISC_W
    return w_qkv, w_misc


def _mla_weights(w_q_b, w_kv_b):
    half = MLA_ROPE // 2
    wq = w_q_b.reshape(Q_LORA, N_MLA, MLA_NOPE + MLA_ROPE)
    x1 = wq[:, :, MLA_NOPE:MLA_NOPE + half]
    x2 = wq[:, :, MLA_NOPE + half:]
    z = jnp.zeros((Q_LORA, N_MLA, LANES - MLA_ROPE), w_q_b.dtype)
    wq_r = jnp.concatenate([wq[:, :, :MLA_NOPE], x1, x2, z, x2, x1, z], axis=2)
    wq_r = wq_r.reshape(Q_LORA, N_MLA * 3 * LANES).astype(BF16)
    wkv = w_kv_b.reshape(KV_LORA, N_MLA, MLA_NOPE + MLA_V)
    wkv_r = jnp.concatenate([wkv[:, :, :MLA_NOPE].reshape(KV_LORA, -1),
                             wkv[:, :, MLA_NOPE:].reshape(KV_LORA, -1)], axis=1).astype(BF16)
    return wq_r, wkv_r


def _routing_tables(idx, rank, cnt, n_tokens):
    nsb_max = N_EXPERTS + (n_tokens * TOP_K) // EXP_ROWS
    nrows_max = -(-nsb_max * EXP_ROWS // GATHER_CH) * GATHER_CH
    counts = cnt[0, :N_EXPERTS]
    nsb_e = (counts + EXP_ROWS - 1) // EXP_ROWS
    sb_end = jnp.cumsum(nsb_e)
    sb_start = sb_end - nsb_e
    n_used = sb_end[-1]
    top_idx = idx[:, :TOP_K]
    dest = (sb_start * EXP_ROWS)[top_idx] + rank[:, :TOP_K]
    s_ids = jnp.arange(nsb_max, dtype=jnp.int32)
    sb_e = jnp.minimum(jnp.searchsorted(sb_end, s_ids, side='right'), N_EXPERTS - 1).astype(jnp.int32)
    last_e = sb_e[jnp.maximum(n_used - 1, 0)]
    sb_e = jnp.where(s_ids < n_used, sb_e, last_e)
    nval = jnp.clip(counts[sb_e] - (s_ids - sb_start[sb_e]) * EXP_ROWS, 0, EXP_ROWS)
    nval = jnp.where(s_ids < n_used, nval, 0).astype(jnp.int32)
    tok = jnp.broadcast_to(jnp.arange(n_tokens, dtype=jnp.int32)[:, None], (n_tokens, TOP_K))
    row_tok = jnp.zeros((nrows_max,), jnp.int32).at[dest.reshape(-1)].set(tok.reshape(-1))
    return (dest.reshape(-1).astype(jnp.int32), row_tok, sb_e, nval,
            n_used.reshape(1).astype(jnp.int32), nsb_max, nrows_max)


def _layer(x, pos, l, p, final_g, final_norm, batch, seq):
    t = x.shape[0]
    w_qkv, w_misc = _in_proj_weights(p["w_in"][l])
    qkv = _norm_matmul(x, p["norm_mix_g"][l], w_qkv, BF16, 512, 1024, "in_proj_qkv")
    misc = _norm_matmul(x, p["norm_mix_g"][l], w_misc, F32, 512, MISC_W, "in_proj_misc")

    ccol, crow = _fox_prep(misc, p["b_forget"][l], batch, seq)
    nkb = seq // ATT_TILE
    crow = crow.reshape(batch, nkb, N_FOX // 2, 2, ATT_TILE).transpose(0, 2, 1, 3, 4)
    o_fox = _fox_attention(qkv, ccol, crow, batch, seq)
    o_sb = _sb_attention(qkv, batch, seq)
    wq_r, wkv_r = _mla_weights(p["mla_w_q_b"][l], p["mla_w_kv_b"][l])
    q_m, k_m, v_m = _mla_prep(misc, pos, p["mla_q_norm_g"][l], p["mla_kv_norm_g"][l], wq_r, wkv_r)
    o_mla = _mla_attention(q_m, k_m, v_m, batch, seq)
    o_sgu = _sgu(misc, p["sgu_ln_g"][l], p["sgu_ln_b"][l], p["sgu_w_s"][l], p["sgu_b_s"][l])

    x_new, h2p, idx, gates, rank, cnt = _outproj_router(
        o_fox, o_sb, o_mla, o_sgu, x, p["out_norm_g"][l], p["w_o"][l].astype(BF16),
        p["norm_ffn_g"][l], p["router_w"][l], p["router_b"][l])

    dest, row_tok, sb_e, nval, n_used, nsb_max, nrows_max = _routing_tables(idx, rank, cnt, t)
    xs = _gather_rows(row_tok, h2p, n_used * EXP_ROWS, nrows_max)
    y = _experts(sb_e, nval, n_used, xs, l, p["w_gate"], p["b_gate"], p["w_up"], p["b_up"],
                 p["w_down"], p["b_down"], nsb_max)
    return _combine(dest, y, x_new, gates, final_g, final_norm)


def kernel(x, positions, norm_mix_g, w_in, b_forget, mla_q_norm_g, mla_kv_norm_g, mla_w_q_b, mla_w_kv_b,
           sgu_ln_g, sgu_ln_b, sgu_w_s, sgu_b_s, out_norm_g, w_o, norm_ffn_g, router_w, router_b, w_gate,
           b_gate, w_up, b_up, w_down, b_down, final_norm_g):
    batch, seq, d = x.shape
    depth = w_in.shape[0]
    p = dict(norm_mix_g=norm_mix_g, w_in=w_in, b_forget=b_forget, mla_q_norm_g=mla_q_norm_g,
             mla_kv_norm_g=mla_kv_norm_g, mla_w_q_b=mla_w_q_b, mla_w_kv_b=mla_w_kv_b, sgu_ln_g=sgu_ln_g,
             sgu_ln_b=sgu_ln_b, sgu_w_s=sgu_w_s, sgu_b_s=sgu_b_s, out_norm_g=out_norm_g, w_o=w_o,
             norm_ffn_g=norm_ffn_g, router_w=router_w, router_b=router_b, w_gate=w_gate, b_gate=b_gate,
             w_up=w_up, b_up=b_up, w_down=w_down, b_down=b_down)
    h = x.reshape(batch * seq, d)
    pos = positions.reshape(batch * seq, 1)
    for l in range(depth):
        h = _layer(h, pos, l, p, final_norm_g, l == depth - 1, batch, seq)
    return h.reshape(batch, seq, d)
```

```python
import functools

import numpy as np
import jax
import jax.numpy as jnp
from jax import lax
from jax.experimental import pallas as pl
from jax.experimental.pallas import tpu as pltpu

HEAD_DIM = 64
N_FOX = 8
N_SB = 8
N_MLA = 4
MLA_NOPE = 128
MLA_ROPE = 64
MLA_V = 128
Q_LORA = 512
KV_LORA = 256
N_SGU = 8
SGU_CH = 64
CHUNK = 128
ROPE_THETA = 10000.0
D_FOX = N_FOX * HEAD_DIM
D_SB = N_SB * HEAD_DIM
D_MLA = N_MLA * MLA_V
D_SGU = N_SGU * SGU_CH
N_EXPERTS = 32
TOP_K = 4
SWIGLU_LIMIT = 7.0
SWIGLU_ALPHA = 1.702
RMS_EPS = 1e-6
LN_EPS = 1e-5

LANES = 128
VMEM_LIMIT = 56 * 1024 * 1024

ATT_TILE = 256
FOX_HEADS = 4
EXP_ROWS = 1280
EXP_SUB = 256
EXP_FF = 256
EXP_DN = 512
GATHER_CH = 1024
COMBINE_TM = 256

F32 = jnp.float32
BF16 = jnp.bfloat16

MISC_U = 0
MISC_V = 512
MISC_CQ = 1024
MISC_CKV = 1536
MISC_KR1 = 1792
MISC_KR2 = 1920
MISC_GATE = 2048
MISC_W = 2176


def _params(sem, vmem=VMEM_LIMIT):
    return pltpu.CompilerParams(dimension_semantics=sem, vmem_limit_bytes=vmem)


def _split3(x):
    hi = x.astype(BF16)
    r1 = x - hi.astype(F32)
    mid = r1.astype(BF16)
    lo = (r1 - mid.astype(F32)).astype(BF16)
    return hi, mid, lo


def _log_sigmoid_pair(z):
    t = jnp.log1p(jnp.exp(-jnp.abs(z)))
    return jnp.minimum(z, 0.0) - t, jnp.minimum(-z, 0.0) - t


def _rms(x):
    return x * lax.rsqrt(jnp.mean(x * x, axis=-1, keepdims=True) + RMS_EPS)


def _norm_matmul_kernel(x_ref, g_ref, w_ref, o_ref, xn_ref):
    @pl.when(pl.program_id(1) == 0)
    def _():
        xn_ref[...] = (_rms(x_ref[...]) * g_ref[...]).astype(BF16)

    o_ref[...] = jnp.dot(xn_ref[...], w_ref[...], preferred_element_type=F32).astype(o_ref.dtype)


def _norm_matmul(x, g, w, out_dtype, tm, tn, name):
    t, d = x.shape
    n = w.shape[1]
    return pl.pallas_call(
        _norm_matmul_kernel,
        out_shape=jax.ShapeDtypeStruct((t, n), out_dtype),
        grid=(t // tm, n // tn),
        in_specs=[
            pl.BlockSpec((tm, d), lambda i, j: (i, 0)),
            pl.BlockSpec((1, d), lambda i, j: (0, 0)),
            pl.BlockSpec((d, tn), lambda i, j: (0, j)),
        ],
        out_specs=pl.BlockSpec((tm, tn), lambda i, j: (i, j)),
        scratch_shapes=[pltpu.VMEM((tm, d), BF16)],
        compiler_params=_params(("parallel", "arbitrary")),
        name=name,
    )(x, g.reshape(1, d), w)


def _fox_prep_kernel(gate_ref, bf_ref, crow_ref):
    s_len = gate_ref.shape[0]
    bl = ATT_TILE
    r = lax.broadcasted_iota(jnp.int32, (bl, bl), 0)
    c = lax.broadcasted_iota(jnp.int32, (bl, bl), 1)
    tri = (c <= r).astype(BF16)
    carry = jnp.zeros((1, LANES), F32)
    for i in range(s_len // bl):
        z = gate_ref[i * bl:(i + 1) * bl, :] + bf_ref[...]
        lf, _ = _log_sigmoid_pair(z)
        hi, mid, lo = _split3(lf)
        cs = (jnp.dot(tri, hi, preferred_element_type=F32)
              + jnp.dot(tri, mid, preferred_element_type=F32)
              + jnp.dot(tri, lo, preferred_element_type=F32)) + carry
        crow_ref[i] = cs.T[0:N_FOX, :]
        carry = cs[bl - 1:bl, :]


def _fox_prep(misc, b_forget, batch, seq):
    nkb = seq // ATT_TILE
    bf = jnp.zeros((1, LANES), F32).at[0, :N_FOX].set(b_forget)
    return pl.pallas_call(
        _fox_prep_kernel,
        out_shape=jax.ShapeDtypeStruct((batch, nkb, N_FOX, ATT_TILE), F32),
        grid=(batch,),
        in_specs=[
            pl.BlockSpec((seq, LANES), lambda b: (b, MISC_GATE // LANES)),
            pl.BlockSpec((1, LANES), lambda b: (0, 0)),
        ],
        out_specs=pl.BlockSpec((None, nkb, N_FOX, ATT_TILE), lambda b: (b, 0, 0, 0)),
        compiler_params=_params(("parallel",)),
        name="fox_prep",
    )(misc, bf)


def _causal_mask(tq):
    row = lax.broadcasted_iota(jnp.int32, (tq, tq), 0)
    col = lax.broadcasted_iota(jnp.int32, (tq, tq), 1)
    return col <= row, col < row


def _softmax_sweep(qs, k_of, v_of, bias_of, scale, qi, tq):
    nh = len(qs)
    incl, _ = _causal_mask(tq)

    def step(kb, carry, diag):
        ks = pl.multiple_of(kb * tq, tq)
        out = []
        for h in range(nh):
            m, l, acc = carry[h]
            s = lax.dot_general(qs[h], k_of(h, ks), (((1,), (1,)), ((), ())),
                                preferred_element_type=F32)
            if scale is not None:
                s = s * scale
            b = bias_of(h, kb)
            if b is not None:
                s = s + b
            if diag:
                s = jnp.where(incl, s, -jnp.inf)
            m_new = jnp.maximum(m, jnp.max(s, axis=-1, keepdims=True))
            alpha = jnp.exp(m - m_new)
            p = jnp.exp(s - m_new)
            l = alpha * l + jnp.sum(p, axis=-1, keepdims=True)
            acc = alpha * acc + jnp.dot(p.astype(BF16), v_of(h, ks), preferred_element_type=F32)
            out.append((m_new, l, acc))
        return tuple(out)

    dv = v_of(0, 0).shape[-1]
    init = tuple((jnp.full((tq, 1), -jnp.inf, F32), jnp.zeros((tq, 1), F32),
                  jnp.zeros((tq, dv), F32)) for _ in range(nh))
    carry = lax.fori_loop(0, qi, lambda kb, c: step(kb, c, False), init)
    carry = step(qi, carry, True)
    return [acc / l for (_, l, acc) in carry]


def _fox_attn_kernel(q_ref, k_ref, v_ref, crow_ref, o_ref):
    tq = q_ref.shape[0]
    qi = pl.program_id(2)
    q = q_ref[...] * jnp.asarray(HEAD_DIM ** -0.5, BF16)
    qs = [q[:, j * HEAD_DIM:(j + 1) * HEAD_DIM] for j in range(FOX_HEADS)]

    def k_of(h, ks):
        return k_ref[pl.ds(ks, tq), h * HEAD_DIM:(h + 1) * HEAD_DIM]

    def v_of(h, ks):
        return v_ref[pl.ds(ks, tq), h * HEAD_DIM:(h + 1) * HEAD_DIM]

    def bias_of(h, kb):
        return -crow_ref[kb][h:h + 1, :]

    outs = _softmax_sweep(qs, k_of, v_of, bias_of, None, qi, tq)
    o_ref[...] = jnp.concatenate(outs, axis=-1)


def _fox_attention(qkv, crow, batch, seq):
    t = qkv.shape[0]
    tq = ATT_TILE
    nq = seq // tq
    npair = N_FOX // FOX_HEADS
    w = FOX_HEADS * HEAD_DIM
    return pl.pallas_call(
        _fox_attn_kernel,
        out_shape=jax.ShapeDtypeStruct((t, D_FOX), F32),
        grid=(batch, npair, nq),
        in_specs=[
            pl.BlockSpec((tq, w), lambda b, h, i: (b * nq + i, h)),
            pl.BlockSpec((seq, w), lambda b, h, i: (b, npair + h)),
            pl.BlockSpec((seq, w), lambda b, h, i: (b, 2 * npair + h)),
            pl.BlockSpec((None, None, nq, FOX_HEADS, tq), lambda b, h, i: (b, h, 0, 0, 0)),
        ],
        out_specs=pl.BlockSpec((tq, w), lambda b, h, i: (b * nq + i, h)),
        compiler_params=_params(("parallel", "parallel", "arbitrary")),
        name="fox_attn",
    )(qkv, qkv, qkv, crow)


def _mla_attn_kernel(q_ref, k_ref, v_ref, o_ref):
    tq = q_ref.shape[0]
    qi = pl.program_id(2)
    scale = (MLA_NOPE + MLA_ROPE) ** -0.5
    outs = _softmax_sweep([q_ref[...]],
                          lambda h, ks: k_ref[pl.ds(ks, tq), :],
                          lambda h, ks: v_ref[pl.ds(ks, tq), :],
                          lambda h, kb: None, scale, qi, tq)
    o_ref[...] = outs[0]


def _mla_attention(q, k, v, batch, seq):
    t = q.shape[0]
    tq = ATT_TILE
    nq = seq // tq
    dk = 2 * LANES
    return pl.pallas_call(
        _mla_attn_kernel,
        out_shape=jax.ShapeDtypeStruct((t, D_MLA), F32),
        grid=(batch, N_MLA, nq),
        in_specs=[
            pl.BlockSpec((tq, dk), lambda b, h, i: (b * nq + i, h)),
            pl.BlockSpec((seq, dk), lambda b, h, i: (b, h)),
            pl.BlockSpec((seq, MLA_V), lambda b, h, i: (b, h)),
        ],
        out_specs=pl.BlockSpec((tq, MLA_V), lambda b, h, i: (b * nq + i, h)),
        compiler_params=_params(("parallel", "parallel", "arbitrary")),
        name="mla_attn",
    )(q, k, v)


def _sb_attn_kernel(q_ref, k_ref, v_ref, o_ref):
    tq = q_ref.shape[0]
    qi = pl.program_id(2)
    q = q_ref[...] * jnp.asarray(HEAD_DIM ** -0.5, BF16)
    qs = [q[:, j * HEAD_DIM:(j + 1) * HEAD_DIM] for j in range(2)]
    _, strict = _causal_mask(tq)
    row = lax.broadcasted_iota(jnp.int32, (tq, tq), 0)
    col = lax.broadcasted_iota(jnp.int32, (tq, tq), 1)
    later = (row > col).astype(BF16)

    def step(kb, carry, diag):
        ks = pl.multiple_of(kb * tq, tq)
        out = []
        for h in range(2):
            rem, acc = carry[h]
            k = k_ref[pl.ds(ks, tq), h * HEAD_DIM:(h + 1) * HEAD_DIM]
            v = v_ref[pl.ds(ks, tq), h * HEAD_DIM:(h + 1) * HEAD_DIM]
            z = lax.dot_general(qs[h], k, (((1,), (1,)), ((), ())), preferred_element_type=F32)
            ls = jnp.minimum(z, 0.0) - jnp.log(1.0 + jnp.exp(-jnp.abs(z)))
            lr = ls - z
            if diag:
                lr = jnp.where(strict, lr, 0.0)
            hi = lr.astype(BF16)
            lo = (lr - hi.astype(F32)).astype(BF16)
            after = (jnp.dot(hi, later, preferred_element_type=F32)
                     + jnp.dot(lo, later, preferred_element_type=F32)) + rem
            a = jnp.exp(ls + after)
            if diag:
                a = jnp.where(strict, a, 0.0)
            acc = acc + jnp.dot(a.astype(BF16), v, preferred_element_type=F32)
            rem = after[:, 0:1] + lr[:, 0:1]
            out.append((rem, acc))
        return tuple(out)

    init = tuple((jnp.zeros((tq, 1), F32), jnp.zeros((tq, HEAD_DIM), F32)) for _ in range(2))
    carry = step(qi, init, True)
    carry = lax.fori_loop(0, qi, lambda i, c: step(qi - 1 - i, c, False), carry)
    o_ref[...] = jnp.concatenate([acc for (_, acc) in carry], axis=-1)


def _sb_attention(qkv, batch, seq):
    t = qkv.shape[0]
    tq = ATT_TILE
    nq = seq // tq
    npair = N_SB // 2
    base = 3 * D_FOX // LANES
    return pl.pallas_call(
        _sb_attn_kernel,
        out_shape=jax.ShapeDtypeStruct((t, D_SB), F32),
        grid=(batch, npair, nq),
        in_specs=[
            pl.BlockSpec((tq, LANES), lambda b, h, i: (b * nq + i, base + h)),
            pl.BlockSpec((seq, LANES), lambda b, h, i: (b, base + npair + h)),
            pl.BlockSpec((seq, LANES), lambda b, h, i: (b, base + 2 * npair + h)),
        ],
        out_specs=pl.BlockSpec((tq, LANES), lambda b, h, i: (b * nq + i, h)),
        compiler_params=_params(("parallel", "parallel", "arbitrary")),
        name="sb_attn",
    )(qkv, qkv, qkv)


def _mla_prep_kernel(cq_ref, ckv_ref, kr1_ref, kr2_ref, pos_ref, gq_ref, gkv_ref, wq_ref, wkv_ref,
                     invf_ref, sgn_ref, q_out, k_out, v_out):
    ang = pos_ref[...].astype(F32) * invf_ref[...]
    cosv = jnp.cos(ang)
    sinv = jnp.sin(ang) * sgn_ref[...]
    qn = (_rms(cq_ref[...]) * gq_ref[...]).astype(BF16)
    qa = jnp.dot(qn, wq_ref[...], preferred_element_type=F32)
    for h in range(N_MLA):
        o = h * 3 * LANES
        pe = qa[:, o + LANES:o + 2 * LANES] * cosv + qa[:, o + 2 * LANES:o + 3 * LANES] * sinv
        q_out[:, h * 2 * LANES:h * 2 * LANES + LANES] = qa[:, o:o + LANES].astype(BF16)
        q_out[:, h * 2 * LANES + LANES:(h + 1) * 2 * LANES] = pe.astype(BF16)
    kvn = (_rms(ckv_ref[...]) * gkv_ref[...]).astype(BF16)
    kva = jnp.dot(kvn, wkv_ref[...], preferred_element_type=F32)
    kpe = (kr1_ref[...] * cosv + kr2_ref[...] * sinv).astype(BF16)
    for h in range(N_MLA):
        k_out[:, h * 2 * LANES:h * 2 * LANES + LANES] = kva[:, h * LANES:(h + 1) * LANES].astype(BF16)
        k_out[:, h * 2 * LANES + LANES:(h + 1) * 2 * LANES] = kpe
    v_out[...] = kva[:, N_MLA * MLA_NOPE:].astype(BF16)


def _mla_prep(misc, pos, gq, gkv, wq, wkv, tm=512):
    t = misc.shape[0]
    half = MLA_ROPE // 2
    inv_freq = ROPE_THETA ** (-jnp.arange(half, dtype=F32) / half)
    invf = jnp.tile(inv_freq, LANES // half).reshape(1, LANES)
    sgn = jnp.tile(jnp.concatenate([-jnp.ones((half,), F32), jnp.ones((half,), F32)]),
                   LANES // MLA_ROPE).reshape(1, LANES)
    const = lambda i: (0, 0)
    return pl.pallas_call(
        _mla_prep_kernel,
        out_shape=(jax.ShapeDtypeStruct((t, N_MLA * 2 * LANES), BF16),
                   jax.ShapeDtypeStruct((t, N_MLA * 2 * LANES), BF16),
                   jax.ShapeDtypeStruct((t, D_MLA), BF16)),
        grid=(t // tm,),
        in_specs=[
            pl.BlockSpec((tm, Q_LORA), lambda i: (i, MISC_CQ // Q_LORA)),
            pl.BlockSpec((tm, KV_LORA), lambda i: (i, MISC_CKV // KV_LORA)),
            pl.BlockSpec((tm, LANES), lambda i: (i, MISC_KR1 // LANES)),
            pl.BlockSpec((tm, LANES), lambda i: (i, MISC_KR2 // LANES)),
            pl.BlockSpec((tm, 1), lambda i: (i, 0)),
            pl.BlockSpec((1, Q_LORA), const),
            pl.BlockSpec((1, KV_LORA), const),
            pl.BlockSpec(wq.shape, const),
            pl.BlockSpec(wkv.shape, const),
            pl.BlockSpec((1, LANES), const),
            pl.BlockSpec((1, LANES), const),
        ],
        out_specs=(pl.BlockSpec((tm, N_MLA * 2 * LANES), lambda i: (i, 0)),
                   pl.BlockSpec((tm, N_MLA * 2 * LANES), lambda i: (i, 0)),
                   pl.BlockSpec((tm, D_MLA), lambda i: (i, 0))),
        compiler_params=_params(("parallel",)),
        name="mla_prep",
    )(misc, misc, misc, misc, pos, gq.reshape(1, -1), gkv.reshape(1, -1), wq, wkv, invf, sgn)


def _gelu(x):
    return 0.5 * x * (1.0 + lax.erf(x * np.float32(np.sqrt(0.5))))


def _sgu_kernel(u_ref, v_ref, lng_ref, lnb_ref, w_ref, bias_ref, o_ref):
    tm = u_ref.shape[0]
    v = _gelu(v_ref[...])
    mu = jnp.mean(v, axis=-1, keepdims=True)
    xc = v - mu
    var = jnp.mean(xc * xc, axis=-1, keepdims=True)
    vb = (xc * lax.rsqrt(var + LN_EPS) * lng_ref[...] + lnb_ref[...]).astype(BF16)
    row = lax.broadcasted_iota(jnp.int32, (CHUNK, CHUNK), 0)
    col = lax.broadcasted_iota(jnp.int32, (CHUNK, LANES), 1)
    tril = lax.broadcasted_iota(jnp.int32, (CHUNK, CHUNK), 1) <= row
    first = col < SGU_CH
    ws = [jnp.where(tril, w_ref[g], 0.0).astype(BF16) for g in range(N_SGU)]
    zero = jnp.zeros((CHUNK, LANES), BF16)
    for c in range(tm // CHUNK):
        rows = slice(c * CHUNK, (c + 1) * CHUNK)
        for p in range(N_SGU // 2):
            cols = slice(p * LANES, (p + 1) * LANES)
            vp = vb[rows, cols]
            mixed = (jnp.dot(ws[2 * p], jnp.where(first, vp, zero), preferred_element_type=F32)
                     + jnp.dot(ws[2 * p + 1], jnp.where(first, zero, vp), preferred_element_type=F32))
            o_ref[rows, cols] = _gelu(u_ref[rows, cols]) * (mixed + bias_ref[:, cols])


def _sgu(misc, ln_g, ln_b, w_s, b_s, tm=512):
    t = misc.shape[0]
    bias = jnp.repeat(b_s.T, SGU_CH, axis=1)
    const2 = lambda i: (0, 0)
    return pl.pallas_call(
        _sgu_kernel,
        out_shape=jax.ShapeDtypeStruct((t, D_SGU), F32),
        grid=(t // tm,),
        in_specs=[
            pl.BlockSpec((tm, D_SGU), lambda i: (i, MISC_U // D_SGU)),
            pl.BlockSpec((tm, D_SGU), lambda i: (i, MISC_V // D_SGU)),
            pl.BlockSpec((1, D_SGU), const2),
            pl.BlockSpec((1, D_SGU), const2),
            pl.BlockSpec((N_SGU, CHUNK, CHUNK), lambda i: (0, 0, 0)),
            pl.BlockSpec((CHUNK, D_SGU), const2),
        ],
        out_specs=pl.BlockSpec((tm, D_SGU), lambda i: (i, 0)),
        compiler_params=_params(("parallel",)),
        name="sgu",
    )(misc, misc, ln_g.reshape(1, -1), ln_b.reshape(1, -1), w_s, bias)


def _pack_bf16_pairs(hb):
    n = hb.shape[1] // 2
    bits = pltpu.bitcast(hb.astype(F32), jnp.uint32)
    return (bits[:, n:] & jnp.uint32(0xFFFF0000)) | (bits[:, :n] >> 16)


def _unpack_bf16_pairs(xu):
    lo = pltpu.bitcast(xu << 16, F32).astype(BF16)
    hi = pltpu.bitcast(xu & jnp.uint32(0xFFFF0000), F32).astype(BF16)
    return jnp.concatenate([lo, hi], axis=1)


def _outproj_router_kernel(of_ref, os_ref, om_ref, og_ref, x_ref, ong_ref, wo_ref, nfg_ref, rw_ref,
                           rb_ref, xo_ref, h2_ref, idx_ref, gate_ref, rank_ref, cnt_ref, carry_ref):
    tm = x_ref.shape[0]

    @pl.when(pl.program_id(0) == 0)
    def _():
        carry_ref[...] = jnp.zeros_like(carry_ref)

    o = jnp.concatenate([_rms(r[...]) for r in (of_ref, os_ref, om_ref, og_ref)], axis=-1)
    o = (o * ong_ref[...]).astype(BF16)
    xn = x_ref[...] + jnp.dot(o, wo_ref[...], preferred_element_type=F32)
    xo_ref[...] = xn
    hb = (_rms(xn) * nfg_ref[...]).astype(BF16)
    h2_ref[...] = _pack_bf16_pairs(hb)

    lane = lax.broadcasted_iota(jnp.int32, (tm, LANES), 1)
    logits = jnp.dot(hb, rw_ref[...], preferred_element_type=F32) + rb_ref[...]
    vals = jnp.where(lane < N_EXPERTS, logits, -jnp.inf)
    sels, tops = [], []
    for _ in range(TOP_K):
        m = jnp.max(vals, axis=-1, keepdims=True)
        idx = jnp.min(jnp.where(vals == m, lane, LANES), axis=-1, keepdims=True)
        sel = lane == idx
        vals = jnp.where(sel, -jnp.inf, vals)
        sels.append(sel)
        tops.append((m, idx))
    es = [jnp.exp(m - tops[0][0]) for (m, _) in tops]
    den = es[0] + es[1] + es[2] + es[3]

    multi = sels[0] | sels[1] | sels[2] | sels[3]
    mh = jnp.where(multi, 1.0, 0.0)
    r = lax.broadcasted_iota(jnp.int32, (tm, tm), 0)
    c = lax.broadcasted_iota(jnp.int32, (tm, tm), 1)
    before = (c < r).astype(BF16)
    cnt = jnp.dot(before, mh.astype(BF16), preferred_element_type=F32) + carry_ref[...]
    total = cnt[tm - 1:tm, :] + mh[tm - 1:tm, :]
    carry_ref[...] = total
    cnt_ref[...] = jnp.broadcast_to(total, cnt_ref.shape).astype(jnp.int32)

    idx_o = jnp.zeros((tm, LANES), jnp.int32)
    gate_o = jnp.zeros((tm, LANES), F32)
    rank_o = jnp.zeros((tm, LANES), jnp.int32)
    for k in range(TOP_K):
        rank_k = jnp.sum(jnp.where(sels[k], cnt, 0.0), axis=-1, keepdims=True).astype(jnp.int32)
        idx_o = jnp.where(lane == k, tops[k][1], idx_o)
        gate_o = jnp.where(lane == k, es[k] / den, gate_o)
        rank_o = jnp.where(lane == k, rank_k, rank_o)
    idx_ref[...] = idx_o
    gate_ref[...] = gate_o
    rank_ref[...] = rank_o


def _outproj_router(o_fox, o_sb, o_mla, o_sgu, x, out_norm_g, w_o, norm_ffn_g, router_w, router_b, tm=256):
    t, d = x.shape
    dg = o_fox.shape[1]
    rw = jnp.zeros((d, LANES), BF16).at[:, :N_EXPERTS].set(router_w.astype(BF16))
    rb = jnp.zeros((1, LANES), F32).at[0, :N_EXPERTS].set(router_b)
    const = lambda i: (0, 0)
    grp = pl.BlockSpec((tm, dg), lambda i: (i, 0))
    lanes_out = pl.BlockSpec((tm, LANES), lambda i: (i, 0))
    return pl.pallas_call(
        _outproj_router_kernel,
        out_shape=(jax.ShapeDtypeStruct((t, d), F32),
                   jax.ShapeDtypeStruct((t, d // 2), jnp.uint32),
                   jax.ShapeDtypeStruct((t, LANES), jnp.int32),
                   jax.ShapeDtypeStruct((t, LANES), F32),
                   jax.ShapeDtypeStruct((t, LANES), jnp.int32),
                   jax.ShapeDtypeStruct((8, LANES), jnp.int32)),
        grid=(t // tm,),
        in_specs=[grp, grp, grp, grp,
                  pl.BlockSpec((tm, d), lambda i: (i, 0)),
                  pl.BlockSpec((1, 4 * dg), const),
                  pl.BlockSpec((4 * dg, d), const),
                  pl.BlockSpec((1, d), const),
                  pl.BlockSpec((d, LANES), const),
                  pl.BlockSpec((1, LANES), const)],
        out_specs=(pl.BlockSpec((tm, d), lambda i: (i, 0)),
                   pl.BlockSpec((tm, d // 2), lambda i: (i, 0)),
                   lanes_out, lanes_out, lanes_out,
                   pl.BlockSpec((8, LANES), const)),
        scratch_shapes=[pltpu.VMEM((1, LANES), F32)],
        compiler_params=_params(("arbitrary",)),
        name="outproj_router",
    )(o_fox, o_sb, o_mla, o_sgu, x, out_norm_g.reshape(1, -1), w_o, norm_ffn_g.reshape(1, -1), rw, rb)


def _gather_rows_kernel(nrows_ref, tok_hbm, src_hbm, o_ref, tok_smem, rows, sems):
    i = pl.program_id(0)
    ch = tok_smem.shape[0]
    base = i * ch
    used = base < nrows_ref[0]

    @pl.when(jnp.logical_not(used))
    def _():
        o_ref[...] = jnp.zeros_like(o_ref)

    @pl.when(used)
    def _():
        idx_copy = pltpu.make_async_copy(tok_hbm.at[pl.ds(base, ch)], tok_smem, sems.at[0])
        idx_copy.start()
        idx_copy.wait()

        def issue(r, _):
            pltpu.make_async_copy(src_hbm.at[pl.ds(tok_smem[r], 1)], rows.at[pl.ds(r, 1)], sems.at[1]).start()
            return 0

        lax.fori_loop(0, ch, issue, 0, unroll=8)
        pltpu.make_async_copy(src_hbm.at[pl.ds(0, ch)], rows, sems.at[1]).wait()
        o_ref[...] = _unpack_bf16_pairs(rows[...])


def _gather_rows(row_tok, src, nrows_used, nrows_max):
    ch = GATHER_CH
    d = 2 * src.shape[1]
    return pl.pallas_call(
        _gather_rows_kernel,
        out_shape=jax.ShapeDtypeStruct((nrows_max, d), BF16),
        grid_spec=pltpu.PrefetchScalarGridSpec(
            num_scalar_prefetch=1,
            grid=(nrows_max // ch,),
            in_specs=[pl.BlockSpec(memory_space=pl.ANY), pl.BlockSpec(memory_space=pl.ANY)],
            out_specs=pl.BlockSpec((ch, d), lambda i, n: (i, 0)),
            scratch_shapes=[pltpu.SMEM((ch,), jnp.int32), pltpu.VMEM((ch, src.shape[1]), src.dtype),
                            pltpu.SemaphoreType.DMA((2,))],
        ),
        compiler_params=_params(("arbitrary",)),
        name="gather_rows",
    )(nrows_used, row_tok, src)


def _expert_kernel(sbe_ref, nval_ref, nused_ref, x_ref, wg_ref, bg_ref, wu_ref, bu_ref, wd_ref, bd_ref, y_ref):
    s = pl.program_id(0)
    f = pl.program_id(1)
    d = y_ref.shape[1]

    @pl.when(jnp.logical_and(s >= nused_ref[0], f == 0))
    def _():
        y_ref[...] = jnp.zeros_like(y_ref)

    @pl.when(s < nused_ref[0])
    def _():
        @pl.when(f == 0)
        def _():
            y_ref[...] = jnp.broadcast_to(bd_ref[...], y_ref.shape)

        nsub = (nval_ref[s] + EXP_SUB - 1) // EXP_SUB

        for n in range(1, EXP_ROWS // EXP_SUB + 1):
            @pl.when(nsub == n)
            def _(m=n * EXP_SUB):
                x = x_ref[0:m, :]
                g = jnp.dot(x, wg_ref[...].astype(BF16), preferred_element_type=F32) + bg_ref[...]
                g = jnp.minimum(g, SWIGLU_LIMIT)
                u = jnp.dot(x, wu_ref[...].astype(BF16), preferred_element_type=F32) + bu_ref[...]
                u = jnp.clip(u, -SWIGLU_LIMIT, SWIGLU_LIMIT)
                a = (g * jax.nn.sigmoid(SWIGLU_ALPHA * g) * (u + 1.0)).astype(BF16)
                wd = wd_ref[...].astype(BF16)
                for c in range(d // EXP_DN):
                    cols = slice(c * EXP_DN, (c + 1) * EXP_DN)
                    y_ref[0:m, cols] += jnp.dot(a, wd[:, cols], preferred_element_type=F32)


def _experts(sb_expert, sb_nvalid, n_used, xs, layer, w_gate, b_gate, w_up, b_up, w_down, b_down, nsb_max):
    d = w_gate.shape[2]
    dff = w_gate.shape[3]
    nf = dff // EXP_FF

    def xmap(s, f, sbe, nval, nused):
        return (jnp.minimum(s, nused[0] - 1), 0)

    def ff(s, f, nused):
        return jnp.where(s < nused[0], f, nf - 1)

    b_gate4 = b_gate.reshape(b_gate.shape[0], N_EXPERTS, 1, dff)
    b_up4 = b_up.reshape(b_up.shape[0], N_EXPERTS, 1, dff)
    b_down4 = b_down.reshape(b_down.shape[0], N_EXPERTS, 1, d)
    return pl.pallas_call(
        _expert_kernel,
        out_shape=jax.ShapeDtypeStruct((nsb_max * EXP_ROWS, d), F32),
        grid_spec=pltpu.PrefetchScalarGridSpec(
            num_scalar_prefetch=3,
            grid=(nsb_max, nf),
            in_specs=[
                pl.BlockSpec((EXP_ROWS, d), xmap),
                pl.BlockSpec((None, None, d, EXP_FF), lambda s, f, sbe, nval, nused: (layer, sbe[s], 0, ff(s, f, nused))),
                pl.BlockSpec((None, None, 1, EXP_FF), lambda s, f, sbe, nval, nused: (layer, sbe[s], 0, ff(s, f, nused))),
                pl.BlockSpec((None, None, d, EXP_FF), lambda s, f, sbe, nval, nused: (layer, sbe[s], 0, ff(s, f, nused))),
                pl.BlockSpec((None, None, 1, EXP_FF), lambda s, f, sbe, nval, nused: (layer, sbe[s], 0, ff(s, f, nused))),
                pl.BlockSpec((None, None, EXP_FF, d), lambda s, f, sbe, nval, nused: (layer, sbe[s], ff(s, f, nused), 0)),
                pl.BlockSpec((None, None, 1, d), lambda s, f, sbe, nval, nused: (layer, sbe[s], 0, 0)),
            ],
            out_specs=pl.BlockSpec((EXP_ROWS, d), lambda s, f, sbe, nval, nused: (s, 0)),
        ),
        compiler_params=_params(("arbitrary", "arbitrary")),
        name="experts",
    )(sb_expert, sb_nvalid, n_used, xs, w_gate, b_gate4, w_up, b_up4, w_down, b_down4)


def _combine_kernel(dest_hbm, y_hbm, x_ref, gate_ref, fg_ref, o_ref, dest_smem, ybuf, sems, *, final_norm):
    i = pl.program_id(0)
    tm = x_ref.shape[0]
    n = tm * TOP_K
    idx_copy = pltpu.make_async_copy(dest_hbm.at[pl.ds(i * n, n)], dest_smem, sems.at[0])
    idx_copy.start()
    idx_copy.wait()

    def issue(t, _):
        for k in range(TOP_K):
            pltpu.make_async_copy(y_hbm.at[pl.ds(dest_smem[t * TOP_K + k], 1)],
                                  ybuf.at[pl.ds(k * tm + t, 1)], sems.at[1]).start()
        return 0

    lax.fori_loop(0, tm, issue, 0, unroll=2)
    pltpu.make_async_copy(y_hbm.at[pl.ds(0, n)], ybuf, sems.at[1]).wait()
    gate = gate_ref[...]
    acc = gate[:, 0:1] * ybuf[0:tm]
    for k in range(1, TOP_K):
        acc = acc + gate[:, k:k + 1] * ybuf[k * tm:(k + 1) * tm]
    out = x_ref[...] + acc
    if final_norm:
        out = _rms(out) * fg_ref[...]
    o_ref[...] = out


def _combine(dest_flat, y, x, gates, final_g, final_norm):
    t, d = x.shape
    tm = COMBINE_TM
    return pl.pallas_call(
        functools.partial(_combine_kernel, final_norm=final_norm),
        out_shape=jax.ShapeDtypeStruct((t, d), F32),
        grid=(t // tm,),
        in_specs=[
            pl.BlockSpec(memory_space=pl.ANY),
            pl.BlockSpec(memory_space=pl.ANY),
            pl.BlockSpec((tm, d), lambda i: (i, 0)),
            pl.BlockSpec((tm, LANES), lambda i: (i, 0)),
            pl.BlockSpec((1, d), lambda i: (0, 0)),
        ],
        out_specs=pl.BlockSpec((tm, d), lambda i: (i, 0)),
        scratch_shapes=[pltpu.SMEM((tm * TOP_K,), jnp.int32), pltpu.VMEM((TOP_K * tm, d), F32),
                        pltpu.SemaphoreType.DMA((2,))],
        compiler_params=_params(("arbitrary",)),
        name="combine",
    )(dest_flat, y, x, gates, final_g.reshape(1, d))


def _in_proj_weights(w_in):
    d = w_in.shape[0]
    o_gate = 3 * D_FOX
    o_sb = o_gate + N_FOX
    o_cq = o_sb + 3 * D_SB
    o_ckv = o_cq + Q_LORA
    o_kr = o_ckv + KV_LORA
    o_sgu = o_kr + MLA_ROPE
    half = MLA_ROPE // 2
    w_qkv = jnp.concatenate([w_in[:, :o_gate], w_in[:, o_sb:o_cq]], axis=1).astype(BF16)
    kr = w_in[:, o_kr:o_sgu]
    zpad = jnp.zeros((d, LANES - MLA_ROPE), w_in.dtype)
    w_misc = jnp.concatenate([
        w_in[:, o_sgu:o_sgu + 2 * D_SGU],
        w_in[:, o_cq:o_ckv],
        w_in[:, o_ckv:o_kr],
        kr, zpad,
        kr[:, half:], kr[:, :half], zpad,
        w_in[:, o_gate:o_sb], jnp.zeros((d, LANES - N_FOX), w_in.dtype),
    ], axis=1).astype(BF16)
    assert w_misc.shape[1] == MISC_W
    return w_qkv, w_misc


def _mla_weights(w_q_b, w_kv_b):
    half = MLA_ROPE // 2
    wq = w_q_b.reshape(Q_LORA, N_MLA, MLA_NOPE + MLA_ROPE)
    x1 = wq[:, :, MLA_NOPE:MLA_NOPE + half]
    x2 = wq[:, :, MLA_NOPE + half:]
    z = jnp.zeros((Q_LORA, N_MLA, LANES - MLA_ROPE), w_q_b.dtype)
    wq_r = jnp.concatenate([wq[:, :, :MLA_NOPE], x1, x2, z, x2, x1, z], axis=2)
    wq_r = wq_r.reshape(Q_LORA, N_MLA * 3 * LANES).astype(BF16)
    wkv = w_kv_b.reshape(KV_LORA, N_MLA, MLA_NOPE + MLA_V)
    wkv_r = jnp.concatenate([wkv[:, :, :MLA_NOPE].reshape(KV_LORA, -1),
                             wkv[:, :, MLA_NOPE:].reshape(KV_LORA, -1)], axis=1).astype(BF16)
    return wq_r, wkv_r


def _routing_tables(idx, rank, cnt, n_tokens):
    nsb_max = N_EXPERTS + (n_tokens * TOP_K) // EXP_ROWS
    nrows_max = -(-nsb_max * EXP_ROWS // GATHER_CH) * GATHER_CH
    counts = cnt[0, :N_EXPERTS]
    nsb_e = (counts + EXP_ROWS - 1) // EXP_ROWS
    sb_end = jnp.cumsum(nsb_e)
    sb_start = sb_end - nsb_e
    n_used = sb_end[-1]
    top_idx = idx[:, :TOP_K]
    dest = (sb_start * EXP_ROWS)[top_idx] + rank[:, :TOP_K]
    s_ids = jnp.arange(nsb_max, dtype=jnp.int32)
    sb_e = jnp.minimum(jnp.searchsorted(sb_end, s_ids, side='right'), N_EXPERTS - 1).astype(jnp.int32)
    last_e = sb_e[jnp.maximum(n_used - 1, 0)]
    sb_e = jnp.where(s_ids < n_used, sb_e, last_e)
    nval = jnp.clip(counts[sb_e] - (s_ids - sb_start[sb_e]) * EXP_ROWS, 0, EXP_ROWS)
    nval = jnp.where(s_ids < n_used, nval, 0).astype(jnp.int32)
    tok = jnp.broadcast_to(jnp.arange(n_tokens, dtype=jnp.int32)[:, None], (n_tokens, TOP_K))
    row_tok = jnp.zeros((nrows_max,), jnp.int32).at[dest.reshape(-1)].set(tok.reshape(-1))
    return (dest.reshape(-1).astype(jnp.int32), row_tok, sb_e, nval,
            n_used.reshape(1).astype(jnp.int32), nsb_max, nrows_max)


def _layer(x, pos, l, p, final_g, final_norm, batch, seq):
    t = x.shape[0]
    w_qkv, w_misc = _in_proj_weights(p["w_in"][l])
    qkv = _norm_matmul(x, p["norm_mix_g"][l], w_qkv, BF16, 512, 1024, "in_proj_qkv")
    misc = _norm_matmul(x, p["norm_mix_g"][l], w_misc, F32, 512, MISC_W, "in_proj_misc")

    crow = _fox_prep(misc, p["b_forget"][l], batch, seq)
    nkb = seq // ATT_TILE
    crow = crow.reshape(batch, nkb, N_FOX // FOX_HEADS, FOX_HEADS, ATT_TILE).transpose(0, 2, 1, 3, 4)
    o_fox = _fox_attention(qkv, crow, batch, seq)
    o_sb = _sb_attention(qkv, batch, seq)
    wq_r, wkv_r = _mla_weights(p["mla_w_q_b"][l], p["mla_w_kv_b"][l])
    q_m, k_m, v_m = _mla_prep(misc, pos, p["mla_q_norm_g"][l], p["mla_kv_norm_g"][l], wq_r, wkv_r)
    o_mla = _mla_attention(q_m, k_m, v_m, batch, seq)
    o_sgu = _sgu(misc, p["sgu_ln_g"][l], p["sgu_ln_b"][l], p["sgu_w_s"][l], p["sgu_b_s"][l])

    x_new, h2p, idx, gates, rank, cnt = _outproj_router(
        o_fox, o_sb, o_mla, o_sgu, x, p["out_norm_g"][l], p["w_o"][l].astype(BF16),
        p["norm_ffn_g"][l], p["router_w"][l], p["router_b"][l])

    dest, row_tok, sb_e, nval, n_used, nsb_max, nrows_max = _routing_tables(idx, rank, cnt, t)
    xs = _gather_rows(row_tok, h2p, n_used * EXP_ROWS, nrows_max)
    y = _experts(sb_e, nval, n_used, xs, l, p["w_gate"], p["b_gate"], p["w_up"], p["b_up"],
                 p["w_down"], p["b_down"], nsb_max)
    return _combine(dest, y, x_new, gates, final_g, final_norm)


def kernel(x, positions, norm_mix_g, w_in, b_forget, mla_q_norm_g, mla_kv_norm_g, mla_w_q_b, mla_w_kv_b,
           sgu_ln_g, sgu_ln_b, sgu_w_s, sgu_b_s, out_norm_g, w_o, norm_ffn_g, router_w, router_b, w_gate,
           b_gate, w_up, b_up, w_down, b_down, final_norm_g):
    batch, seq, d = x.shape
    depth = w_in.shape[0]
    p = dict(norm_mix_g=norm_mix_g, w_in=w_in, b_forget=b_forget, mla_q_norm_g=mla_q_norm_g,
             mla_kv_norm_g=mla_kv_norm_g, mla_w_q_b=mla_w_q_b, mla_w_kv_b=mla_w_kv_b, sgu_ln_g=sgu_ln_g,
             sgu_ln_b=sgu_ln_b, sgu_w_s=sgu_w_s, sgu_b_s=sgu_b_s, out_norm_g=out_norm_g, w_o=w_o,
             norm_ffn_g=norm_ffn_g, router_w=router_w, router_b=router_b, w_gate=w_gate, b_gate=b_gate,
             w_up=w_up, b_up=b_up, w_down=w_down, b_down=b_down)
    h = x.reshape(batch * seq, d)
    pos = positions.reshape(batch * seq, 1)
    for l in range(depth):
        h = _layer(h, pos, l, p, final_norm_g, l == depth - 1, batch, seq)
    return h.reshape(batch, seq, d)
```

```python
import functools

import numpy as np
import jax
import jax.numpy as jnp
from jax import lax
from jax.experimental import pallas as pl
from jax.experimental.pallas import tpu as pltpu

HEAD_DIM = 64
N_FOX = 8
N_SB = 8
N_MLA = 4
MLA_NOPE = 128
MLA_ROPE = 64
MLA_V = 128
Q_LORA = 512
KV_LORA = 256
N_SGU = 8
SGU_CH = 64
CHUNK = 128
ROPE_THETA = 10000.0
D_FOX = N_FOX * HEAD_DIM
D_SB = N_SB * HEAD_DIM
D_MLA = N_MLA * MLA_V
D_SGU = N_SGU * SGU_CH
N_EXPERTS = 32
TOP_K = 4
SWIGLU_LIMIT = 7.0
SWIGLU_ALPHA = 1.702
RMS_EPS = 1e-6
LN_EPS = 1e-5

LANES = 128
VMEM_LIMIT = 56 * 1024 * 1024

ATT_TILE = 256
FOX_HEADS = 4
MLA_HEADS = 4
SB_HEADS = 4
EXP_ROWS = 1536
EXP_SUB = 256
EXP_FF = 256
EXP_DN = 512
SCATTER_TM = 256
COMBINE_TM = 256

F32 = jnp.float32
BF16 = jnp.bfloat16

MISC_U = 0
MISC_V = 512
MISC_CQ = 1024
MISC_CKV = 1536
MISC_KR1 = 1792
MISC_KR2 = 1920
MISC_GATE = 2048
MISC_W = 2176


def _params(sem, vmem=VMEM_LIMIT):
    return pltpu.CompilerParams(dimension_semantics=sem, vmem_limit_bytes=vmem)


def _split3(x):
    hi = x.astype(BF16)
    r1 = x - hi.astype(F32)
    mid = r1.astype(BF16)
    lo = (r1 - mid.astype(F32)).astype(BF16)
    return hi, mid, lo


def _log_sigmoid_pair(z):
    t = jnp.log1p(jnp.exp(-jnp.abs(z)))
    return jnp.minimum(z, 0.0) - t, jnp.minimum(-z, 0.0) - t


def _rms(x):
    return x * lax.rsqrt(jnp.mean(x * x, axis=-1, keepdims=True) + RMS_EPS)


def _norm_matmul_kernel(x_ref, g_ref, w_ref, o_ref, xn_ref):
    @pl.when(pl.program_id(1) == 0)
    def _():
        xn_ref[...] = (_rms(x_ref[...]) * g_ref[...]).astype(BF16)

    o_ref[...] = jnp.dot(xn_ref[...], w_ref[...], preferred_element_type=F32).astype(o_ref.dtype)


def _norm_matmul(x, g, w, out_dtype, tm, tn, name):
    t, d = x.shape
    n = w.shape[1]
    return pl.pallas_call(
        _norm_matmul_kernel,
        out_shape=jax.ShapeDtypeStruct((t, n), out_dtype),
        grid=(t // tm, n // tn),
        in_specs=[
            pl.BlockSpec((tm, d), lambda i, j: (i, 0)),
            pl.BlockSpec((1, d), lambda i, j: (0, 0)),
            pl.BlockSpec((d, tn), lambda i, j: (0, j)),
        ],
        out_specs=pl.BlockSpec((tm, tn), lambda i, j: (i, j)),
        scratch_shapes=[pltpu.VMEM((tm, d), BF16)],
        compiler_params=_params(("parallel", "arbitrary")),
        name=name,
    )(x, g.reshape(1, d), w)


def _fox_prep_kernel(gate_ref, bf_ref, crow_ref):
    s_len = gate_ref.shape[0]
    bl = ATT_TILE
    r = lax.broadcasted_iota(jnp.int32, (bl, bl), 0)
    c = lax.broadcasted_iota(jnp.int32, (bl, bl), 1)
    tri = (c <= r).astype(BF16)
    carry = jnp.zeros((1, LANES), F32)
    for i in range(s_len // bl):
        z = gate_ref[i * bl:(i + 1) * bl, :] + bf_ref[...]
        lf, _ = _log_sigmoid_pair(z)
        hi, mid, lo = _split3(lf)
        cs = (jnp.dot(tri, hi, preferred_element_type=F32)
              + jnp.dot(tri, mid, preferred_element_type=F32)
              + jnp.dot(tri, lo, preferred_element_type=F32)) + carry
        crow_ref[i] = cs.T[0:N_FOX, :]
        carry = cs[bl - 1:bl, :]


def _fox_prep(misc, b_forget, batch, seq):
    nkb = seq // ATT_TILE
    bf = jnp.zeros((1, LANES), F32).at[0, :N_FOX].set(b_forget)
    return pl.pallas_call(
        _fox_prep_kernel,
        out_shape=jax.ShapeDtypeStruct((batch, nkb, N_FOX, ATT_TILE), F32),
        grid=(batch,),
        in_specs=[
            pl.BlockSpec((seq, LANES), lambda b: (b, MISC_GATE // LANES)),
            pl.BlockSpec((1, LANES), lambda b: (0, 0)),
        ],
        out_specs=pl.BlockSpec((None, nkb, N_FOX, ATT_TILE), lambda b: (b, 0, 0, 0)),
        compiler_params=_params(("parallel",)),
        name="fox_prep",
    )(misc, bf)


def _causal_mask(tq):
    row = lax.broadcasted_iota(jnp.int32, (tq, tq), 0)
    col = lax.broadcasted_iota(jnp.int32, (tq, tq), 1)
    return col <= row, col < row


def _softmax_sweep(qs, k_of, v_of, bias_of, scale, qi, tq):
    nh = len(qs)
    incl, _ = _causal_mask(tq)

    def step(kb, carry, diag):
        ks = pl.multiple_of(kb * tq, tq)
        ss = [lax.dot_general(qs[h], k_of(h, ks), (((1,), (1,)), ((), ())), preferred_element_type=F32)
              for h in range(nh)]
        ps, stats = [], []
        for h in range(nh):
            m, l, _ = carry[h]
            s = ss[h]
            if scale is not None:
                s = s * scale
            b = bias_of(h, kb)
            if b is not None:
                s = s + b
            if diag:
                s = jnp.where(incl, s, -jnp.inf)
            m_new = jnp.maximum(m, jnp.max(s, axis=-1, keepdims=True))
            alpha = jnp.exp(m - m_new)
            p = jnp.exp(s - m_new)
            stats.append((m_new, alpha * l + jnp.sum(p, axis=-1, keepdims=True), alpha))
            ps.append(p.astype(BF16))
        out = []
        for h in range(nh):
            m_new, l, alpha = stats[h]
            acc = alpha * carry[h][2] + jnp.dot(ps[h], v_of(h, ks), preferred_element_type=F32)
            out.append((m_new, l, acc))
        return tuple(out)

    dv = v_of(0, 0).shape[-1]
    init = tuple((jnp.full((tq, 1), -jnp.inf, F32), jnp.zeros((tq, 1), F32),
                  jnp.zeros((tq, dv), F32)) for _ in range(nh))
    carry = lax.fori_loop(0, qi, lambda kb, c: step(kb, c, False), init)
    carry = step(qi, carry, True)
    return [acc / l for (_, l, acc) in carry]


def _fox_attn_kernel(q_ref, k_ref, v_ref, crow_ref, o_ref):
    tq = q_ref.shape[0]
    qi = pl.program_id(2)
    q = q_ref[...] * jnp.asarray(HEAD_DIM ** -0.5, BF16)
    qs = [q[:, j * HEAD_DIM:(j + 1) * HEAD_DIM] for j in range(FOX_HEADS)]

    def k_of(h, ks):
        return k_ref[pl.ds(ks, tq), h * HEAD_DIM:(h + 1) * HEAD_DIM]

    def v_of(h, ks):
        return v_ref[pl.ds(ks, tq), h * HEAD_DIM:(h + 1) * HEAD_DIM]

    def bias_of(h, kb):
        return -crow_ref[kb][h:h + 1, :]

    outs = _softmax_sweep(qs, k_of, v_of, bias_of, None, qi, tq)
    o_ref[...] = jnp.concatenate(outs, axis=-1)


def _fox_attention(qkv, crow, batch, seq):
    t = qkv.shape[0]
    tq = ATT_TILE
    nq = seq // tq
    npair = N_FOX // FOX_HEADS
    w = FOX_HEADS * HEAD_DIM
    return pl.pallas_call(
        _fox_attn_kernel,
        out_shape=jax.ShapeDtypeStruct((t, D_FOX), F32),
        grid=(batch, npair, nq),
        in_specs=[
            pl.BlockSpec((tq, w), lambda b, h, i: (b * nq + i, h)),
            pl.BlockSpec((seq, w), lambda b, h, i: (b, npair + h)),
            pl.BlockSpec((seq, w), lambda b, h, i: (b, 2 * npair + h)),
            pl.BlockSpec((None, None, nq, FOX_HEADS, tq), lambda b, h, i: (b, h, 0, 0, 0)),
        ],
        out_specs=pl.BlockSpec((tq, w), lambda b, h, i: (b * nq + i, h)),
        compiler_params=_params(("parallel", "parallel", "arbitrary")),
        name="fox_attn",
    )(qkv, qkv, qkv, crow)


def _mla_attn_kernel(q_ref, k_ref, v_ref, o_ref):
    tq = q_ref.shape[0]
    qi = pl.program_id(2)
    dk = 2 * LANES
    scale = (MLA_NOPE + MLA_ROPE) ** -0.5
    qs = [q_ref[:, h * dk:(h + 1) * dk] for h in range(MLA_HEADS)]
    outs = _softmax_sweep(qs,
                          lambda h, ks: k_ref[pl.ds(ks, tq), h * dk:(h + 1) * dk],
                          lambda h, ks: v_ref[pl.ds(ks, tq), h * MLA_V:(h + 1) * MLA_V],
                          lambda h, kb: None, scale, qi, tq)
    o_ref[...] = jnp.concatenate(outs, axis=-1)


def _mla_attention(q, k, v, batch, seq):
    t = q.shape[0]
    tq = ATT_TILE
    nq = seq // tq
    dk = MLA_HEADS * 2 * LANES
    dv = MLA_HEADS * MLA_V
    return pl.pallas_call(
        _mla_attn_kernel,
        out_shape=jax.ShapeDtypeStruct((t, D_MLA), F32),
        grid=(batch, N_MLA // MLA_HEADS, nq),
        in_specs=[
            pl.BlockSpec((tq, dk), lambda b, h, i: (b * nq + i, h)),
            pl.BlockSpec((seq, dk), lambda b, h, i: (b, h)),
            pl.BlockSpec((seq, dv), lambda b, h, i: (b, h)),
        ],
        out_specs=pl.BlockSpec((tq, dv), lambda b, h, i: (b * nq + i, h)),
        compiler_params=_params(("parallel", "parallel", "arbitrary")),
        name="mla_attn",
    )(q, k, v)


def _sb_attn_kernel(q_ref, k_ref, v_ref, o_ref):
    tq = q_ref.shape[0]
    qi = pl.program_id(2)
    q = q_ref[...] * jnp.asarray(HEAD_DIM ** -0.5, BF16)
    nh = SB_HEADS
    qs = [q[:, j * HEAD_DIM:(j + 1) * HEAD_DIM] for j in range(nh)]
    _, strict = _causal_mask(tq)
    row = lax.broadcasted_iota(jnp.int32, (tq, tq), 0)
    col = lax.broadcasted_iota(jnp.int32, (tq, tq), 1)
    later = (row > col).astype(BF16)

    def step(kb, carry, diag):
        ks = pl.multiple_of(kb * tq, tq)
        zs = [lax.dot_general(qs[h], k_ref[pl.ds(ks, tq), h * HEAD_DIM:(h + 1) * HEAD_DIM],
                              (((1,), (1,)), ((), ())), preferred_element_type=F32) for h in range(nh)]
        lss, lrs = [], []
        for h in range(nh):
            z = zs[h]
            ls = jnp.minimum(z, 0.0) - jnp.log(1.0 + jnp.exp(-jnp.abs(z)))
            lr = ls - z
            if diag:
                lr = jnp.where(strict, lr, 0.0)
            lss.append(ls)
            lrs.append(lr)
        afters = []
        for h in range(nh):
            hi = lrs[h].astype(BF16)
            lo = (lrs[h] - hi.astype(F32)).astype(BF16)
            afters.append((jnp.dot(hi, later, preferred_element_type=F32)
                           + jnp.dot(lo, later, preferred_element_type=F32)) + carry[h][0])
        out = []
        for h in range(nh):
            a = jnp.exp(lss[h] + afters[h])
            if diag:
                a = jnp.where(strict, a, 0.0)
            v = v_ref[pl.ds(ks, tq), h * HEAD_DIM:(h + 1) * HEAD_DIM]
            acc = carry[h][1] + jnp.dot(a.astype(BF16), v, preferred_element_type=F32)
            out.append((afters[h][:, 0:1] + lrs[h][:, 0:1], acc))
        return tuple(out)

    init = tuple((jnp.zeros((tq, 1), F32), jnp.zeros((tq, HEAD_DIM), F32)) for _ in range(nh))
    carry = step(qi, init, True)
    carry = lax.fori_loop(0, qi, lambda i, c: step(qi - 1 - i, c, False), carry)
    o_ref[...] = jnp.concatenate([acc for (_, acc) in carry], axis=-1)


def _sb_attention(qkv, batch, seq):
    t = qkv.shape[0]
    tq = ATT_TILE
    nq = seq // tq
    npair = N_SB // SB_HEADS
    w = SB_HEADS * HEAD_DIM
    base = 3 * D_FOX // w
    return pl.pallas_call(
        _sb_attn_kernel,
        out_shape=jax.ShapeDtypeStruct((t, D_SB), F32),
        grid=(batch, npair, nq),
        in_specs=[
            pl.BlockSpec((tq, w), lambda b, h, i: (b * nq + i, base + h)),
            pl.BlockSpec((seq, w), lambda b, h, i: (b, base + npair + h)),
            pl.BlockSpec((seq, w), lambda b, h, i: (b, base + 2 * npair + h)),
        ],
        out_specs=pl.BlockSpec((tq, w), lambda b, h, i: (b * nq + i, h)),
        compiler_params=_params(("parallel", "parallel", "arbitrary")),
        name="sb_attn",
    )(qkv, qkv, qkv)


def _mla_prep_kernel(cq_ref, ckv_ref, kr1_ref, kr2_ref, pos_ref, gq_ref, gkv_ref, wq_ref, wkv_ref,
                     invf_ref, sgn_ref, q_out, k_out, v_out):
    ang = pos_ref[...].astype(F32) * invf_ref[...]
    cosv = jnp.cos(ang)
    sinv = jnp.sin(ang) * sgn_ref[...]
    qn = (_rms(cq_ref[...]) * gq_ref[...]).astype(BF16)
    qa = jnp.dot(qn, wq_ref[...], preferred_element_type=F32)
    for h in range(N_MLA):
        o = h * 3 * LANES
        pe = qa[:, o + LANES:o + 2 * LANES] * cosv + qa[:, o + 2 * LANES:o + 3 * LANES] * sinv
        q_out[:, h * 2 * LANES:h * 2 * LANES + LANES] = qa[:, o:o + LANES].astype(BF16)
        q_out[:, h * 2 * LANES + LANES:(h + 1) * 2 * LANES] = pe.astype(BF16)
    kvn = (_rms(ckv_ref[...]) * gkv_ref[...]).astype(BF16)
    kva = jnp.dot(kvn, wkv_ref[...], preferred_element_type=F32)
    kpe = (kr1_ref[...] * cosv + kr2_ref[...] * sinv).astype(BF16)
    for h in range(N_MLA):
        k_out[:, h * 2 * LANES:h * 2 * LANES + LANES] = kva[:, h * LANES:(h + 1) * LANES].astype(BF16)
        k_out[:, h * 2 * LANES + LANES:(h + 1) * 2 * LANES] = kpe
    v_out[...] = kva[:, N_MLA * MLA_NOPE:].astype(BF16)


def _mla_prep(misc, pos, gq, gkv, wq, wkv, tm=512):
    t = misc.shape[0]
    half = MLA_ROPE // 2
    inv_freq = ROPE_THETA ** (-jnp.arange(half, dtype=F32) / half)
    invf = jnp.tile(inv_freq, LANES // half).reshape(1, LANES)
    sgn = jnp.tile(jnp.concatenate([-jnp.ones((half,), F32), jnp.ones((half,), F32)]),
                   LANES // MLA_ROPE).reshape(1, LANES)
    const = lambda i: (0, 0)
    return pl.pallas_call(
        _mla_prep_kernel,
        out_shape=(jax.ShapeDtypeStruct((t, N_MLA * 2 * LANES), BF16),
                   jax.ShapeDtypeStruct((t, N_MLA * 2 * LANES), BF16),
                   jax.ShapeDtypeStruct((t, D_MLA), BF16)),
        grid=(t // tm,),
        in_specs=[
            pl.BlockSpec((tm, Q_LORA), lambda i: (i, MISC_CQ // Q_LORA)),
            pl.BlockSpec((tm, KV_LORA), lambda i: (i, MISC_CKV // KV_LORA)),
            pl.BlockSpec((tm, LANES), lambda i: (i, MISC_KR1 // LANES)),
            pl.BlockSpec((tm, LANES), lambda i: (i, MISC_KR2 // LANES)),
            pl.BlockSpec((tm, 1), lambda i: (i, 0)),
            pl.BlockSpec((1, Q_LORA), const),
            pl.BlockSpec((1, KV_LORA), const),
            pl.BlockSpec(wq.shape, const),
            pl.BlockSpec(wkv.shape, const),
            pl.BlockSpec((1, LANES), const),
            pl.BlockSpec((1, LANES), const),
        ],
        out_specs=(pl.BlockSpec((tm, N_MLA * 2 * LANES), lambda i: (i, 0)),
                   pl.BlockSpec((tm, N_MLA * 2 * LANES), lambda i: (i, 0)),
                   pl.BlockSpec((tm, D_MLA), lambda i: (i, 0))),
        compiler_params=_params(("parallel",)),
        name="mla_prep",
    )(misc, misc, misc, misc, pos, gq.reshape(1, -1), gkv.reshape(1, -1), wq, wkv, invf, sgn)


def _gelu(x):
    return 0.5 * x * (1.0 + lax.erf(x * np.float32(np.sqrt(0.5))))


def _sgu_kernel(u_ref, v_ref, lng_ref, lnb_ref, w_ref, bias_ref, o_ref):
    tm = u_ref.shape[0]
    v = _gelu(v_ref[...])
    mu = jnp.mean(v, axis=-1, keepdims=True)
    xc = v - mu
    var = jnp.mean(xc * xc, axis=-1, keepdims=True)
    vb = (xc * lax.rsqrt(var + LN_EPS) * lng_ref[...] + lnb_ref[...]).astype(BF16)
    row = lax.broadcasted_iota(jnp.int32, (CHUNK, CHUNK), 0)
    col = lax.broadcasted_iota(jnp.int32, (CHUNK, LANES), 1)
    tril = lax.broadcasted_iota(jnp.int32, (CHUNK, CHUNK), 1) <= row
    first = col < SGU_CH
    ws = [jnp.where(tril, w_ref[g], 0.0).astype(BF16) for g in range(N_SGU)]
    zero = jnp.zeros((CHUNK, LANES), BF16)
    for c in range(tm // CHUNK):
        rows = slice(c * CHUNK, (c + 1) * CHUNK)
        for p in range(N_SGU // 2):
            cols = slice(p * LANES, (p + 1) * LANES)
            vp = vb[rows, cols]
            mixed = (jnp.dot(ws[2 * p], jnp.where(first, vp, zero), preferred_element_type=F32)
                     + jnp.dot(ws[2 * p + 1], jnp.where(first, zero, vp), preferred_element_type=F32))
            o_ref[rows, cols] = _gelu(u_ref[rows, cols]) * (mixed + bias_ref[:, cols])


def _sgu(misc, ln_g, ln_b, w_s, b_s, tm=512):
    t = misc.shape[0]
    bias = jnp.repeat(b_s.T, SGU_CH, axis=1)
    const2 = lambda i: (0, 0)
    return pl.pallas_call(
        _sgu_kernel,
        out_shape=jax.ShapeDtypeStruct((t, D_SGU), F32),
        grid=(t // tm,),
        in_specs=[
            pl.BlockSpec((tm, D_SGU), lambda i: (i, MISC_U // D_SGU)),
            pl.BlockSpec((tm, D_SGU), lambda i: (i, MISC_V // D_SGU)),
            pl.BlockSpec((1, D_SGU), const2),
            pl.BlockSpec((1, D_SGU), const2),
            pl.BlockSpec((N_SGU, CHUNK, CHUNK), lambda i: (0, 0, 0)),
            pl.BlockSpec((CHUNK, D_SGU), const2),
        ],
        out_specs=pl.BlockSpec((tm, D_SGU), lambda i: (i, 0)),
        compiler_params=_params(("parallel",)),
        name="sgu",
    )(misc, misc, ln_g.reshape(1, -1), ln_b.reshape(1, -1), w_s, bias)


def _pack_bf16_pairs(hb):
    n = hb.shape[1] // 2
    bits = pltpu.bitcast(hb.astype(F32), jnp.uint32)
    return (bits[:, n:] & jnp.uint32(0xFFFF0000)) | (bits[:, :n] >> 16)


def _unpack_bf16_pairs(xu):
    lo = pltpu.bitcast(xu << 16, F32).astype(BF16)
    hi = pltpu.bitcast(xu & jnp.uint32(0xFFFF0000), F32).astype(BF16)
    return jnp.concatenate([lo, hi], axis=1)


def _outproj_router_kernel(of_ref, os_ref, om_ref, og_ref, x_ref, ong_ref, wo_ref, nfg_ref, rw_ref,
                           rb_ref, xo_ref, h2_ref, idx_ref, gate_ref, rank_ref, cnt_ref, carry_ref):
    tm = x_ref.shape[0]

    @pl.when(pl.program_id(0) == 0)
    def _():
        carry_ref[...] = jnp.zeros_like(carry_ref)

    o = jnp.concatenate([_rms(r[...]) for r in (of_ref, os_ref, om_ref, og_ref)], axis=-1)
    o = (o * ong_ref[...]).astype(BF16)
    xn = x_ref[...] + jnp.dot(o, wo_ref[...], preferred_element_type=F32)
    xo_ref[...] = xn
    hb = (_rms(xn) * nfg_ref[...]).astype(BF16)
    h2_ref[...] = _pack_bf16_pairs(hb)

    lane = lax.broadcasted_iota(jnp.int32, (tm, LANES), 1)
    logits = jnp.dot(hb, rw_ref[...], preferred_element_type=F32) + rb_ref[...]
    vals = jnp.where(lane < N_EXPERTS, logits, -jnp.inf)
    sels, tops = [], []
    for _ in range(TOP_K):
        m = jnp.max(vals, axis=-1, keepdims=True)
        idx = jnp.min(jnp.where(vals == m, lane, LANES), axis=-1, keepdims=True)
        sel = lane == idx
        vals = jnp.where(sel, -jnp.inf, vals)
        sels.append(sel)
        tops.append((m, idx))
    es = [jnp.exp(m - tops[0][0]) for (m, _) in tops]
    den = es[0] + es[1] + es[2] + es[3]

    multi = sels[0] | sels[1] | sels[2] | sels[3]
    mh = jnp.where(multi, 1.0, 0.0)
    r = lax.broadcasted_iota(jnp.int32, (tm, tm), 0)
    c = lax.broadcasted_iota(jnp.int32, (tm, tm), 1)
    before = (c < r).astype(BF16)
    cnt = jnp.dot(before, mh.astype(BF16), preferred_element_type=F32) + carry_ref[...]
    total = cnt[tm - 1:tm, :] + mh[tm - 1:tm, :]
    carry_ref[...] = total
    cnt_ref[...] = jnp.broadcast_to(total, cnt_ref.shape).astype(jnp.int32)

    idx_o = jnp.zeros((tm, LANES), jnp.int32)
    gate_o = jnp.zeros((tm, LANES), F32)
    rank_o = jnp.zeros((tm, LANES), jnp.int32)
    for k in range(TOP_K):
        rank_k = jnp.sum(jnp.where(sels[k], cnt, 0.0), axis=-1, keepdims=True).astype(jnp.int32)
        idx_o = jnp.where(lane == k, tops[k][1], idx_o)
        gate_o = jnp.where(lane == k, es[k] / den, gate_o)
        rank_o = jnp.where(lane == k, rank_k, rank_o)
    idx_ref[...] = idx_o
    gate_ref[...] = gate_o
    rank_ref[...] = rank_o


def _outproj_router(o_fox, o_sb, o_mla, o_sgu, x, out_norm_g, w_o, norm_ffn_g, router_w, router_b, tm=256):
    t, d = x.shape
    dg = o_fox.shape[1]
    rw = jnp.zeros((d, LANES), BF16).at[:, :N_EXPERTS].set(router_w.astype(BF16))
    rb = jnp.zeros((1, LANES), F32).at[0, :N_EXPERTS].set(router_b)
    const = lambda i: (0, 0)
    grp = pl.BlockSpec((tm, dg), lambda i: (i, 0))
    lanes_out = pl.BlockSpec((tm, LANES), lambda i: (i, 0))
    return pl.pallas_call(
        _outproj_router_kernel,
        out_shape=(jax.ShapeDtypeStruct((t, d), F32),
                   jax.ShapeDtypeStruct((t, d // 2), jnp.uint32),
                   jax.ShapeDtypeStruct((t, LANES), jnp.int32),
                   jax.ShapeDtypeStruct((t, LANES), F32),
                   jax.ShapeDtypeStruct((t, LANES), jnp.int32),
                   jax.ShapeDtypeStruct((8, LANES), jnp.int32)),
        grid=(t // tm,),
        in_specs=[grp, grp, grp, grp,
                  pl.BlockSpec((tm, d), lambda i: (i, 0)),
                  pl.BlockSpec((1, 4 * dg), const),
                  pl.BlockSpec((4 * dg, d), const),
                  pl.BlockSpec((1, d), const),
                  pl.BlockSpec((d, LANES), const),
                  pl.BlockSpec((1, LANES), const)],
        out_specs=(pl.BlockSpec((tm, d), lambda i: (i, 0)),
                   pl.BlockSpec((tm, d // 2), lambda i: (i, 0)),
                   lanes_out, lanes_out, lanes_out,
                   pl.BlockSpec((8, LANES), const)),
        scratch_shapes=[pltpu.VMEM((1, LANES), F32)],
        compiler_params=_params(("arbitrary",)),
        name="outproj_router",
    )(o_fox, o_sb, o_mla, o_sgu, x, out_norm_g.reshape(1, -1), w_o, norm_ffn_g.reshape(1, -1), rw, rb)


def _scatter_rows_kernel(dest_hbm, x_ref, xs_init_hbm, xs_hbm, dest_smem, sems):
    del xs_init_hbm
    i = pl.program_id(0)
    tm = x_ref.shape[0]
    n = tm * TOP_K
    idx_copy = pltpu.make_async_copy(dest_hbm.at[pl.ds(i * n, n)], dest_smem, sems.at[0])
    idx_copy.start()
    idx_copy.wait()

    def issue(t, _):
        for k in range(TOP_K):
            pltpu.make_async_copy(x_ref.at[pl.ds(t, 1)], xs_hbm.at[pl.ds(dest_smem[t * TOP_K + k], 1)],
                                  sems.at[1]).start()
        return 0

    lax.fori_loop(0, tm, issue, 0, unroll=2)
    for _ in range(TOP_K):
        pltpu.make_async_copy(x_ref, xs_hbm.at[pl.ds(0, tm)], sems.at[1]).wait()


def _scatter_rows(dest_flat, src, nrows):
    t, w = src.shape
    tm = SCATTER_TM
    xs_init = jnp.zeros((nrows, w), src.dtype)
    return pl.pallas_call(
        _scatter_rows_kernel,
        out_shape=jax.ShapeDtypeStruct((nrows, w), src.dtype),
        grid=(t // tm,),
        in_specs=[pl.BlockSpec(memory_space=pl.ANY),
                  pl.BlockSpec((tm, w), lambda i: (i, 0)),
                  pl.BlockSpec(memory_space=pl.ANY)],
        out_specs=pl.BlockSpec(memory_space=pl.ANY),
        scratch_shapes=[pltpu.SMEM((tm * TOP_K,), jnp.int32), pltpu.SemaphoreType.DMA((2,))],
        input_output_aliases={2: 0},
        compiler_params=_params(("arbitrary",)),
        name="scatter_rows",
    )(dest_flat, src, xs_init)


def _expert_kernel(sbe_ref, nval_ref, nused_ref, x_ref, wg_ref, bg_ref, wu_ref, bu_ref, wd_ref, bd_ref, y_ref):
    s = pl.program_id(0)
    f = pl.program_id(1)
    d = y_ref.shape[1]

    @pl.when(jnp.logical_and(s >= nused_ref[0], f == 0))
    def _():
        y_ref[...] = jnp.zeros_like(y_ref)

    @pl.when(s < nused_ref[0])
    def _():
        @pl.when(f == 0)
        def _():
            y_ref[...] = jnp.broadcast_to(bd_ref[...], y_ref.shape)

        nsub = (nval_ref[s] + EXP_SUB - 1) // EXP_SUB

        for n in range(1, EXP_ROWS // EXP_SUB + 1):
            @pl.when(nsub == n)
            def _(m=n * EXP_SUB):
                x = _unpack_bf16_pairs(x_ref[0:m, :])
                g = jnp.dot(x, wg_ref[...].astype(BF16), preferred_element_type=F32) + bg_ref[...]
                g = jnp.minimum(g, SWIGLU_LIMIT)
                u = jnp.dot(x, wu_ref[...].astype(BF16), preferred_element_type=F32) + bu_ref[...]
                u = jnp.clip(u, -SWIGLU_LIMIT, SWIGLU_LIMIT)
                a = (g * jax.nn.sigmoid(SWIGLU_ALPHA * g) * (u + 1.0)).astype(BF16)
                wd = wd_ref[...].astype(BF16)
                for c in range(d // EXP_DN):
                    cols = slice(c * EXP_DN, (c + 1) * EXP_DN)
                    y_ref[0:m, cols] += jnp.dot(a, wd[:, cols], preferred_element_type=F32)


def _experts(sb_expert, sb_nvalid, n_used, xs, layer, w_gate, b_gate, w_up, b_up, w_down, b_down, nsb_max):
    d = w_gate.shape[2]
    dff = w_gate.shape[3]
    nf = dff // EXP_FF

    def xmap(s, f, sbe, nval, nused):
        return (jnp.minimum(s, nused[0] - 1), 0)

    def ff(s, f, nused):
        return jnp.where(s < nused[0], f, nf - 1)

    b_gate4 = b_gate.reshape(b_gate.shape[0], N_EXPERTS, 1, dff)
    b_up4 = b_up.reshape(b_up.shape[0], N_EXPERTS, 1, dff)
    b_down4 = b_down.reshape(b_down.shape[0], N_EXPERTS, 1, d)
    return pl.pallas_call(
        _expert_kernel,
        out_shape=jax.ShapeDtypeStruct((nsb_max * EXP_ROWS, d), F32),
        grid_spec=pltpu.PrefetchScalarGridSpec(
            num_scalar_prefetch=3,
            grid=(nsb_max, nf),
            in_specs=[
                pl.BlockSpec((EXP_ROWS, d // 2), xmap),
                pl.BlockSpec((None, None, d, EXP_FF), lambda s, f, sbe, nval, nused: (layer, sbe[s], 0, ff(s, f, nused))),
                pl.BlockSpec((None, None, 1, EXP_FF), lambda s, f, sbe, nval, nused: (layer, sbe[s], 0, ff(s, f, nused))),
                pl.BlockSpec((None, None, d, EXP_FF), lambda s, f, sbe, nval, nused: (layer, sbe[s], 0, ff(s, f, nused))),
                pl.BlockSpec((None, None, 1, EXP_FF), lambda s, f, sbe, nval, nused: (layer, sbe[s], 0, ff(s, f, nused))),
                pl.BlockSpec((None, None, EXP_FF, d), lambda s, f, sbe, nval, nused: (layer, sbe[s], ff(s, f, nused), 0)),
                pl.BlockSpec((None, None, 1, d), lambda s, f, sbe, nval, nused: (layer, sbe[s], 0, 0)),
            ],
            out_specs=pl.BlockSpec((EXP_ROWS, d), lambda s, f, sbe, nval, nused: (s, 0)),
        ),
        compiler_params=_params(("arbitrary", "arbitrary")),
        name="experts",
    )(sb_expert, sb_nvalid, n_used, xs, w_gate, b_gate4, w_up, b_up4, w_down, b_down4)


def _combine_kernel(dest_hbm, y_hbm, x_ref, gate_ref, fg_ref, o_ref, dest_smem, ybuf, sems, *, final_norm):
    i = pl.program_id(0)
    tm = x_ref.shape[0]
    n = tm * TOP_K
    idx_copy = pltpu.make_async_copy(dest_hbm.at[pl.ds(i * n, n)], dest_smem, sems.at[0])
    idx_copy.start()
    idx_copy.wait()

    def issue(t, _):
        for k in range(TOP_K):
            pltpu.make_async_copy(y_hbm.at[pl.ds(dest_smem[t * TOP_K + k], 1)],
                                  ybuf.at[pl.ds(k * tm + t, 1)], sems.at[1]).start()
        return 0

    lax.fori_loop(0, tm, issue, 0, unroll=2)
    pltpu.make_async_copy(y_hbm.at[pl.ds(0, n)], ybuf, sems.at[1]).wait()
    gate = gate_ref[...]
    acc = gate[:, 0:1] * ybuf[0:tm]
    for k in range(1, TOP_K):
        acc = acc + gate[:, k:k + 1] * ybuf[k * tm:(k + 1) * tm]
    out = x_ref[...] + acc
    if final_norm:
        out = _rms(out) * fg_ref[...]
    o_ref[...] = out


def _combine(dest_flat, y, x, gates, final_g, final_norm):
    t, d = x.shape
    tm = COMBINE_TM
    return pl.pallas_call(
        functools.partial(_combine_kernel, final_norm=final_norm),
        out_shape=jax.ShapeDtypeStruct((t, d), F32),
        grid=(t // tm,),
        in_specs=[
            pl.BlockSpec(memory_space=pl.ANY),
            pl.BlockSpec(memory_space=pl.ANY),
            pl.BlockSpec((tm, d), lambda i: (i, 0)),
            pl.BlockSpec((tm, LANES), lambda i: (i, 0)),
            pl.BlockSpec((1, d), lambda i: (0, 0)),
        ],
        out_specs=pl.BlockSpec((tm, d), lambda i: (i, 0)),
        scratch_shapes=[pltpu.SMEM((tm * TOP_K,), jnp.int32), pltpu.VMEM((TOP_K * tm, d), F32),
                        pltpu.SemaphoreType.DMA((2,))],
        compiler_params=_params(("arbitrary",)),
        name="combine",
    )(dest_flat, y, x, gates, final_g.reshape(1, d))


def _in_proj_weights(w_in):
    d = w_in.shape[0]
    o_gate = 3 * D_FOX
    o_sb = o_gate + N_FOX
    o_cq = o_sb + 3 * D_SB
    o_ckv = o_cq + Q_LORA
    o_kr = o_ckv + KV_LORA
    o_sgu = o_kr + MLA_ROPE
    half = MLA_ROPE // 2
    w_qkv = jnp.concatenate([w_in[:, :o_gate], w_in[:, o_sb:o_cq]], axis=1).astype(BF16)
    kr = w_in[:, o_kr:o_sgu]
    zpad = jnp.zeros((d, LANES - MLA_ROPE), w_in.dtype)
    w_misc = jnp.concatenate([
        w_in[:, o_sgu:o_sgu + 2 * D_SGU],
        w_in[:, o_cq:o_ckv],
        w_in[:, o_ckv:o_kr],
        kr, zpad,
        kr[:, half:], kr[:, :half], zpad,
        w_in[:, o_gate:o_sb], jnp.zeros((d, LANES - N_FOX), w_in.dtype),
    ], axis=1).astype(BF16)
    assert w_misc.shape[1] == MISC_W
    return w_qkv, w_misc


def _mla_weights(w_q_b, w_kv_b):
    half = MLA_ROPE // 2
    wq = w_q_b.reshape(Q_LORA, N_MLA, MLA_NOPE + MLA_ROPE)
    x1 = wq[:, :, MLA_NOPE:MLA_NOPE + half]
    x2 = wq[:, :, MLA_NOPE + half:]
    z = jnp.zeros((Q_LORA, N_MLA, LANES - MLA_ROPE), w_q_b.dtype)
    wq_r = jnp.concatenate([wq[:, :, :MLA_NOPE], x1, x2, z, x2, x1, z], axis=2)
    wq_r = wq_r.reshape(Q_LORA, N_MLA * 3 * LANES).astype(BF16)
    wkv = w_kv_b.reshape(KV_LORA, N_MLA, MLA_NOPE + MLA_V)
    wkv_r = jnp.concatenate([wkv[:, :, :MLA_NOPE].reshape(KV_LORA, -1),
                             wkv[:, :, MLA_NOPE:].reshape(KV_LORA, -1)], axis=1).astype(BF16)
    return wq_r, wkv_r


def _routing_tables(idx, rank, cnt, n_tokens):
    nsb_max = N_EXPERTS + (n_tokens * TOP_K) // EXP_ROWS
    counts = cnt[0, :N_EXPERTS]
    nsb_e = (counts + EXP_ROWS - 1) // EXP_ROWS
    sb_end = jnp.cumsum(nsb_e)
    sb_start = sb_end - nsb_e
    n_used = sb_end[-1]
    top_idx = idx[:, :TOP_K]
    dest = (sb_start * EXP_ROWS)[top_idx] + rank[:, :TOP_K]
    s_ids = jnp.arange(nsb_max, dtype=jnp.int32)
    sb_e = jnp.minimum(jnp.searchsorted(sb_end, s_ids, side='right'), N_EXPERTS - 1).astype(jnp.int32)
    last_e = sb_e[jnp.maximum(n_used - 1, 0)]
    sb_e = jnp.where(s_ids < n_used, sb_e, last_e)
    nval = jnp.clip(counts[sb_e] - (s_ids - sb_start[sb_e]) * EXP_ROWS, 0, EXP_ROWS)
    nval = jnp.where(s_ids < n_used, nval, 0).astype(jnp.int32)
    return dest.reshape(-1).astype(jnp.int32), sb_e, nval, n_used.reshape(1).astype(jnp.int32), nsb_max


def _layer(x, pos, l, p, final_g, final_norm, batch, seq):
    t = x.shape[0]
    w_qkv, w_misc = _in_proj_weights(p["w_in"][l])
    qkv = _norm_matmul(x, p["norm_mix_g"][l], w_qkv, BF16, 512, 1024, "in_proj_qkv")
    misc = _norm_matmul(x, p["norm_mix_g"][l], w_misc, F32, 512, MISC_W, "in_proj_misc")

    crow = _fox_prep(misc, p["b_forget"][l], batch, seq)
    nkb = seq // ATT_TILE
    crow = crow.reshape(batch, nkb, N_FOX // FOX_HEADS, FOX_HEADS, ATT_TILE).transpose(0, 2, 1, 3, 4)
    o_fox = _fox_attention(qkv, crow, batch, seq)
    o_sb = _sb_attention(qkv, batch, seq)
    wq_r, wkv_r = _mla_weights(p["mla_w_q_b"][l], p["mla_w_kv_b"][l])
    q_m, k_m, v_m = _mla_prep(misc, pos, p["mla_q_norm_g"][l], p["mla_kv_norm_g"][l], wq_r, wkv_r)
    o_mla = _mla_attention(q_m, k_m, v_m, batch, seq)
    o_sgu = _sgu(misc, p["sgu_ln_g"][l], p["sgu_ln_b"][l], p["sgu_w_s"][l], p["sgu_b_s"][l])

    x_new, h2p, idx, gates, rank, cnt = _outproj_router(
        o_fox, o_sb, o_mla, o_sgu, x, p["out_norm_g"][l], p["w_o"][l].astype(BF16),
        p["norm_ffn_g"][l], p["router_w"][l], p["router_b"][l])

    dest, sb_e, nval, n_used, nsb_max = _routing_tables(idx, rank, cnt, t)
    xs = _scatter_rows(dest, h2p, nsb_max * EXP_ROWS)
    y = _experts(sb_e, nval, n_used, xs, l, p["w_gate"], p["b_gate"], p["w_up"], p["b_up"],
                 p["w_down"], p["b_down"], nsb_max)
    return _combine(dest, y, x_new, gates, final_g, final_norm)


def kernel(x, positions, norm_mix_g, w_in, b_forget, mla_q_norm_g, mla_kv_norm_g, mla_w_q_b, mla_w_kv_b,
           sgu_ln_g, sgu_ln_b, sgu_w_s, sgu_b_s, out_norm_g, w_o, norm_ffn_g, router_w, router_b, w_gate,
           b_gate, w_up, b_up, w_down, b_down, final_norm_g):
    batch, seq, d = x.shape
    depth = w_in.shape[0]
    p = dict(norm_mix_g=norm_mix_g, w_in=w_in, b_forget=b_forget, mla_q_norm_g=mla_q_norm_g,
             mla_kv_norm_g=mla_kv_norm_g, mla_w_q_b=mla_w_q_b, mla_w_kv_b=mla_w_kv_b, sgu_ln_g=sgu_ln_g,
             sgu_ln_b=sgu_ln_b, sgu_w_s=sgu_w_s, sgu_b_s=sgu_b_s, out_norm_g=out_norm_g, w_o=w_o,
             norm_ffn_g=norm_ffn_g, router_w=router_w, router_b=router_b, w_gate=w_gate, b_gate=b_gate,
             w_up=w_up, b_up=b_up, w_down=w_down, b_down=b_down)
    h = x.reshape(batch * seq, d)
    pos = positions.reshape(batch * seq, 1)
    for l in range(depth):
        h = _layer(h, pos, l, p, final_norm_g, l == depth - 1, batch, seq)
    return h.reshape(batch, seq, d)
```

```python
import functools

import numpy as np
import jax
import jax.numpy as jnp
from jax import lax
from jax.experimental import pallas as pl
from jax.experimental.pallas import tpu as pltpu

HEAD_DIM = 64
N_FOX = 8
N_SB = 8
N_MLA = 4
MLA_NOPE = 128
MLA_ROPE = 64
MLA_V = 128
Q_LORA = 512
KV_LORA = 256
N_SGU = 8
SGU_CH = 64
CHUNK = 128
ROPE_THETA = 10000.0
D_FOX = N_FOX * HEAD_DIM
D_SB = N_SB * HEAD_DIM
D_MLA = N_MLA * MLA_V
D_SGU = N_SGU * SGU_CH
N_EXPERTS = 32
TOP_K = 4
SWIGLU_LIMIT = 7.0
SWIGLU_ALPHA = 1.702
RMS_EPS = 1e-6
LN_EPS = 1e-5

LANES = 128
VMEM_LIMIT = 56 * 1024 * 1024

ATT_TILE = 256
FOX_HEADS = 4
MLA_HEADS = 4
SB_HEADS = 4
EXP_ROWS = 1536
EXP_SUB = 256
EXP_FF = 256
EXP_DN = 512
SCATTER_TM = 256
COMBINE_TM = 256

F32 = jnp.float32
BF16 = jnp.bfloat16

MISC_U = 0
MISC_V = 512
MISC_CQ = 1024
MISC_CKV = 1536
MISC_KR1 = 1792
MISC_KR2 = 1920
MISC_GATE = 2048
MISC_W = 2176


def _params(sem, vmem=VMEM_LIMIT):
    return pltpu.CompilerParams(dimension_semantics=sem, vmem_limit_bytes=vmem)


def _split3(x):
    hi = x.astype(BF16)
    r1 = x - hi.astype(F32)
    mid = r1.astype(BF16)
    lo = (r1 - mid.astype(F32)).astype(BF16)
    return hi, mid, lo


def _log_sigmoid_pair(z):
    t = jnp.log1p(jnp.exp(-jnp.abs(z)))
    return jnp.minimum(z, 0.0) - t, jnp.minimum(-z, 0.0) - t


def _rms(x):
    return x * lax.rsqrt(jnp.mean(x * x, axis=-1, keepdims=True) + RMS_EPS)


def _norm_matmul_kernel(x_ref, g_ref, w_ref, o_ref, xn_ref):
    @pl.when(pl.program_id(1) == 0)
    def _():
        xn_ref[...] = (_rms(x_ref[...]) * g_ref[...]).astype(BF16)

    o_ref[...] = jnp.dot(xn_ref[...], w_ref[...], preferred_element_type=F32).astype(o_ref.dtype)


def _norm_matmul(x, g, w, out_dtype, tm, tn, name):
    t, d = x.shape
    n = w.shape[1]
    return pl.pallas_call(
        _norm_matmul_kernel,
        out_shape=jax.ShapeDtypeStruct((t, n), out_dtype),
        grid=(t // tm, n // tn),
        in_specs=[
            pl.BlockSpec((tm, d), lambda i, j: (i, 0)),
            pl.BlockSpec((1, d), lambda i, j: (0, 0)),
            pl.BlockSpec((d, tn), lambda i, j: (0, j)),
        ],
        out_specs=pl.BlockSpec((tm, tn), lambda i, j: (i, j)),
        scratch_shapes=[pltpu.VMEM((tm, d), BF16)],
        compiler_params=_params(("parallel", "arbitrary")),
        name=name,
    )(x, g.reshape(1, d), w)


def _fox_prep_kernel(gate_ref, bf_ref, crow_ref):
    s_len = gate_ref.shape[0]
    bl = ATT_TILE
    r = lax.broadcasted_iota(jnp.int32, (bl, bl), 0)
    c = lax.broadcasted_iota(jnp.int32, (bl, bl), 1)
    tri = (c <= r).astype(BF16)
    carry = jnp.zeros((1, LANES), F32)
    for i in range(s_len // bl):
        z = gate_ref[i * bl:(i + 1) * bl, :] + bf_ref[...]
        lf, _ = _log_sigmoid_pair(z)
        hi, mid, lo = _split3(lf)
        cs = (jnp.dot(tri, hi, preferred_element_type=F32)
              + jnp.dot(tri, mid, preferred_element_type=F32)
              + jnp.dot(tri, lo, preferred_element_type=F32)) + carry
        crow_ref[i] = cs.T[0:N_FOX, :]
        carry = cs[bl - 1:bl, :]


def _fox_prep(misc, b_forget, batch, seq):
    nkb = seq // ATT_TILE
    bf = jnp.zeros((1, LANES), F32).at[0, :N_FOX].set(b_forget)
    return pl.pallas_call(
        _fox_prep_kernel,
        out_shape=jax.ShapeDtypeStruct((batch, nkb, N_FOX, ATT_TILE), F32),
        grid=(batch,),
        in_specs=[
            pl.BlockSpec((seq, LANES), lambda b: (b, MISC_GATE // LANES)),
            pl.BlockSpec((1, LANES), lambda b: (0, 0)),
        ],
        out_specs=pl.BlockSpec((None, nkb, N_FOX, ATT_TILE), lambda b: (b, 0, 0, 0)),
        compiler_params=_params(("parallel",)),
        name="fox_prep",
    )(misc, bf)


def _causal_mask(tq):
    row = lax.broadcasted_iota(jnp.int32, (tq, tq), 0)
    col = lax.broadcasted_iota(jnp.int32, (tq, tq), 1)
    return col <= row, col < row


def _softmax_sweep(qs, k_of, v_of, bias_of, scale, qi, tq, ones_lane=None):
    nh = len(qs)
    incl, _ = _causal_mask(tq)

    def step(kb, carry, diag):
        ks = pl.multiple_of(kb * tq, tq)
        ss = [lax.dot_general(qs[h], k_of(h, ks), (((1,), (1,)), ((), ())), preferred_element_type=F32)
              for h in range(nh)]
        ps, stats = [], []
        for h in range(nh):
            m, l, _ = carry[h]
            s = ss[h]
            if scale is not None:
                s = s * scale
            b = bias_of(h, kb)
            if b is not None:
                s = s + b
            if diag:
                s = jnp.where(incl, s, -jnp.inf)
            m_new = jnp.maximum(m, jnp.max(s, axis=-1, keepdims=True))
            alpha = jnp.exp(m - m_new)
            p = jnp.exp(s - m_new)
            if ones_lane is None:
                l = alpha * l + jnp.sum(p, axis=-1, keepdims=True)
            stats.append((m_new, l, alpha))
            ps.append(p.astype(BF16))
        out = []
        for h in range(nh):
            m_new, l, alpha = stats[h]
            acc = alpha * carry[h][2] + jnp.dot(ps[h], v_of(h, ks), preferred_element_type=F32)
            out.append((m_new, l, acc))
        return tuple(out)

    dv = v_of(0, 0).shape[-1]
    l0 = jnp.zeros((tq, 1), F32) if ones_lane is None else None
    init = tuple((jnp.full((tq, 1), -jnp.inf, F32), l0, jnp.zeros((tq, dv), F32)) for _ in range(nh))
    carry = lax.fori_loop(0, qi, lambda kb, c: step(kb, c, False), init)
    carry = step(qi, carry, True)
    if ones_lane is None:
        return [acc / l for (_, l, acc) in carry]
    return [acc / acc[:, ones_lane(h):ones_lane(h) + 1] for h, (_, _, acc) in enumerate(carry)]


def _fox_attn_kernel(q_ref, k_ref, v_ref, crow_ref, o_ref):
    tq = q_ref.shape[0]
    qi = pl.program_id(2)
    q = q_ref[...] * jnp.asarray(HEAD_DIM ** -0.5, BF16)
    qs = [q[:, j * HEAD_DIM:(j + 1) * HEAD_DIM] for j in range(FOX_HEADS)]

    def k_of(h, ks):
        return k_ref[pl.ds(ks, tq), h * HEAD_DIM:(h + 1) * HEAD_DIM]

    first = lax.broadcasted_iota(jnp.int32, (tq, LANES), 1) < HEAD_DIM
    one = jnp.ones((tq, LANES), BF16)

    def v_of(h, ks):
        pair = v_ref[pl.ds(ks, tq), (h // 2) * LANES:(h // 2 + 1) * LANES]
        return jnp.where(first, pair, one) if h % 2 == 0 else jnp.where(first, one, pair)

    def bias_of(h, kb):
        return -crow_ref[kb][h:h + 1, :]

    outs = _softmax_sweep(qs, k_of, v_of, bias_of, None, qi, tq,
                          ones_lane=lambda h: HEAD_DIM if h % 2 == 0 else 0)
    o_ref[...] = jnp.concatenate([jnp.where(first, outs[h], outs[h + 1]) for h in range(0, FOX_HEADS, 2)],
                                 axis=-1)


def _fox_attention(qkv, crow, batch, seq):
    t = qkv.shape[0]
    tq = ATT_TILE
    nq = seq // tq
    npair = N_FOX // FOX_HEADS
    w = FOX_HEADS * HEAD_DIM
    return pl.pallas_call(
        _fox_attn_kernel,
        out_shape=jax.ShapeDtypeStruct((t, D_FOX), F32),
        grid=(batch, npair, nq),
        in_specs=[
            pl.BlockSpec((tq, w), lambda b, h, i: (b * nq + i, h)),
            pl.BlockSpec((seq, w), lambda b, h, i: (b, npair + h)),
            pl.BlockSpec((seq, w), lambda b, h, i: (b, 2 * npair + h)),
            pl.BlockSpec((None, None, nq, FOX_HEADS, tq), lambda b, h, i: (b, h, 0, 0, 0)),
        ],
        out_specs=pl.BlockSpec((tq, w), lambda b, h, i: (b * nq + i, h)),
        compiler_params=_params(("parallel", "parallel", "arbitrary")),
        name="fox_attn",
    )(qkv, qkv, qkv, crow)


def _mla_attn_kernel(q_ref, k_ref, v_ref, o_ref):
    tq = q_ref.shape[0]
    qi = pl.program_id(2)
    dk = 2 * LANES
    scale = (MLA_NOPE + MLA_ROPE) ** -0.5
    qs = [q_ref[:, h * dk:(h + 1) * dk] for h in range(MLA_HEADS)]
    outs = _softmax_sweep(qs,
                          lambda h, ks: k_ref[pl.ds(ks, tq), h * dk:(h + 1) * dk],
                          lambda h, ks: v_ref[pl.ds(ks, tq), h * MLA_V:(h + 1) * MLA_V],
                          lambda h, kb: None, scale, qi, tq)
    o_ref[...] = jnp.concatenate(outs, axis=-1)


def _mla_attention(q, k, v, batch, seq):
    t = q.shape[0]
    tq = ATT_TILE
    nq = seq // tq
    dk = MLA_HEADS * 2 * LANES
    dv = MLA_HEADS * MLA_V
    return pl.pallas_call(
        _mla_attn_kernel,
        out_shape=jax.ShapeDtypeStruct((t, D_MLA), F32),
        grid=(batch, N_MLA // MLA_HEADS, nq),
        in_specs=[
            pl.BlockSpec((tq, dk), lambda b, h, i: (b * nq + i, h)),
            pl.BlockSpec((seq, dk), lambda b, h, i: (b, h)),
            pl.BlockSpec((seq, dv), lambda b, h, i: (b, h)),
        ],
        out_specs=pl.BlockSpec((tq, dv), lambda b, h, i: (b * nq + i, h)),
        compiler_params=_params(("parallel", "parallel", "arbitrary")),
        name="mla_attn",
    )(q, k, v)


def _sb_attn_kernel(q_ref, k_ref, v_ref, o_ref):
    tq = q_ref.shape[0]
    qi = pl.program_id(2)
    q = q_ref[...] * jnp.asarray(HEAD_DIM ** -0.5, BF16)
    nh = SB_HEADS
    qs = [q[:, j * HEAD_DIM:(j + 1) * HEAD_DIM] for j in range(nh)]
    _, strict = _causal_mask(tq)
    row = lax.broadcasted_iota(jnp.int32, (tq, tq), 0)
    col = lax.broadcasted_iota(jnp.int32, (tq, tq), 1)
    later = (row > col).astype(BF16)

    def step(kb, carry, diag):
        ks = pl.multiple_of(kb * tq, tq)
        zs = [lax.dot_general(qs[h], k_ref[pl.ds(ks, tq), h * HEAD_DIM:(h + 1) * HEAD_DIM],
                              (((1,), (1,)), ((), ())), preferred_element_type=F32) for h in range(nh)]
        lss, lrs = [], []
        for h in range(nh):
            z = zs[h]
            ls = jnp.minimum(z, 0.0) - jnp.log(1.0 + jnp.exp(-jnp.abs(z)))
            lr = ls - z
            if diag:
                lr = jnp.where(strict, lr, 0.0)
            lss.append(ls)
            lrs.append(lr)
        afters = []
        for h in range(nh):
            hi = lrs[h].astype(BF16)
            lo = (lrs[h] - hi.astype(F32)).astype(BF16)
            afters.append((jnp.dot(hi, later, preferred_element_type=F32)
                           + jnp.dot(lo, later, preferred_element_type=F32)) + carry[h][0])
        out = []
        for h in range(nh):
            a = jnp.exp(lss[h] + afters[h])
            if diag:
                a = jnp.where(strict, a, 0.0)
            v = v_ref[pl.ds(ks, tq), h * HEAD_DIM:(h + 1) * HEAD_DIM]
            acc = carry[h][1] + jnp.dot(a.astype(BF16), v, preferred_element_type=F32)
            out.append((afters[h][:, 0:1] + lrs[h][:, 0:1], acc))
        return tuple(out)

    init = tuple((jnp.zeros((tq, 1), F32), jnp.zeros((tq, HEAD_DIM), F32)) for _ in range(nh))
    carry = step(qi, init, True)
    carry = lax.fori_loop(0, qi, lambda i, c: step(qi - 1 - i, c, False), carry)
    o_ref[...] = jnp.concatenate([acc for (_, acc) in carry], axis=-1)


def _sb_attention(qkv, batch, seq):
    t = qkv.shape[0]
    tq = ATT_TILE
    nq = seq // tq
    npair = N_SB // SB_HEADS
    w = SB_HEADS * HEAD_DIM
    base = 3 * D_FOX // w
    return pl.pallas_call(
        _sb_attn_kernel,
        out_shape=jax.ShapeDtypeStruct((t, D_SB), F32),
        grid=(batch, npair, nq),
        in_specs=[
            pl.BlockSpec((tq, w), lambda b, h, i: (b * nq + i, base + h)),
            pl.BlockSpec((seq, w), lambda b, h, i: (b, base + npair + h)),
            pl.BlockSpec((seq, w), lambda b, h, i: (b, base + 2 * npair + h)),
        ],
        out_specs=pl.BlockSpec((tq, w), lambda b, h, i: (b * nq + i, h)),
        compiler_params=_params(("parallel", "parallel", "arbitrary")),
        name="sb_attn",
    )(qkv, qkv, qkv)


def _mla_prep_kernel(cq_ref, ckv_ref, kr1_ref, kr2_ref, pos_ref, gq_ref, gkv_ref, wq_ref, wkv_ref,
                     invf_ref, sgn_ref, q_out, k_out, v_out):
    ang = pos_ref[...].astype(F32) * invf_ref[...]
    cosv = jnp.cos(ang)
    sinv = jnp.sin(ang) * sgn_ref[...]
    qn = (_rms(cq_ref[...]) * gq_ref[...]).astype(BF16)
    qa = jnp.dot(qn, wq_ref[...], preferred_element_type=F32)
    for h in range(N_MLA):
        o = h * 3 * LANES
        pe = qa[:, o + LANES:o + 2 * LANES] * cosv + qa[:, o + 2 * LANES:o + 3 * LANES] * sinv
        q_out[:, h * 2 * LANES:h * 2 * LANES + LANES] = qa[:, o:o + LANES].astype(BF16)
        q_out[:, h * 2 * LANES + LANES:(h + 1) * 2 * LANES] = pe.astype(BF16)
    kvn = (_rms(ckv_ref[...]) * gkv_ref[...]).astype(BF16)
    kva = jnp.dot(kvn, wkv_ref[...], preferred_element_type=F32)
    kpe = (kr1_ref[...] * cosv + kr2_ref[...] * sinv).astype(BF16)
    for h in range(N_MLA):
        k_out[:, h * 2 * LANES:h * 2 * LANES + LANES] = kva[:, h * LANES:(h + 1) * LANES].astype(BF16)
        k_out[:, h * 2 * LANES + LANES:(h + 1) * 2 * LANES] = kpe
    v_out[...] = kva[:, N_MLA * MLA_NOPE:].astype(BF16)


def _mla_prep(misc, pos, gq, gkv, wq, wkv, tm=512):
    t = misc.shape[0]
    half = MLA_ROPE // 2
    inv_freq = ROPE_THETA ** (-jnp.arange(half, dtype=F32) / half)
    invf = jnp.tile(inv_freq, LANES // half).reshape(1, LANES)
    sgn = jnp.tile(jnp.concatenate([-jnp.ones((half,), F32), jnp.ones((half,), F32)]),
                   LANES // MLA_ROPE).reshape(1, LANES)
    const = lambda i: (0, 0)
    return pl.pallas_call(
        _mla_prep_kernel,
        out_shape=(jax.ShapeDtypeStruct((t, N_MLA * 2 * LANES), BF16),
                   jax.ShapeDtypeStruct((t, N_MLA * 2 * LANES), BF16),
                   jax.ShapeDtypeStruct((t, D_MLA), BF16)),
        grid=(t // tm,),
        in_specs=[
            pl.BlockSpec((tm, Q_LORA), lambda i: (i, MISC_CQ // Q_LORA)),
            pl.BlockSpec((tm, KV_LORA), lambda i: (i, MISC_CKV // KV_LORA)),
            pl.BlockSpec((tm, LANES), lambda i: (i, MISC_KR1 // LANES)),
            pl.BlockSpec((tm, LANES), lambda i: (i, MISC_KR2 // LANES)),
            pl.BlockSpec((tm, 1), lambda i: (i, 0)),
            pl.BlockSpec((1, Q_LORA), const),
            pl.BlockSpec((1, KV_LORA), const),
            pl.BlockSpec(wq.shape, const),
            pl.BlockSpec(wkv.shape, const),
            pl.BlockSpec((1, LANES), const),
            pl.BlockSpec((1, LANES), const),
        ],
        out_specs=(pl.BlockSpec((tm, N_MLA * 2 * LANES), lambda i: (i, 0)),
                   pl.BlockSpec((tm, N_MLA * 2 * LANES), lambda i: (i, 0)),
                   pl.BlockSpec((tm, D_MLA), lambda i: (i, 0))),
        compiler_params=_params(("parallel",)),
        name="mla_prep",
    )(misc, misc, misc, misc, pos, gq.reshape(1, -1), gkv.reshape(1, -1), wq, wkv, invf, sgn)


def _gelu(x):
    return 0.5 * x * (1.0 + lax.erf(x * np.float32(np.sqrt(0.5))))


def _sgu_kernel(u_ref, v_ref, lng_ref, lnb_ref, w_ref, bias_ref, o_ref):
    tm = u_ref.shape[0]
    v = _gelu(v_ref[...])
    mu = jnp.mean(v, axis=-1, keepdims=True)
    xc = v - mu
    var = jnp.mean(xc * xc, axis=-1, keepdims=True)
    vb = (xc * lax.rsqrt(var + LN_EPS) * lng_ref[...] + lnb_ref[...]).astype(BF16)
    row = lax.broadcasted_iota(jnp.int32, (CHUNK, CHUNK), 0)
    col = lax.broadcasted_iota(jnp.int32, (CHUNK, LANES), 1)
    tril = lax.broadcasted_iota(jnp.int32, (CHUNK, CHUNK), 1) <= row
    first = col < SGU_CH
    ws = [jnp.where(tril, w_ref[g], 0.0).astype(BF16) for g in range(N_SGU)]
    zero = jnp.zeros((CHUNK, LANES), BF16)
    for c in range(tm // CHUNK):
        rows = slice(c * CHUNK, (c + 1) * CHUNK)
        for p in range(N_SGU // 2):
            cols = slice(p * LANES, (p + 1) * LANES)
            vp = vb[rows, cols]
            mixed = (jnp.dot(ws[2 * p], jnp.where(first, vp, zero), preferred_element_type=F32)
                     + jnp.dot(ws[2 * p + 1], jnp.where(first, zero, vp), preferred_element_type=F32))
            o_ref[rows, cols] = _gelu(u_ref[rows, cols]) * (mixed + bias_ref[:, cols])


def _sgu(misc, ln_g, ln_b, w_s, b_s, tm=512):
    t = misc.shape[0]
    bias = jnp.repeat(b_s.T, SGU_CH, axis=1)
    const2 = lambda i: (0, 0)
    return pl.pallas_call(
        _sgu_kernel,
        out_shape=jax.ShapeDtypeStruct((t, D_SGU), F32),
        grid=(t // tm,),
        in_specs=[
            pl.BlockSpec((tm, D_SGU), lambda i: (i, MISC_U // D_SGU)),
            pl.BlockSpec((tm, D_SGU), lambda i: (i, MISC_V // D_SGU)),
            pl.BlockSpec((1, D_SGU), const2),
            pl.BlockSpec((1, D_SGU), const2),
            pl.BlockSpec((N_SGU, CHUNK, CHUNK), lambda i: (0, 0, 0)),
            pl.BlockSpec((CHUNK, D_SGU), const2),
        ],
        out_specs=pl.BlockSpec((tm, D_SGU), lambda i: (i, 0)),
        compiler_params=_params(("parallel",)),
        name="sgu",
    )(misc, misc, ln_g.reshape(1, -1), ln_b.reshape(1, -1), w_s, bias)


def _pack_bf16_pairs(hb):
    n = hb.shape[1] // 2
    bits = pltpu.bitcast(hb.astype(F32), jnp.uint32)
    return (bits[:, n:] & jnp.uint32(0xFFFF0000)) | (bits[:, :n] >> 16)


def _unpack_bf16_pairs(xu):
    lo = pltpu.bitcast(xu << 16, F32).astype(BF16)
    hi = pltpu.bitcast(xu & jnp.uint32(0xFFFF0000), F32).astype(BF16)
    return jnp.concatenate([lo, hi], axis=1)


def _outproj_router_kernel(of_ref, os_ref, om_ref, og_ref, x_ref, ong_ref, wo_ref, nfg_ref, rw_ref,
                           rb_ref, xo_ref, h2_ref, idx_ref, gate_ref, rank_ref, cnt_ref, carry_ref):
    tm = x_ref.shape[0]

    @pl.when(pl.program_id(0) == 0)
    def _():
        carry_ref[...] = jnp.zeros_like(carry_ref)

    o = jnp.concatenate([_rms(r[...]) for r in (of_ref, os_ref, om_ref, og_ref)], axis=-1)
    o = (o * ong_ref[...]).astype(BF16)
    xn = x_ref[...] + jnp.dot(o, wo_ref[...], preferred_element_type=F32)
    xo_ref[...] = xn
    hb = (_rms(xn) * nfg_ref[...]).astype(BF16)
    h2_ref[...] = _pack_bf16_pairs(hb)

    lane = lax.broadcasted_iota(jnp.int32, (tm, LANES), 1)
    logits = jnp.dot(hb, rw_ref[...], preferred_element_type=F32) + rb_ref[...]
    vals = jnp.where(lane < N_EXPERTS, logits, -jnp.inf)
    sels, tops = [], []
    for _ in range(TOP_K):
        m = jnp.max(vals, axis=-1, keepdims=True)
        idx = jnp.min(jnp.where(vals == m, lane, LANES), axis=-1, keepdims=True)
        sel = lane == idx
        vals = jnp.where(sel, -jnp.inf, vals)
        sels.append(sel)
        tops.append((m, idx))
    es = [jnp.exp(m - tops[0][0]) for (m, _) in tops]
    den = es[0] + es[1] + es[2] + es[3]

    multi = sels[0] | sels[1] | sels[2] | sels[3]
    mh = jnp.where(multi, 1.0, 0.0)
    r = lax.broadcasted_iota(jnp.int32, (tm, tm), 0)
    c = lax.broadcasted_iota(jnp.int32, (tm, tm), 1)
    before = (c < r).astype(BF16)
    cnt = jnp.dot(before, mh.astype(BF16), preferred_element_type=F32) + carry_ref[...]
    total = cnt[tm - 1:tm, :] + mh[tm - 1:tm, :]
    carry_ref[...] = total
    cnt_ref[...] = jnp.broadcast_to(total, cnt_ref.shape).astype(jnp.int32)

    idx_o = jnp.zeros((tm, LANES), jnp.int32)
    gate_o = jnp.zeros((tm, LANES), F32)
    rank_o = jnp.zeros((tm, LANES), jnp.int32)
    for k in range(TOP_K):
        rank_k = jnp.sum(jnp.where(sels[k], cnt, 0.0), axis=-1, keepdims=True).astype(jnp.int32)
        idx_o = jnp.where(lane == k, tops[k][1], idx_o)
        gate_o = jnp.where(lane == k, es[k] / den, gate_o)
        rank_o = jnp.where(lane == k, rank_k, rank_o)
    idx_ref[...] = idx_o
    gate_ref[...] = gate_o
    rank_ref[...] = rank_o


def _outproj_router(o_fox, o_sb, o_mla, o_sgu, x, out_norm_g, w_o, norm_ffn_g, router_w, router_b, tm=512):
    t, d = x.shape
    dg = o_fox.shape[1]
    rw = jnp.zeros((d, LANES), BF16).at[:, :N_EXPERTS].set(router_w.astype(BF16))
    rb = jnp.zeros((1, LANES), F32).at[0, :N_EXPERTS].set(router_b)
    const = lambda i: (0, 0)
    grp = pl.BlockSpec((tm, dg), lambda i: (i, 0))
    lanes_out = pl.BlockSpec((tm, LANES), lambda i: (i, 0))
    return pl.pallas_call(
        _outproj_router_kernel,
        out_shape=(jax.ShapeDtypeStruct((t, d), F32),
                   jax.ShapeDtypeStruct((t, d // 2), jnp.uint32),
                   jax.ShapeDtypeStruct((t, LANES), jnp.int32),
                   jax.ShapeDtypeStruct((t, LANES), F32),
                   jax.ShapeDtypeStruct((t, LANES), jnp.int32),
                   jax.ShapeDtypeStruct((8, LANES), jnp.int32)),
        grid=(t // tm,),
        in_specs=[grp, grp, grp, grp,
                  pl.BlockSpec((tm, d), lambda i: (i, 0)),
                  pl.BlockSpec((1, 4 * dg), const),
                  pl.BlockSpec((4 * dg, d), const),
                  pl.BlockSpec((1, d), const),
                  pl.BlockSpec((d, LANES), const),
                  pl.BlockSpec((1, LANES), const)],
        out_specs=(pl.BlockSpec((tm, d), lambda i: (i, 0)),
                   pl.BlockSpec((tm, d // 2), lambda i: (i, 0)),
                   lanes_out, lanes_out, lanes_out,
                   pl.BlockSpec((8, LANES), const)),
        scratch_shapes=[pltpu.VMEM((1, LANES), F32)],
        compiler_params=_params(("arbitrary",)),
        name="outproj_router",
    )(o_fox, o_sb, o_mla, o_sgu, x, out_norm_g.reshape(1, -1), w_o, norm_ffn_g.reshape(1, -1), rw, rb)


def _scatter_rows_kernel(dest_hbm, x_ref, xs_init_hbm, xs_hbm, dest_smem, sems):
    del xs_init_hbm
    i = pl.program_id(0)
    tm = x_ref.shape[0]
    n = tm * TOP_K
    idx_copy = pltpu.make_async_copy(dest_hbm.at[pl.ds(i * n, n)], dest_smem, sems.at[0])
    idx_copy.start()
    idx_copy.wait()

    def issue(t, _):
        for k in range(TOP_K):
            pltpu.make_async_copy(x_ref.at[pl.ds(t, 1)], xs_hbm.at[pl.ds(dest_smem[t * TOP_K + k], 1)],
                                  sems.at[1]).start()
        return 0

    lax.fori_loop(0, tm, issue, 0, unroll=2)
    for _ in range(TOP_K):
        pltpu.make_async_copy(x_ref, xs_hbm.at[pl.ds(0, tm)], sems.at[1]).wait()


def _scatter_rows(dest_flat, src, nrows):
    t, w = src.shape
    tm = SCATTER_TM
    xs_init = jnp.zeros((nrows, w), src.dtype)
    return pl.pallas_call(
        _scatter_rows_kernel,
        out_shape=jax.ShapeDtypeStruct((nrows, w), src.dtype),
        grid=(t // tm,),
        in_specs=[pl.BlockSpec(memory_space=pl.ANY),
                  pl.BlockSpec((tm, w), lambda i: (i, 0)),
                  pl.BlockSpec(memory_space=pl.ANY)],
        out_specs=pl.BlockSpec(memory_space=pl.ANY),
        scratch_shapes=[pltpu.SMEM((tm * TOP_K,), jnp.int32), pltpu.SemaphoreType.DMA((2,))],
        input_output_aliases={2: 0},
        compiler_params=_params(("arbitrary",)),
        name="scatter_rows",
    )(dest_flat, src, xs_init)


def _expert_kernel(sbe_ref, nval_ref, nused_ref, x_ref, wg_ref, bg_ref, wu_ref, bu_ref, wd_ref, bd_ref, y_ref):
    s = pl.program_id(0)
    f = pl.program_id(1)
    d = y_ref.shape[1]

    @pl.when(jnp.logical_and(s >= nused_ref[0], f == 0))
    def _():
        y_ref[...] = jnp.zeros_like(y_ref)

    @pl.when(s < nused_ref[0])
    def _():
        @pl.when(f == 0)
        def _():
            y_ref[...] = jnp.broadcast_to(bd_ref[...], y_ref.shape)

        nsub = (nval_ref[s] + EXP_SUB - 1) // EXP_SUB

        for n in range(1, EXP_ROWS // EXP_SUB + 1):
            @pl.when(nsub == n)
            def _(m=n * EXP_SUB):
                x = _unpack_bf16_pairs(x_ref[0:m, :])
                g = jnp.dot(x, wg_ref[...].astype(BF16), preferred_element_type=F32) + bg_ref[...]
                g = jnp.minimum(g, SWIGLU_LIMIT)
                u = jnp.dot(x, wu_ref[...].astype(BF16), preferred_element_type=F32) + bu_ref[...]
                u = jnp.clip(u, -SWIGLU_LIMIT, SWIGLU_LIMIT)
                a = (g * jax.nn.sigmoid(SWIGLU_ALPHA * g) * (u + 1.0)).astype(BF16)
                wd = wd_ref[...].astype(BF16)
                for c in range(d // EXP_DN):
                    cols = slice(c * EXP_DN, (c + 1) * EXP_DN)
                    y_ref[0:m, cols] += jnp.dot(a, wd[:, cols], preferred_element_type=F32)


def _experts(sb_expert, sb_nvalid, n_used, xs, layer, w_gate, b_gate, w_up, b_up, w_down, b_down, nsb_max):
    d = w_gate.shape[2]
    dff = w_gate.shape[3]
    nf = dff // EXP_FF

    def xmap(s, f, sbe, nval, nused):
        return (jnp.minimum(s, nused[0] - 1), 0)

    def ff(s, f, nused):
        return jnp.where(s < nused[0], f, nf - 1)

    b_gate4 = b_gate.reshape(b_gate.shape[0], N_EXPERTS, 1, dff)
    b_up4 = b_up.reshape(b_up.shape[0], N_EXPERTS, 1, dff)
    b_down4 = b_down.reshape(b_down.shape[0], N_EXPERTS, 1, d)
    return pl.pallas_call(
        _expert_kernel,
        out_shape=jax.ShapeDtypeStruct((nsb_max * EXP_ROWS, d), F32),
        grid_spec=pltpu.PrefetchScalarGridSpec(
            num_scalar_prefetch=3,
            grid=(nsb_max, nf),
            in_specs=[
                pl.BlockSpec((EXP_ROWS, d // 2), xmap),
                pl.BlockSpec((None, None, d, EXP_FF), lambda s, f, sbe, nval, nused: (layer, sbe[s], 0, ff(s, f, nused))),
                pl.BlockSpec((None, None, 1, EXP_FF), lambda s, f, sbe, nval, nused: (layer, sbe[s], 0, ff(s, f, nused))),
                pl.BlockSpec((None, None, d, EXP_FF), lambda s, f, sbe, nval, nused: (layer, sbe[s], 0, ff(s, f, nused))),
                pl.BlockSpec((None, None, 1, EXP_FF), lambda s, f, sbe, nval, nused: (layer, sbe[s], 0, ff(s, f, nused))),
                pl.BlockSpec((None, None, EXP_FF, d), lambda s, f, sbe, nval, nused: (layer, sbe[s], ff(s, f, nused), 0)),
                pl.BlockSpec((None, None, 1, d), lambda s, f, sbe, nval, nused: (layer, sbe[s], 0, 0)),
            ],
            out_specs=pl.BlockSpec((EXP_ROWS, d), lambda s, f, sbe, nval, nused: (s, 0)),
        ),
        compiler_params=_params(("arbitrary", "arbitrary")),
        name="experts",
    )(sb_expert, sb_nvalid, n_used, xs, w_gate, b_gate4, w_up, b_up4, w_down, b_down4)


def _combine_kernel(dest_hbm, y_hbm, x_ref, gate_ref, fg_ref, o_ref, dest0, dest1, ybuf0, ybuf1, sems, *,
                    final_norm):
    dest_smem = (dest0, dest1)
    ybuf = (ybuf0, ybuf1)
    i = pl.program_id(0)
    nsteps = pl.num_programs(0)
    tm = x_ref.shape[0]
    n = tm * TOP_K
    slot = i % 2
    nslot = 1 - slot

    def idx_copy(step, sl):
        return pltpu.make_async_copy(dest_hbm.at[pl.ds(step * n, n)], dest_smem[sl], sems.at[sl])

    def issue_rows(sl):
        def issue(t, _):
            for k in range(TOP_K):
                pltpu.make_async_copy(y_hbm.at[pl.ds(dest_smem[sl][t * TOP_K + k], 1)],
                                      ybuf[sl].at[pl.ds(k * tm + t, 1)], sems.at[2 + sl]).start()
            return 0

        lax.fori_loop(0, tm, issue, 0, unroll=2)

    @pl.when(i == 0)
    def _():
        first = idx_copy(0, 0)
        first.start()
        first.wait()
        issue_rows(0)

        @pl.when(nsteps > 1)
        def _():
            idx_copy(1, 1).start()

    for sl in range(2):
        @pl.when(jnp.logical_and(i + 1 < nsteps, nslot == sl))
        def _(sl=sl):
            idx_copy(i + 1, sl).wait()
            issue_rows(sl)

    for sl in range(2):
        @pl.when(slot == sl)
        def _(sl=sl):
            @pl.when(i + 2 < nsteps)
            def _():
                idx_copy(i + 2, sl).start()

            pltpu.make_async_copy(y_hbm.at[pl.ds(0, n)], ybuf[sl], sems.at[2 + sl]).wait()
            gate = gate_ref[...]
            acc = gate[:, 0:1] * ybuf[sl][0:tm]
            for k in range(1, TOP_K):
                acc = acc + gate[:, k:k + 1] * ybuf[sl][k * tm:(k + 1) * tm]
            out = x_ref[...] + acc
            if final_norm:
                out = _rms(out) * fg_ref[...]
            o_ref[...] = out


def _combine(dest_flat, y, x, gates, final_g, final_norm):
    t, d = x.shape
    tm = COMBINE_TM
    return pl.pallas_call(
        functools.partial(_combine_kernel, final_norm=final_norm),
        out_shape=jax.ShapeDtypeStruct((t, d), F32),
        grid=(t // tm,),
        in_specs=[
            pl.BlockSpec(memory_space=pl.ANY),
            pl.BlockSpec(memory_space=pl.ANY),
            pl.BlockSpec((tm, d), lambda i: (i, 0)),
            pl.BlockSpec((tm, LANES), lambda i: (i, 0)),
            pl.BlockSpec((1, d), lambda i: (0, 0)),
        ],
        out_specs=pl.BlockSpec((tm, d), lambda i: (i, 0)),
        scratch_shapes=[pltpu.SMEM((tm * TOP_K,), jnp.int32), pltpu.SMEM((tm * TOP_K,), jnp.int32),
                        pltpu.VMEM((TOP_K * tm, d), F32), pltpu.VMEM((TOP_K * tm, d), F32),
                        pltpu.SemaphoreType.DMA((4,))],
        compiler_params=_params(("arbitrary",)),
        name="combine",
    )(dest_flat, y, x, gates, final_g.reshape(1, d))


def _in_proj_weights(w_in):
    d = w_in.shape[0]
    o_gate = 3 * D_FOX
    o_sb = o_gate + N_FOX
    o_cq = o_sb + 3 * D_SB
    o_ckv = o_cq + Q_LORA
    o_kr = o_ckv + KV_LORA
    o_sgu = o_kr + MLA_ROPE
    half = MLA_ROPE // 2
    w_qkv = jnp.concatenate([w_in[:, :o_gate], w_in[:, o_sb:o_cq]], axis=1).astype(BF16)
    kr = w_in[:, o_kr:o_sgu]
    zpad = jnp.zeros((d, LANES - MLA_ROPE), w_in.dtype)
    w_misc = jnp.concatenate([
        w_in[:, o_sgu:o_sgu + 2 * D_SGU],
        w_in[:, o_cq:o_ckv],
        w_in[:, o_ckv:o_kr],
        kr, zpad,
        kr[:, half:], kr[:, :half], zpad,
        w_in[:, o_gate:o_sb], jnp.zeros((d, LANES - N_FOX), w_in.dtype),
    ], axis=1).astype(BF16)
    assert w_misc.shape[1] == MISC_W
    return w_qkv, w_misc


def _mla_weights(w_q_b, w_kv_b):
    half = MLA_ROPE // 2
    wq = w_q_b.reshape(Q_LORA, N_MLA, MLA_NOPE + MLA_ROPE)
    x1 = wq[:, :, MLA_NOPE:MLA_NOPE + half]
    x2 = wq[:, :, MLA_NOPE + half:]
    z = jnp.zeros((Q_LORA, N_MLA, LANES - MLA_ROPE), w_q_b.dtype)
    wq_r = jnp.concatenate([wq[:, :, :MLA_NOPE], x1, x2, z, x2, x1, z], axis=2)
    wq_r = wq_r.reshape(Q_LORA, N_MLA * 3 * LANES).astype(BF16)
    wkv = w_kv_b.reshape(KV_LORA, N_MLA, MLA_NOPE + MLA_V)
    wkv_r = jnp.concatenate([wkv[:, :, :MLA_NOPE].reshape(KV_LORA, -1),
                             wkv[:, :, MLA_NOPE:].reshape(KV_LORA, -1)], axis=1).astype(BF16)
    return wq_r, wkv_r


def _routing_tables(idx, rank, cnt, n_tokens):
    nsb_max = N_EXPERTS + (n_tokens * TOP_K) // EXP_ROWS
    counts = cnt[0, :N_EXPERTS]
    nsb_e = (counts + EXP_ROWS - 1) // EXP_ROWS
    sb_end = jnp.cumsum(nsb_e)
    sb_start = sb_end - nsb_e
    n_used = sb_end[-1]
    top_idx = idx[:, :TOP_K]
    dest = (sb_start * EXP_ROWS)[top_idx] + rank[:, :TOP_K]
    s_ids = jnp.arange(nsb_max, dtype=jnp.int32)
    sb_e = jnp.minimum(jnp.searchsorted(sb_end, s_ids, side='right'), N_EXPERTS - 1).astype(jnp.int32)
    last_e = sb_e[jnp.maximum(n_used - 1, 0)]
    sb_e = jnp.where(s_ids < n_used, sb_e, last_e)
    nval = jnp.clip(counts[sb_e] - (s_ids - sb_start[sb_e]) * EXP_ROWS, 0, EXP_ROWS)
    nval = jnp.where(s_ids < n_used, nval, 0).astype(jnp.int32)
    return dest.reshape(-1).astype(jnp.int32), sb_e, nval, n_used.reshape(1).astype(jnp.int32), nsb_max


def _layer(x, pos, l, p, final_g, final_norm, batch, seq):
    t = x.shape[0]
    w_qkv, w_misc = _in_proj_weights(p["w_in"][l])
    qkv = _norm_matmul(x, p["norm_mix_g"][l], w_qkv, BF16, 512, 1536, "in_proj_qkv")
    misc = _norm_matmul(x, p["norm_mix_g"][l], w_misc, F32, 512, MISC_W, "in_proj_misc")

    crow = _fox_prep(misc, p["b_forget"][l], batch, seq)
    nkb = seq // ATT_TILE
    crow = crow.reshape(batch, nkb, N_FOX // FOX_HEADS, FOX_HEADS, ATT_TILE).transpose(0, 2, 1, 3, 4)
    o_fox = _fox_attention(qkv, crow, batch, seq)
    o_sb = _sb_attention(qkv, batch, seq)
    wq_r, wkv_r = _mla_weights(p["mla_w_q_b"][l], p["mla_w_kv_b"][l])
    q_m, k_m, v_m = _mla_prep(misc, pos, p["mla_q_norm_g"][l], p["mla_kv_norm_g"][l], wq_r, wkv_r)
    o_mla = _mla_attention(q_m, k_m, v_m, batch, seq)
    o_sgu = _sgu(misc, p["sgu_ln_g"][l], p["sgu_ln_b"][l], p["sgu_w_s"][l], p["sgu_b_s"][l])

    x_new, h2p, idx, gates, rank, cnt = _outproj_router(
        o_fox, o_sb, o_mla, o_sgu, x, p["out_norm_g"][l], p["w_o"][l].astype(BF16),
        p["norm_ffn_g"][l], p["router_w"][l], p["router_b"][l])

    dest, sb_e, nval, n_used, nsb_max = _routing_tables(idx, rank, cnt, t)
    xs = _scatter_rows(dest, h2p, nsb_max * EXP_ROWS)
    y = _experts(sb_e, nval, n_used, xs, l, p["w_gate"], p["b_gate"], p["w_up"], p["b_up"],
                 p["w_down"], p["b_down"], nsb_max)
    return _combine(dest, y, x_new, gates, final_g, final_norm)


def kernel(x, positions, norm_mix_g, w_in, b_forget, mla_q_norm_g, mla_kv_norm_g, mla_w_q_b, mla_w_kv_b,
           sgu_ln_g, sgu_ln_b, sgu_w_s, sgu_b_s, out_norm_g, w_o, norm_ffn_g, router_w, router_b, w_gate,
           b_gate, w_up, b_up, w_down, b_down, final_norm_g):
    batch, seq, d = x.shape
    depth = w_in.shape[0]
    p = dict(norm_mix_g=norm_mix_g, w_in=w_in, b_forget=b_forget, mla_q_norm_g=mla_q_norm_g,
             mla_kv_norm_g=mla_kv_norm_g, mla_w_q_b=mla_w_q_b, mla_w_kv_b=mla_w_kv_b, sgu_ln_g=sgu_ln_g,
             sgu_ln_b=sgu_ln_b, sgu_w_s=sgu_w_s, sgu_b_s=sgu_b_s, out_norm_g=out_norm_g, w_o=w_o,
             norm_ffn_g=norm_ffn_g, router_w=router_w, router_b=router_b, w_gate=w_gate, b_gate=b_gate,
             w_up=w_up, b_up=b_up, w_down=w_down, b_down=b_down)
    h = x.reshape(batch * seq, d)
    pos = positions.reshape(batch * seq, 1)
    for l in range(depth):
        h = _layer(h, pos, l, p, final_norm_g, l == depth - 1, batch, seq)
    return h.reshape(batch, seq, d)
```

```python
import functools

import numpy as np
import jax
import jax.numpy as jnp
from jax import lax
from jax.experimental import pallas as pl
from jax.experimental.pallas import tpu as pltpu

HEAD_DIM = 64
N_FOX = 8
N_SB = 8
N_MLA = 4
MLA_NOPE = 128
MLA_ROPE = 64
MLA_V = 128
Q_LORA = 512
KV_LORA = 256
N_SGU = 8
SGU_CH = 64
CHUNK = 128
ROPE_THETA = 10000.0
D_FOX = N_FOX * HEAD_DIM
D_SB = N_SB * HEAD_DIM
D_MLA = N_MLA * MLA_V
D_SGU = N_SGU * SGU_CH
N_EXPERTS = 32
TOP_K = 4
SWIGLU_LIMIT = 7.0
SWIGLU_ALPHA = 1.702
RMS_EPS = 1e-6
LN_EPS = 1e-5

LANES = 128
VMEM_LIMIT = 56 * 1024 * 1024

ATT_TILE = 256
FOX_HEADS = 4
MLA_HEADS = 4
SB_HEADS = 4
EXP_ROWS = 1536
EXP_SUB = 256
EXP_FF = 256
EXP_DN = 512
SCATTER_TM = 256
COMBINE_TM = 256

F32 = jnp.float32
BF16 = jnp.bfloat16

MISC_U = 0
MISC_V = 512
MISC_CQ = 1024
MISC_CKV = 1536
MISC_KR1 = 1792
MISC_KR2 = 1920
MISC_GATE = 2048
MISC_W = 2176


def _params(sem, vmem=VMEM_LIMIT):
    return pltpu.CompilerParams(dimension_semantics=sem, vmem_limit_bytes=vmem)


def _split3(x):
    hi = x.astype(BF16)
    r1 = x - hi.astype(F32)
    mid = r1.astype(BF16)
    lo = (r1 - mid.astype(F32)).astype(BF16)
    return hi, mid, lo


def _log_sigmoid_pair(z):
    t = jnp.log1p(jnp.exp(-jnp.abs(z)))
    return jnp.minimum(z, 0.0) - t, jnp.minimum(-z, 0.0) - t


def _rms(x):
    return x * lax.rsqrt(jnp.mean(x * x, axis=-1, keepdims=True) + RMS_EPS)


def _norm_matmul_kernel(x_ref, g_ref, w_ref, o_ref, xn_ref):
    @pl.when(pl.program_id(1) == 0)
    def _():
        xn_ref[...] = (_rms(x_ref[...]) * g_ref[...]).astype(BF16)

    o_ref[...] = jnp.dot(xn_ref[...], w_ref[...], preferred_element_type=F32).astype(o_ref.dtype)


def _norm_matmul(x, g, w, out_dtype, tm, tn, name):
    t, d = x.shape
    n = w.shape[1]
    return pl.pallas_call(
        _norm_matmul_kernel,
        out_shape=jax.ShapeDtypeStruct((t, n), out_dtype),
        grid=(t // tm, n // tn),
        in_specs=[
            pl.BlockSpec((tm, d), lambda i, j: (i, 0)),
            pl.BlockSpec((1, d), lambda i, j: (0, 0)),
            pl.BlockSpec((d, tn), lambda i, j: (0, j)),
        ],
        out_specs=pl.BlockSpec((tm, tn), lambda i, j: (i, j)),
        scratch_shapes=[pltpu.VMEM((tm, d), BF16)],
        compiler_params=_params(("parallel", "arbitrary")),
        name=name,
    )(x, g.reshape(1, d), w)


def _fox_prep_kernel(gate_ref, bf_ref, crow_ref):
    s_len = gate_ref.shape[0]
    bl = ATT_TILE
    r = lax.broadcasted_iota(jnp.int32, (bl, bl), 0)
    c = lax.broadcasted_iota(jnp.int32, (bl, bl), 1)
    tri = (c <= r).astype(BF16)
    carry = jnp.zeros((1, LANES), F32)
    for i in range(s_len // bl):
        z = gate_ref[i * bl:(i + 1) * bl, :] + bf_ref[...]
        lf, _ = _log_sigmoid_pair(z)
        hi, mid, lo = _split3(lf)
        cs = (jnp.dot(tri, hi, preferred_element_type=F32)
              + jnp.dot(tri, mid, preferred_element_type=F32)
              + jnp.dot(tri, lo, preferred_element_type=F32)) + carry
        crow_ref[i] = cs.T[0:N_FOX, :]
        carry = cs[bl - 1:bl, :]


def _fox_prep(misc, b_forget, batch, seq):
    nkb = seq // ATT_TILE
    bf = jnp.zeros((1, LANES), F32).at[0, :N_FOX].set(b_forget)
    return pl.pallas_call(
        _fox_prep_kernel,
        out_shape=jax.ShapeDtypeStruct((batch, nkb, N_FOX, ATT_TILE), F32),
        grid=(batch,),
        in_specs=[
            pl.BlockSpec((seq, LANES), lambda b: (b, MISC_GATE // LANES)),
            pl.BlockSpec((1, LANES), lambda b: (0, 0)),
        ],
        out_specs=pl.BlockSpec((None, nkb, N_FOX, ATT_TILE), lambda b: (b, 0, 0, 0)),
        compiler_params=_params(("parallel",)),
        name="fox_prep",
    )(misc, bf)


def _causal_mask(tq):
    row = lax.broadcasted_iota(jnp.int32, (tq, tq), 0)
    col = lax.broadcasted_iota(jnp.int32, (tq, tq), 1)
    return col <= row, col < row


def _softmax_sweep(qs, k_of, v_of, bias_of, scale, qi, tq, ones_lane=None):
    nh = len(qs)
    incl, _ = _causal_mask(tq)

    def step(kb, carry, diag):
        ks = pl.multiple_of(kb * tq, tq)
        ss = [lax.dot_general(qs[h], k_of(h, ks), (((1,), (1,)), ((), ())), preferred_element_type=F32)
              for h in range(nh)]
        ps, stats = [], []
        for h in range(nh):
            m, l, _ = carry[h]
            s = ss[h]
            if scale is not None:
                s = s * scale
            b = bias_of(h, kb)
            if b is not None:
                s = s + b
            if diag:
                s = jnp.where(incl, s, -jnp.inf)
            m_new = jnp.maximum(m, jnp.max(s, axis=-1, keepdims=True))
            alpha = jnp.exp(m - m_new)
            p = jnp.exp(s - m_new)
            if ones_lane is None:
                l = alpha * l + jnp.sum(p, axis=-1, keepdims=True)
            stats.append((m_new, l, alpha))
            ps.append(p.astype(BF16))
        out = []
        for h in range(nh):
            m_new, l, alpha = stats[h]
            acc = alpha * carry[h][2] + jnp.dot(ps[h], v_of(h, ks), preferred_element_type=F32)
            out.append((m_new, l, acc))
        return tuple(out)

    dv = v_of(0, 0).shape[-1]
    l0 = jnp.zeros((tq, 1), F32) if ones_lane is None else None
    init = tuple((jnp.full((tq, 1), -jnp.inf, F32), l0, jnp.zeros((tq, dv), F32)) for _ in range(nh))
    carry = lax.fori_loop(0, qi, lambda kb, c: step(kb, c, False), init)
    carry = step(qi, carry, True)
    if ones_lane is None:
        return [acc / l for (_, l, acc) in carry]
    return [acc / acc[:, ones_lane(h):ones_lane(h) + 1] for h, (_, _, acc) in enumerate(carry)]


def _fox_attn_kernel(q_ref, k_ref, v_ref, crow_ref, o_ref):
    tq = q_ref.shape[0]
    qi = pl.program_id(2)
    q = q_ref[...] * jnp.asarray(HEAD_DIM ** -0.5, BF16)
    qs = [q[:, j * HEAD_DIM:(j + 1) * HEAD_DIM] for j in range(FOX_HEADS)]

    def k_of(h, ks):
        return k_ref[pl.ds(ks, tq), h * HEAD_DIM:(h + 1) * HEAD_DIM]

    first = lax.broadcasted_iota(jnp.int32, (tq, LANES), 1) < HEAD_DIM
    one = jnp.ones((tq, LANES), BF16)

    def v_of(h, ks):
        pair = v_ref[pl.ds(ks, tq), (h // 2) * LANES:(h // 2 + 1) * LANES]
        return jnp.where(first, pair, one) if h % 2 == 0 else jnp.where(first, one, pair)

    def bias_of(h, kb):
        return -crow_ref[kb][h:h + 1, :]

    outs = _softmax_sweep(qs, k_of, v_of, bias_of, None, qi, tq,
                          ones_lane=lambda h: HEAD_DIM if h % 2 == 0 else 0)
    o_ref[...] = jnp.concatenate([jnp.where(first, outs[h], outs[h + 1]) for h in range(0, FOX_HEADS, 2)],
                                 axis=-1)


def _fox_attention(qkv, crow, batch, seq):
    t = qkv.shape[0]
    tq = ATT_TILE
    nq = seq // tq
    npair = N_FOX // FOX_HEADS
    w = FOX_HEADS * HEAD_DIM
    return pl.pallas_call(
        _fox_attn_kernel,
        out_shape=jax.ShapeDtypeStruct((t, D_FOX), F32),
        grid=(batch, npair, nq),
        in_specs=[
            pl.BlockSpec((tq, w), lambda b, h, i: (b * nq + i, h)),
            pl.BlockSpec((seq, w), lambda b, h, i: (b, npair + h)),
            pl.BlockSpec((seq, w), lambda b, h, i: (b, 2 * npair + h)),
            pl.BlockSpec((None, None, nq, FOX_HEADS, tq), lambda b, h, i: (b, h, 0, 0, 0)),
        ],
        out_specs=pl.BlockSpec((tq, w), lambda b, h, i: (b * nq + i, h)),
        compiler_params=_params(("parallel", "parallel", "arbitrary")),
        name="fox_attn",
    )(qkv, qkv, qkv, crow)


def _mla_attn_kernel(q_ref, k_ref, v_ref, o_ref):
    tq = q_ref.shape[0]
    qi = pl.program_id(2)
    dk = 2 * LANES
    scale = (MLA_NOPE + MLA_ROPE) ** -0.5
    qs = [q_ref[:, h * dk:(h + 1) * dk] for h in range(MLA_HEADS)]
    outs = _softmax_sweep(qs,
                          lambda h, ks: k_ref[pl.ds(ks, tq), h * dk:(h + 1) * dk],
                          lambda h, ks: v_ref[pl.ds(ks, tq), h * MLA_V:(h + 1) * MLA_V],
                          lambda h, kb: None, scale, qi, tq)
    o_ref[...] = jnp.concatenate(outs, axis=-1)


def _mla_attention(q, k, v, batch, seq):
    t = q.shape[0]
    tq = ATT_TILE
    nq = seq // tq
    dk = MLA_HEADS * 2 * LANES
    dv = MLA_HEADS * MLA_V
    return pl.pallas_call(
        _mla_attn_kernel,
        out_shape=jax.ShapeDtypeStruct((t, D_MLA), F32),
        grid=(batch, N_MLA // MLA_HEADS, nq),
        in_specs=[
            pl.BlockSpec((tq, dk), lambda b, h, i: (b * nq + i, h)),
            pl.BlockSpec((seq, dk), lambda b, h, i: (b, h)),
            pl.BlockSpec((seq, dv), lambda b, h, i: (b, h)),
        ],
        out_specs=pl.BlockSpec((tq, dv), lambda b, h, i: (b * nq + i, h)),
        compiler_params=_params(("parallel", "parallel", "arbitrary")),
        name="mla_attn",
    )(q, k, v)


def _sb_attn_kernel(q_ref, k_ref, v_ref, o_ref):
    tq = q_ref.shape[0]
    qi = pl.program_id(2)
    q = q_ref[...] * jnp.asarray(HEAD_DIM ** -0.5, BF16)
    nh = SB_HEADS
    qs = [q[:, j * HEAD_DIM:(j + 1) * HEAD_DIM] for j in range(nh)]
    _, strict = _causal_mask(tq)
    row = lax.broadcasted_iota(jnp.int32, (tq, tq), 0)
    col = lax.broadcasted_iota(jnp.int32, (tq, tq), 1)
    later = (row > col).astype(BF16)

    def step(kb, carry, diag):
        ks = pl.multiple_of(kb * tq, tq)
        zs = [lax.dot_general(qs[h], k_ref[pl.ds(ks, tq), h * HEAD_DIM:(h + 1) * HEAD_DIM],
                              (((1,), (1,)), ((), ())), preferred_element_type=F32) for h in range(nh)]
        lss, lrs = [], []
        for h in range(nh):
            z = zs[h]
            ls = jnp.minimum(z, 0.0) - jnp.log(1.0 + jnp.exp(-jnp.abs(z)))
            lr = ls - z
            if diag:
                lr = jnp.where(strict, lr, 0.0)
            lss.append(ls)
            lrs.append(lr)
        afters = []
        for h in range(nh):
            hi = lrs[h].astype(BF16)
            lo = (lrs[h] - hi.astype(F32)).astype(BF16)
            afters.append((jnp.dot(hi, later, preferred_element_type=F32)
                           + jnp.dot(lo, later, preferred_element_type=F32)) + carry[h][0])
        out = []
        for h in range(nh):
            a = jnp.exp(lss[h] + afters[h])
            if diag:
                a = jnp.where(strict, a, 0.0)
            v = v_ref[pl.ds(ks, tq), h * HEAD_DIM:(h + 1) * HEAD_DIM]
            acc = carry[h][1] + jnp.dot(a.astype(BF16), v, preferred_element_type=F32)
            out.append((afters[h][:, 0:1] + lrs[h][:, 0:1], acc))
        return tuple(out)

    init = tuple((jnp.zeros((tq, 1), F32), jnp.zeros((tq, HEAD_DIM), F32)) for _ in range(nh))
    carry = step(qi, init, True)
    carry = lax.fori_loop(0, qi, lambda i, c: step(qi - 1 - i, c, False), carry)
    o_ref[...] = jnp.concatenate([acc for (_, acc) in carry], axis=-1)


def _sb_attention(qkv, batch, seq):
    t = qkv.shape[0]
    tq = ATT_TILE
    nq = seq // tq
    npair = N_SB // SB_HEADS
    w = SB_HEADS * HEAD_DIM
    base = 3 * D_FOX // w
    return pl.pallas_call(
        _sb_attn_kernel,
        out_shape=jax.ShapeDtypeStruct((t, D_SB), F32),
        grid=(batch, npair, nq),
        in_specs=[
            pl.BlockSpec((tq, w), lambda b, h, i: (b * nq + i, base + h)),
            pl.BlockSpec((seq, w), lambda b, h, i: (b, base + npair + h)),
            pl.BlockSpec((seq, w), lambda b, h, i: (b, base + 2 * npair + h)),
        ],
        out_specs=pl.BlockSpec((tq, w), lambda b, h, i: (b * nq + i, h)),
        compiler_params=_params(("parallel", "parallel", "arbitrary")),
        name="sb_attn",
    )(qkv, qkv, qkv)


def _mla_prep_kernel(cq_ref, ckv_ref, kr1_ref, kr2_ref, pos_ref, gq_ref, gkv_ref, wq_ref, wkv_ref,
                     invf_ref, sgn_ref, q_out, k_out, v_out):
    ang = pos_ref[...].astype(F32) * invf_ref[...]
    cosv = jnp.cos(ang)
    sinv = jnp.sin(ang) * sgn_ref[...]
    qn = (_rms(cq_ref[...]) * gq_ref[...]).astype(BF16)
    qa = jnp.dot(qn, wq_ref[...], preferred_element_type=F32)
    for h in range(N_MLA):
        o = h * 3 * LANES
        pe = qa[:, o + LANES:o + 2 * LANES] * cosv + qa[:, o + 2 * LANES:o + 3 * LANES] * sinv
        q_out[:, h * 2 * LANES:h * 2 * LANES + LANES] = qa[:, o:o + LANES].astype(BF16)
        q_out[:, h * 2 * LANES + LANES:(h + 1) * 2 * LANES] = pe.astype(BF16)
    kvn = (_rms(ckv_ref[...]) * gkv_ref[...]).astype(BF16)
    kva = jnp.dot(kvn, wkv_ref[...], preferred_element_type=F32)
    kpe = (kr1_ref[...] * cosv + kr2_ref[...] * sinv).astype(BF16)
    for h in range(N_MLA):
        k_out[:, h * 2 * LANES:h * 2 * LANES + LANES] = kva[:, h * LANES:(h + 1) * LANES].astype(BF16)
        k_out[:, h * 2 * LANES + LANES:(h + 1) * 2 * LANES] = kpe
    v_out[...] = kva[:, N_MLA * MLA_NOPE:].astype(BF16)


def _mla_prep(misc, pos, gq, gkv, wq, wkv, tm=512):
    t = misc.shape[0]
    half = MLA_ROPE // 2
    inv_freq = ROPE_THETA ** (-jnp.arange(half, dtype=F32) / half)
    invf = jnp.tile(inv_freq, LANES // half).reshape(1, LANES)
    sgn = jnp.tile(jnp.concatenate([-jnp.ones((half,), F32), jnp.ones((half,), F32)]),
                   LANES // MLA_ROPE).reshape(1, LANES)
    const = lambda i: (0, 0)
    return pl.pallas_call(
        _mla_prep_kernel,
        out_shape=(jax.ShapeDtypeStruct((t, N_MLA * 2 * LANES), BF16),
                   jax.ShapeDtypeStruct((t, N_MLA * 2 * LANES), BF16),
                   jax.ShapeDtypeStruct((t, D_MLA), BF16)),
        grid=(t // tm,),
        in_specs=[
            pl.BlockSpec((tm, Q_LORA), lambda i: (i, MISC_CQ // Q_LORA)),
            pl.BlockSpec((tm, KV_LORA), lambda i: (i, MISC_CKV // KV_LORA)),
            pl.BlockSpec((tm, LANES), lambda i: (i, MISC_KR1 // LANES)),
            pl.BlockSpec((tm, LANES), lambda i: (i, MISC_KR2 // LANES)),
            pl.BlockSpec((tm, 1), lambda i: (i, 0)),
            pl.BlockSpec((1, Q_LORA), const),
            pl.BlockSpec((1, KV_LORA), const),
            pl.BlockSpec(wq.shape, const),
            pl.BlockSpec(wkv.shape, const),
            pl.BlockSpec((1, LANES), const),
            pl.BlockSpec((1, LANES), const),
        ],
        out_specs=(pl.BlockSpec((tm, N_MLA * 2 * LANES), lambda i: (i, 0)),
                   pl.BlockSpec((tm, N_MLA * 2 * LANES), lambda i: (i, 0)),
                   pl.BlockSpec((tm, D_MLA), lambda i: (i, 0))),
        compiler_params=_params(("parallel",)),
        name="mla_prep",
    )(misc, misc, misc, misc, pos, gq.reshape(1, -1), gkv.reshape(1, -1), wq, wkv, invf, sgn)


def _gelu(x):
    return 0.5 * x * (1.0 + lax.erf(x * np.float32(np.sqrt(0.5))))


def _sgu_kernel(u_ref, v_ref, lng_ref, lnb_ref, w_ref, bias_ref, o_ref):
    tm = u_ref.shape[0]
    v = _gelu(v_ref[...])
    mu = jnp.mean(v, axis=-1, keepdims=True)
    xc = v - mu
    var = jnp.mean(xc * xc, axis=-1, keepdims=True)
    vb = (xc * lax.rsqrt(var + LN_EPS) * lng_ref[...] + lnb_ref[...]).astype(BF16)
    row = lax.broadcasted_iota(jnp.int32, (CHUNK, CHUNK), 0)
    col = lax.broadcasted_iota(jnp.int32, (CHUNK, LANES), 1)
    tril = lax.broadcasted_iota(jnp.int32, (CHUNK, CHUNK), 1) <= row
    first = col < SGU_CH
    ws = [jnp.where(tril, w_ref[g], 0.0).astype(BF16) for g in range(N_SGU)]
    zero = jnp.zeros((CHUNK, LANES), BF16)
    for c in range(tm // CHUNK):
        rows = slice(c * CHUNK, (c + 1) * CHUNK)
        for p in range(N_SGU // 2):
            cols = slice(p * LANES, (p + 1) * LANES)
            vp = vb[rows, cols]
            mixed = (jnp.dot(ws[2 * p], jnp.where(first, vp, zero), preferred_element_type=F32)
                     + jnp.dot(ws[2 * p + 1], jnp.where(first, zero, vp), preferred_element_type=F32))
            o_ref[rows, cols] = _gelu(u_ref[rows, cols]) * (mixed + bias_ref[:, cols])


def _sgu(misc, ln_g, ln_b, w_s, b_s, tm=512):
    t = misc.shape[0]
    bias = jnp.repeat(b_s.T, SGU_CH, axis=1)
    const2 = lambda i: (0, 0)
    return pl.pallas_call(
        _sgu_kernel,
        out_shape=jax.ShapeDtypeStruct((t, D_SGU), F32),
        grid=(t // tm,),
        in_specs=[
            pl.BlockSpec((tm, D_SGU), lambda i: (i, MISC_U // D_SGU)),
            pl.BlockSpec((tm, D_SGU), lambda i: (i, MISC_V // D_SGU)),
            pl.BlockSpec((1, D_SGU), const2),
            pl.BlockSpec((1, D_SGU), const2),
            pl.BlockSpec((N_SGU, CHUNK, CHUNK), lambda i: (0, 0, 0)),
            pl.BlockSpec((CHUNK, D_SGU), const2),
        ],
        out_specs=pl.BlockSpec((tm, D_SGU), lambda i: (i, 0)),
        compiler_params=_params(("parallel",)),
        name="sgu",
    )(misc, misc, ln_g.reshape(1, -1), ln_b.reshape(1, -1), w_s, bias)


def _pack_bf16_pairs(hb):
    n = hb.shape[1] // 2
    bits = pltpu.bitcast(hb.astype(F32), jnp.uint32)
    return (bits[:, n:] & jnp.uint32(0xFFFF0000)) | (bits[:, :n] >> 16)


def _unpack_bf16_pairs(xu):
    lo = pltpu.bitcast(xu << 16, F32).astype(BF16)
    hi = pltpu.bitcast(xu & jnp.uint32(0xFFFF0000), F32).astype(BF16)
    return jnp.concatenate([lo, hi], axis=1)


def _outproj_router_kernel(of_ref, os_ref, om_ref, og_ref, x_ref, ong_ref, wo_ref, nfg_ref, rw_ref,
                           rb_ref, xo_ref, h2_ref, idx_ref, gate_ref, rank_ref, cnt_ref, carry_ref):
    tm = x_ref.shape[0]

    @pl.when(pl.program_id(0) == 0)
    def _():
        carry_ref[...] = jnp.zeros_like(carry_ref)

    o = jnp.concatenate([_rms(r[...]) for r in (of_ref, os_ref, om_ref, og_ref)], axis=-1)
    o = (o * ong_ref[...]).astype(BF16)
    xn = x_ref[...] + jnp.dot(o, wo_ref[...], preferred_element_type=F32)
    xo_ref[...] = xn
    hb = (_rms(xn) * nfg_ref[...]).astype(BF16)
    h2_ref[...] = _pack_bf16_pairs(hb)

    lane = lax.broadcasted_iota(jnp.int32, (tm, LANES), 1)
    logits = jnp.dot(hb, rw_ref[...], preferred_element_type=F32) + rb_ref[...]
    vals = jnp.where(lane < N_EXPERTS, logits, -jnp.inf)
    sels, tops = [], []
    for _ in range(TOP_K):
        m = jnp.max(vals, axis=-1, keepdims=True)
        idx = jnp.min(jnp.where(vals == m, lane, LANES), axis=-1, keepdims=True)
        sel = lane == idx
        vals = jnp.where(sel, -jnp.inf, vals)
        sels.append(sel)
        tops.append((m, idx))
    es = [jnp.exp(m - tops[0][0]) for (m, _) in tops]
    den = es[0] + es[1] + es[2] + es[3]

    multi = sels[0] | sels[1] | sels[2] | sels[3]
    mh = jnp.where(multi, 1.0, 0.0)
    r = lax.broadcasted_iota(jnp.int32, (tm, tm), 0)
    c = lax.broadcasted_iota(jnp.int32, (tm, tm), 1)
    before = (c < r).astype(BF16)
    cnt = jnp.dot(before, mh.astype(BF16), preferred_element_type=F32) + carry_ref[...]
    total = cnt[tm - 1:tm, :] + mh[tm - 1:tm, :]
    carry_ref[...] = total
    cnt_ref[...] = jnp.broadcast_to(total, cnt_ref.shape).astype(jnp.int32)

    idx_o = jnp.zeros((tm, LANES), jnp.int32)
    gate_o = jnp.zeros((tm, LANES), F32)
    rank_o = jnp.zeros((tm, LANES), jnp.int32)
    for k in range(TOP_K):
        rank_k = jnp.sum(jnp.where(sels[k], cnt, 0.0), axis=-1, keepdims=True).astype(jnp.int32)
        idx_o = jnp.where(lane == k, tops[k][1], idx_o)
        gate_o = jnp.where(lane == k, es[k] / den, gate_o)
        rank_o = jnp.where(lane == k, rank_k, rank_o)
    idx_ref[...] = idx_o
    gate_ref[...] = gate_o
    rank_ref[...] = rank_o


def _outproj_router(o_fox, o_sb, o_mla, o_sgu, x, out_norm_g, w_o, norm_ffn_g, router_w, router_b, tm=512):
    t, d = x.shape
    dg = o_fox.shape[1]
    rw = jnp.zeros((d, LANES), BF16).at[:, :N_EXPERTS].set(router_w.astype(BF16))
    rb = jnp.zeros((1, LANES), F32).at[0, :N_EXPERTS].set(router_b)
    const = lambda i: (0, 0)
    grp = pl.BlockSpec((tm, dg), lambda i: (i, 0))
    lanes_out = pl.BlockSpec((tm, LANES), lambda i: (i, 0))
    return pl.pallas_call(
        _outproj_router_kernel,
        out_shape=(jax.ShapeDtypeStruct((t, d), F32),
                   jax.ShapeDtypeStruct((t, d // 2), jnp.uint32),
                   jax.ShapeDtypeStruct((t, LANES), jnp.int32),
                   jax.ShapeDtypeStruct((t, LANES), F32),
                   jax.ShapeDtypeStruct((t, LANES), jnp.int32),
                   jax.ShapeDtypeStruct((8, LANES), jnp.int32)),
        grid=(t // tm,),
        in_specs=[grp, grp, grp, grp,
                  pl.BlockSpec((tm, d), lambda i: (i, 0)),
                  pl.BlockSpec((1, 4 * dg), const),
                  pl.BlockSpec((4 * dg, d), const),
                  pl.BlockSpec((1, d), const),
                  pl.BlockSpec((d, LANES), const),
                  pl.BlockSpec((1, LANES), const)],
        out_specs=(pl.BlockSpec((tm, d), lambda i: (i, 0)),
                   pl.BlockSpec((tm, d // 2), lambda i: (i, 0)),
                   lanes_out, lanes_out, lanes_out,
                   pl.BlockSpec((8, LANES), const)),
        scratch_shapes=[pltpu.VMEM((1, LANES), F32)],
        compiler_params=_params(("arbitrary",)),
        name="outproj_router",
    )(o_fox, o_sb, o_mla, o_sgu, x, out_norm_g.reshape(1, -1), w_o, norm_ffn_g.reshape(1, -1), rw, rb)


def _scatter_rows_kernel(dest_hbm, x_ref, xs_init_hbm, xs_hbm, dest_smem, sems):
    del xs_init_hbm
    i = pl.program_id(0)
    tm = x_ref.shape[0]
    n = tm * TOP_K
    idx_copy = pltpu.make_async_copy(dest_hbm.at[pl.ds(i * n, n)], dest_smem, sems.at[0])
    idx_copy.start()
    idx_copy.wait()

    def issue(t, _):
        for k in range(TOP_K):
            pltpu.make_async_copy(x_ref.at[pl.ds(t, 1)], xs_hbm.at[pl.ds(dest_smem[t * TOP_K + k], 1)],
                                  sems.at[1]).start()
        return 0

    lax.fori_loop(0, tm, issue, 0, unroll=2)
    for _ in range(TOP_K):
        pltpu.make_async_copy(x_ref, xs_hbm.at[pl.ds(0, tm)], sems.at[1]).wait()


def _scatter_rows(dest_flat, src, nrows, xs_init=None):
    t, w = src.shape
    tm = SCATTER_TM
    if xs_init is None:
        xs_init = jnp.zeros((nrows, w), src.dtype)
    return pl.pallas_call(
        _scatter_rows_kernel,
        out_shape=jax.ShapeDtypeStruct((nrows, w), src.dtype),
        grid=(t // tm,),
        in_specs=[pl.BlockSpec(memory_space=pl.ANY),
                  pl.BlockSpec((tm, w), lambda i: (i, 0)),
                  pl.BlockSpec(memory_space=pl.ANY)],
        out_specs=pl.BlockSpec(memory_space=pl.ANY),
        scratch_shapes=[pltpu.SMEM((tm * TOP_K,), jnp.int32), pltpu.SemaphoreType.DMA((2,))],
        input_output_aliases={2: 0},
        compiler_params=_params(("arbitrary",)),
        name="scatter_rows",
    )(dest_flat, src, xs_init)


def _expert_kernel(sbe_ref, nval_ref, nused_ref, x_ref, wg_ref, bg_ref, wu_ref, bu_ref, wd_ref, bd_ref, *rest,
                   reuse_y):
    y_ref = rest[-1]
    s = pl.program_id(0)
    f = pl.program_id(1)
    d = y_ref.shape[1]

    if not reuse_y:
        @pl.when(jnp.logical_and(s >= nused_ref[0], f == 0))
        def _():
            y_ref[...] = jnp.zeros_like(y_ref)

    @pl.when(s < nused_ref[0])
    def _():
        @pl.when(f == 0)
        def _():
            y_ref[...] = jnp.broadcast_to(bd_ref[...], y_ref.shape)

        nsub = (nval_ref[s] + EXP_SUB - 1) // EXP_SUB

        for n in range(1, EXP_ROWS // EXP_SUB + 1):
            @pl.when(nsub == n)
            def _(m=n * EXP_SUB):
                x = _unpack_bf16_pairs(x_ref[0:m, :])
                g = jnp.dot(x, wg_ref[...].astype(BF16), preferred_element_type=F32) + bg_ref[...]
                g = jnp.minimum(g, SWIGLU_LIMIT)
                u = jnp.dot(x, wu_ref[...].astype(BF16), preferred_element_type=F32) + bu_ref[...]
                u = jnp.clip(u, -SWIGLU_LIMIT, SWIGLU_LIMIT)
                a = (g * jax.nn.sigmoid(SWIGLU_ALPHA * g) * (u + 1.0)).astype(BF16)
                wd = wd_ref[...].astype(BF16)
                for c in range(d // EXP_DN):
                    cols = slice(c * EXP_DN, (c + 1) * EXP_DN)
                    y_ref[0:m, cols] += jnp.dot(a, wd[:, cols], preferred_element_type=F32)


def _experts(sb_expert, sb_nvalid, n_used, xs, layer, w_gate, b_gate, w_up, b_up, w_down, b_down, nsb_max,
             y_init=None):
    d = w_gate.shape[2]
    dff = w_gate.shape[3]
    nf = dff // EXP_FF
    reuse_y = y_init is not None

    def xmap(s, f, sbe, nval, nused):
        return (jnp.minimum(s, nused[0] - 1), 0)

    def ymap(s, f, sbe, nval, nused):
        return xmap(s, f, sbe, nval, nused) if reuse_y else (s, 0)

    def ff(s, f, nused):
        return jnp.where(s < nused[0], f, nf - 1)

    b_gate4 = b_gate.reshape(b_gate.shape[0], N_EXPERTS, 1, dff)
    b_up4 = b_up.reshape(b_up.shape[0], N_EXPERTS, 1, dff)
    b_down4 = b_down.reshape(b_down.shape[0], N_EXPERTS, 1, d)
    extra_in = [y_init] if reuse_y else []
    extra_specs = [pl.BlockSpec(memory_space=pl.ANY)] if reuse_y else []
    n_in = 3 + 7
    return pl.pallas_call(
        functools.partial(_expert_kernel, reuse_y=reuse_y),
        out_shape=jax.ShapeDtypeStruct((nsb_max * EXP_ROWS, d), F32),
        grid_spec=pltpu.PrefetchScalarGridSpec(
            num_scalar_prefetch=3,
            grid=(nsb_max, nf),
            in_specs=[
                pl.BlockSpec((EXP_ROWS, d // 2), xmap),
                pl.BlockSpec((None, None, d, EXP_FF), lambda s, f, sbe, nval, nused: (layer, sbe[s], 0, ff(s, f, nused))),
                pl.BlockSpec((None, None, 1, EXP_FF), lambda s, f, sbe, nval, nused: (layer, sbe[s], 0, ff(s, f, nused))),
                pl.BlockSpec((None, None, d, EXP_FF), lambda s, f, sbe, nval, nused: (layer, sbe[s], 0, ff(s, f, nused))),
                pl.BlockSpec((None, None, 1, EXP_FF), lambda s, f, sbe, nval, nused: (layer, sbe[s], 0, ff(s, f, nused))),
                pl.BlockSpec((None, None, EXP_FF, d), lambda s, f, sbe, nval, nused: (layer, sbe[s], ff(s, f, nused), 0)),
                pl.BlockSpec((None, None, 1, d), lambda s, f, sbe, nval, nused: (layer, sbe[s], 0, 0)),
            ] + extra_specs,
            out_specs=pl.BlockSpec((EXP_ROWS, d), ymap),
        ),
        input_output_aliases={n_in: 0} if reuse_y else {},
        compiler_params=_params(("arbitrary", "arbitrary")),
        name="experts",
    )(sb_expert, sb_nvalid, n_used, xs, w_gate, b_gate4, w_up, b_up4, w_down, b_down4, *extra_in)


def _combine_kernel(dest_hbm, y_hbm, x_ref, gate_ref, fg_ref, o_ref, dest0, dest1, ybuf0, ybuf1, sems, *,
                    final_norm):
    dest_smem = (dest0, dest1)
    ybuf = (ybuf0, ybuf1)
    i = pl.program_id(0)
    nsteps = pl.num_programs(0)
    tm = x_ref.shape[0]
    n = tm * TOP_K
    slot = i % 2
    nslot = 1 - slot

    def idx_copy(step, sl):
        return pltpu.make_async_copy(dest_hbm.at[pl.ds(step * n, n)], dest_smem[sl], sems.at[sl])

    def issue_rows(sl):
        def issue(t, _):
            for k in range(TOP_K):
                pltpu.make_async_copy(y_hbm.at[pl.ds(dest_smem[sl][t * TOP_K + k], 1)],
                                      ybuf[sl].at[pl.ds(k * tm + t, 1)], sems.at[2 + sl]).start()
            return 0

        lax.fori_loop(0, tm, issue, 0, unroll=2)

    @pl.when(i == 0)
    def _():
        first = idx_copy(0, 0)
        first.start()
        first.wait()
        issue_rows(0)

        @pl.when(nsteps > 1)
        def _():
            idx_copy(1, 1).start()

    for sl in range(2):
        @pl.when(jnp.logical_and(i + 1 < nsteps, nslot == sl))
        def _(sl=sl):
            idx_copy(i + 1, sl).wait()
            issue_rows(sl)

    for sl in range(2):
        @pl.when(slot == sl)
        def _(sl=sl):
            @pl.when(i + 2 < nsteps)
            def _():
                idx_copy(i + 2, sl).start()

            pltpu.make_async_copy(y_hbm.at[pl.ds(0, n)], ybuf[sl], sems.at[2 + sl]).wait()
            gate = gate_ref[...]
            acc = gate[:, 0:1] * ybuf[sl][0:tm]
            for k in range(1, TOP_K):
                acc = acc + gate[:, k:k + 1] * ybuf[sl][k * tm:(k + 1) * tm]
            out = x_ref[...] + acc
            if final_norm:
                out = _rms(out) * fg_ref[...]
            o_ref[...] = out


def _combine(dest_flat, y, x, gates, final_g, final_norm):
    t, d = x.shape
    tm = COMBINE_TM
    return pl.pallas_call(
        functools.partial(_combine_kernel, final_norm=final_norm),
        out_shape=jax.ShapeDtypeStruct((t, d), F32),
        grid=(t // tm,),
        in_specs=[
            pl.BlockSpec(memory_space=pl.ANY),
            pl.BlockSpec(memory_space=pl.ANY),
            pl.BlockSpec((tm, d), lambda i: (i, 0)),
            pl.BlockSpec((tm, LANES), lambda i: (i, 0)),
            pl.BlockSpec((1, d), lambda i: (0, 0)),
        ],
        out_specs=pl.BlockSpec((tm, d), lambda i: (i, 0)),
        scratch_shapes=[pltpu.SMEM((tm * TOP_K,), jnp.int32), pltpu.SMEM((tm * TOP_K,), jnp.int32),
                        pltpu.VMEM((TOP_K * tm, d), F32), pltpu.VMEM((TOP_K * tm, d), F32),
                        pltpu.SemaphoreType.DMA((4,))],
        compiler_params=_params(("arbitrary",)),
        name="combine",
    )(dest_flat, y, x, gates, final_g.reshape(1, d))


def _in_proj_weights(w_in):
    d = w_in.shape[0]
    o_gate = 3 * D_FOX
    o_sb = o_gate + N_FOX
    o_cq = o_sb + 3 * D_SB
    o_ckv = o_cq + Q_LORA
    o_kr = o_ckv + KV_LORA
    o_sgu = o_kr + MLA_ROPE
    half = MLA_ROPE // 2
    w_qkv = jnp.concatenate([w_in[:, :o_gate], w_in[:, o_sb:o_cq]], axis=1).astype(BF16)
    kr = w_in[:, o_kr:o_sgu]
    zpad = jnp.zeros((d, LANES - MLA_ROPE), w_in.dtype)
    w_misc = jnp.concatenate([
        w_in[:, o_sgu:o_sgu + 2 * D_SGU],
        w_in[:, o_cq:o_ckv],
        w_in[:, o_ckv:o_kr],
        kr, zpad,
        kr[:, half:], kr[:, :half], zpad,
        w_in[:, o_gate:o_sb], jnp.zeros((d, LANES - N_FOX), w_in.dtype),
    ], axis=1).astype(BF16)
    assert w_misc.shape[1] == MISC_W
    return w_qkv, w_misc


def _mla_weights(w_q_b, w_kv_b):
    half = MLA_ROPE // 2
    wq = w_q_b.reshape(Q_LORA, N_MLA, MLA_NOPE + MLA_ROPE)
    x1 = wq[:, :, MLA_NOPE:MLA_NOPE + half]
    x2 = wq[:, :, MLA_NOPE + half:]
    z = jnp.zeros((Q_LORA, N_MLA, LANES - MLA_ROPE), w_q_b.dtype)
    wq_r = jnp.concatenate([wq[:, :, :MLA_NOPE], x1, x2, z, x2, x1, z], axis=2)
    wq_r = wq_r.reshape(Q_LORA, N_MLA * 3 * LANES).astype(BF16)
    wkv = w_kv_b.reshape(KV_LORA, N_MLA, MLA_NOPE + MLA_V)
    wkv_r = jnp.concatenate([wkv[:, :, :MLA_NOPE].reshape(KV_LORA, -1),
                             wkv[:, :, MLA_NOPE:].reshape(KV_LORA, -1)], axis=1).astype(BF16)
    return wq_r, wkv_r


def _routing_tables(idx, rank, cnt, n_tokens):
    nsb_max = N_EXPERTS + (n_tokens * TOP_K) // EXP_ROWS
    counts = cnt[0, :N_EXPERTS]
    nsb_e = (counts + EXP_ROWS - 1) // EXP_ROWS
    sb_end = jnp.cumsum(nsb_e)
    sb_start = sb_end - nsb_e
    n_used = sb_end[-1]
    top_idx = idx[:, :TOP_K]
    dest = (sb_start * EXP_ROWS)[top_idx] + rank[:, :TOP_K]
    s_ids = jnp.arange(nsb_max, dtype=jnp.int32)
    sb_e = jnp.minimum(jnp.searchsorted(sb_end, s_ids, side='right'), N_EXPERTS - 1).astype(jnp.int32)
    last_e = sb_e[jnp.maximum(n_used - 1, 0)]
    sb_e = jnp.where(s_ids < n_used, sb_e, last_e)
    nval = jnp.clip(counts[sb_e] - (s_ids - sb_start[sb_e]) * EXP_ROWS, 0, EXP_ROWS)
    nval = jnp.where(s_ids < n_used, nval, 0).astype(jnp.int32)
    return dest.reshape(-1).astype(jnp.int32), sb_e, nval, n_used.reshape(1).astype(jnp.int32), nsb_max


def _layer(x, pos, l, p, final_g, final_norm, batch, seq, moe_bufs):
    t = x.shape[0]
    w_qkv, w_misc = _in_proj_weights(p["w_in"][l])
    qkv = _norm_matmul(x, p["norm_mix_g"][l], w_qkv, BF16, 512, 1536, "in_proj_qkv")
    misc = _norm_matmul(x, p["norm_mix_g"][l], w_misc, F32, 512, MISC_W, "in_proj_misc")

    crow = _fox_prep(misc, p["b_forget"][l], batch, seq)
    nkb = seq // ATT_TILE
    crow = crow.reshape(batch, nkb, N_FOX // FOX_HEADS, FOX_HEADS, ATT_TILE).transpose(0, 2, 1, 3, 4)
    o_fox = _fox_attention(qkv, crow, batch, seq)
    o_sb = _sb_attention(qkv, batch, seq)
    wq_r, wkv_r = _mla_weights(p["mla_w_q_b"][l], p["mla_w_kv_b"][l])
    q_m, k_m, v_m = _mla_prep(misc, pos, p["mla_q_norm_g"][l], p["mla_kv_norm_g"][l], wq_r, wkv_r)
    o_mla = _mla_attention(q_m, k_m, v_m, batch, seq)
    o_sgu = _sgu(misc, p["sgu_ln_g"][l], p["sgu_ln_b"][l], p["sgu_w_s"][l], p["sgu_b_s"][l])

    x_new, h2p, idx, gates, rank, cnt = _outproj_router(
        o_fox, o_sb, o_mla, o_sgu, x, p["out_norm_g"][l], p["w_o"][l].astype(BF16),
        p["norm_ffn_g"][l], p["router_w"][l], p["router_b"][l])

    dest, sb_e, nval, n_used, nsb_max = _routing_tables(idx, rank, cnt, t)
    xs_prev, y_prev = moe_bufs if moe_bufs is not None else (None, None)
    xs = _scatter_rows(dest, h2p, nsb_max * EXP_ROWS, xs_prev)
    y = _experts(sb_e, nval, n_used, xs, l, p["w_gate"], p["b_gate"], p["w_up"], p["b_up"],
                 p["w_down"], p["b_down"], nsb_max, y_prev)
    return _combine(dest, y, x_new, gates, final_g, final_norm), (xs, y)


def kernel(x, positions, norm_mix_g, w_in, b_forget, mla_q_norm_g, mla_kv_norm_g, mla_w_q_b, mla_w_kv_b,
           sgu_ln_g, sgu_ln_b, sgu_w_s, sgu_b_s, out_norm_g, w_o, norm_ffn_g, router_w, router_b, w_gate,
           b_gate, w_up, b_up, w_down, b_down, final_norm_g):
    batch, seq, d = x.shape
    depth = w_in.shape[0]
    p = dict(norm_mix_g=norm_mix_g, w_in=w_in, b_forget=b_forget, mla_q_norm_g=mla_q_norm_g,
             mla_kv_norm_g=mla_kv_norm_g, mla_w_q_b=mla_w_q_b, mla_w_kv_b=mla_w_kv_b, sgu_ln_g=sgu_ln_g,
             sgu_ln_b=sgu_ln_b, sgu_w_s=sgu_w_s, sgu_b_s=sgu_b_s, out_norm_g=out_norm_g, w_o=w_o,
             norm_ffn_g=norm_ffn_g, router_w=router_w, router_b=router_b, w_gate=w_gate, b_gate=b_gate,
             w_up=w_up, b_up=b_up, w_down=w_down, b_down=b_down)
    h = x.reshape(batch * seq, d)
    pos = positions.reshape(batch * seq, 1)
    moe_bufs = None
    for l in range(depth):
        h, moe_bufs = _layer(h, pos, l, p, final_norm_g, l == depth - 1, batch, seq, moe_bufs)
    return h.reshape(batch, seq, d)
```

```python
import functools

import numpy as np
import jax
import jax.numpy as jnp
from jax import lax
from jax.experimental import pallas as pl
from jax.experimental.pallas import tpu as pltpu

HEAD_DIM = 64
N_FOX = 8
N_SB = 8
N_MLA = 4
MLA_NOPE = 128
MLA_ROPE = 64
MLA_V = 128
Q_LORA = 512
KV_LORA = 256
N_SGU = 8
SGU_CH = 64
CHUNK = 128
ROPE_THETA = 10000.0
D_FOX = N_FOX * HEAD_DIM
D_SB = N_SB * HEAD_DIM
D_MLA = N_MLA * MLA_V
D_SGU = N_SGU * SGU_CH
N_EXPERTS = 32
TOP_K = 4
SWIGLU_LIMIT = 7.0
SWIGLU_ALPHA = 1.702
RMS_EPS = 1e-6
LN_EPS = 1e-5

LANES = 128
VMEM_LIMIT = 56 * 1024 * 1024

ATT_TILE = 256
FOX_HEADS = 8
MLA_HEADS = 4
SB_HEADS = 8
EXP_ROWS = 1536
EXP_SUB = 256
EXP_FF = 256
EXP_DN = 512
SCATTER_TM = 256
COMBINE_TM = 256

F32 = jnp.float32
BF16 = jnp.bfloat16

MISC_U = 0
MISC_V = 512
MISC_CQ = 1024
MISC_CKV = 1536
MISC_KR1 = 1792
MISC_KR2 = 1920
MISC_GATE = 2048
MISC_W = 2176


def _params(sem, vmem=VMEM_LIMIT):
    return pltpu.CompilerParams(dimension_semantics=sem, vmem_limit_bytes=vmem)


def _split3(x):
    hi = x.astype(BF16)
    r1 = x - hi.astype(F32)
    mid = r1.astype(BF16)
    lo = (r1 - mid.astype(F32)).astype(BF16)
    return hi, mid, lo


def _log_sigmoid_pair(z):
    t = jnp.log1p(jnp.exp(-jnp.abs(z)))
    return jnp.minimum(z, 0.0) - t, jnp.minimum(-z, 0.0) - t


def _rms(x):
    return x * lax.rsqrt(jnp.mean(x * x, axis=-1, keepdims=True) + RMS_EPS)


def _norm_matmul_kernel(x_ref, g_ref, w_ref, o_ref, xn_ref):
    @pl.when(pl.program_id(1) == 0)
    def _():
        xn_ref[...] = (_rms(x_ref[...]) * g_ref[...]).astype(BF16)

    o_ref[...] = jnp.dot(xn_ref[...], w_ref[...], preferred_element_type=F32).astype(o_ref.dtype)


def _norm_matmul(x, g, w, out_dtype, tm, tn, name):
    t, d = x.shape
    n = w.shape[1]
    return pl.pallas_call(
        _norm_matmul_kernel,
        out_shape=jax.ShapeDtypeStruct((t, n), out_dtype),
        grid=(t // tm, n // tn),
        in_specs=[
            pl.BlockSpec((tm, d), lambda i, j: (i, 0)),
            pl.BlockSpec((1, d), lambda i, j: (0, 0)),
            pl.BlockSpec((d, tn), lambda i, j: (0, j)),
        ],
        out_specs=pl.BlockSpec((tm, tn), lambda i, j: (i, j)),
        scratch_shapes=[pltpu.VMEM((tm, d), BF16)],
        compiler_params=_params(("parallel", "arbitrary")),
        name=name,
    )(x, g.reshape(1, d), w)


def _fox_prep_kernel(gate_ref, bf_ref, crow_ref):
    s_len = gate_ref.shape[0]
    bl = ATT_TILE
    r = lax.broadcasted_iota(jnp.int32, (bl, bl), 0)
    c = lax.broadcasted_iota(jnp.int32, (bl, bl), 1)
    tri = (c <= r).astype(BF16)
    carry = jnp.zeros((1, LANES), F32)
    for i in range(s_len // bl):
        z = gate_ref[i * bl:(i + 1) * bl, :] + bf_ref[...]
        lf, _ = _log_sigmoid_pair(z)
        hi, mid, lo = _split3(lf)
        cs = (jnp.dot(tri, hi, preferred_element_type=F32)
              + jnp.dot(tri, mid, preferred_element_type=F32)
              + jnp.dot(tri, lo, preferred_element_type=F32)) + carry
        crow_ref[i] = cs.T[0:N_FOX, :]
        carry = cs[bl - 1:bl, :]


def _fox_prep(misc, b_forget, batch, seq):
    nkb = seq // ATT_TILE
    bf = jnp.zeros((1, LANES), F32).at[0, :N_FOX].set(b_forget)
    return pl.pallas_call(
        _fox_prep_kernel,
        out_shape=jax.ShapeDtypeStruct((batch, nkb, N_FOX, ATT_TILE), F32),
        grid=(batch,),
        in_specs=[
            pl.BlockSpec((seq, LANES), lambda b: (b, MISC_GATE // LANES)),
            pl.BlockSpec((1, LANES), lambda b: (0, 0)),
        ],
        out_specs=pl.BlockSpec((None, nkb, N_FOX, ATT_TILE), lambda b: (b, 0, 0, 0)),
        compiler_params=_params(("parallel",)),
        name="fox_prep",
    )(misc, bf)


def _causal_mask(tq):
    row = lax.broadcasted_iota(jnp.int32, (tq, tq), 0)
    col = lax.broadcasted_iota(jnp.int32, (tq, tq), 1)
    return col <= row, col < row


def _softmax_sweep(qs, k_of, v_of, bias_of, scale, qi, tq, ones_lane=None):
    nh = len(qs)
    incl, _ = _causal_mask(tq)

    def step(kb, carry, diag):
        ks = pl.multiple_of(kb * tq, tq)
        ss = [lax.dot_general(qs[h], k_of(h, ks), (((1,), (1,)), ((), ())), preferred_element_type=F32)
              for h in range(nh)]
        ps, stats = [], []
        for h in range(nh):
            m, l, _ = carry[h]
            s = ss[h]
            if scale is not None:
                s = s * scale
            b = bias_of(h, kb)
            if b is not None:
                s = s + b
            if diag:
                s = jnp.where(incl, s, -jnp.inf)
            m_new = jnp.maximum(m, jnp.max(s, axis=-1, keepdims=True))
            alpha = jnp.exp(m - m_new)
            p = jnp.exp(s - m_new)
            if ones_lane is None:
                l = alpha * l + jnp.sum(p, axis=-1, keepdims=True)
            stats.append((m_new, l, alpha))
            ps.append(p.astype(BF16))
        out = []
        for h in range(nh):
            m_new, l, alpha = stats[h]
            acc = alpha * carry[h][2] + jnp.dot(ps[h], v_of(h, ks), preferred_element_type=F32)
            out.append((m_new, l, acc))
        return tuple(out)

    dv = v_of(0, 0).shape[-1]
    l0 = jnp.zeros((tq, 1), F32) if ones_lane is None else None
    init = tuple((jnp.full((tq, 1), -jnp.inf, F32), l0, jnp.zeros((tq, dv), F32)) for _ in range(nh))
    carry = lax.fori_loop(0, qi, lambda kb, c: step(kb, c, False), init)
    carry = step(qi, carry, True)
    if ones_lane is None:
        return [acc / l for (_, l, acc) in carry]
    return [acc / acc[:, ones_lane(h):ones_lane(h) + 1] for h, (_, _, acc) in enumerate(carry)]


def _fox_attn_kernel(q_ref, k_ref, v_ref, crow_ref, o_ref):
    tq = q_ref.shape[0]
    qi = pl.program_id(2)
    q = q_ref[...] * jnp.asarray(HEAD_DIM ** -0.5, BF16)
    qs = [q[:, j * HEAD_DIM:(j + 1) * HEAD_DIM] for j in range(FOX_HEADS)]

    def k_of(h, ks):
        return k_ref[pl.ds(ks, tq), h * HEAD_DIM:(h + 1) * HEAD_DIM]

    first = lax.broadcasted_iota(jnp.int32, (tq, LANES), 1) < HEAD_DIM
    one = jnp.ones((tq, LANES), BF16)

    def v_of(h, ks):
        pair = v_ref[pl.ds(ks, tq), (h // 2) * LANES:(h // 2 + 1) * LANES]
        return jnp.where(first, pair, one) if h % 2 == 0 else jnp.where(first, one, pair)

    def bias_of(h, kb):
        return -crow_ref[kb][h:h + 1, :]

    outs = _softmax_sweep(qs, k_of, v_of, bias_of, None, qi, tq,
                          ones_lane=lambda h: HEAD_DIM if h % 2 == 0 else 0)
    o_ref[...] = jnp.concatenate([jnp.where(first, outs[h], outs[h + 1]) for h in range(0, FOX_HEADS, 2)],
                                 axis=-1)


def _fox_attention(qkv, crow, batch, seq):
    t = qkv.shape[0]
    tq = ATT_TILE
    nq = seq // tq
    npair = N_FOX // FOX_HEADS
    w = FOX_HEADS * HEAD_DIM
    return pl.pallas_call(
        _fox_attn_kernel,
        out_shape=jax.ShapeDtypeStruct((t, D_FOX), F32),
        grid=(batch, npair, nq),
        in_specs=[
            pl.BlockSpec((tq, w), lambda b, h, i: (b * nq + i, h)),
            pl.BlockSpec((seq, w), lambda b, h, i: (b, npair + h)),
            pl.BlockSpec((seq, w), lambda b, h, i: (b, 2 * npair + h)),
            pl.BlockSpec((None, None, nq, FOX_HEADS, tq), lambda b, h, i: (b, h, 0, 0, 0)),
        ],
        out_specs=pl.BlockSpec((tq, w), lambda b, h, i: (b * nq + i, h)),
        compiler_params=_params(("parallel", "parallel", "arbitrary")),
        name="fox_attn",
    )(qkv, qkv, qkv, crow)


def _mla_attn_kernel(q_ref, k_ref, v_ref, o_ref):
    tq = q_ref.shape[0]
    qi = pl.program_id(2)
    dk = 2 * LANES
    scale = (MLA_NOPE + MLA_ROPE) ** -0.5
    qs = [q_ref[:, h * dk:(h + 1) * dk] for h in range(MLA_HEADS)]
    outs = _softmax_sweep(qs,
                          lambda h, ks: k_ref[pl.ds(ks, tq), h * dk:(h + 1) * dk],
                          lambda h, ks: v_ref[pl.ds(ks, tq), h * MLA_V:(h + 1) * MLA_V],
                          lambda h, kb: None, scale, qi, tq)
    o_ref[...] = jnp.concatenate(outs, axis=-1)


def _mla_attention(q, k, v, batch, seq):
    t = q.shape[0]
    tq = ATT_TILE
    nq = seq // tq
    dk = MLA_HEADS * 2 * LANES
    dv = MLA_HEADS * MLA_V
    return pl.pallas_call(
        _mla_attn_kernel,
        out_shape=jax.ShapeDtypeStruct((t, D_MLA), F32),
        grid=(batch, N_MLA // MLA_HEADS, nq),
        in_specs=[
            pl.BlockSpec((tq, dk), lambda b, h, i: (b * nq + i, h)),
            pl.BlockSpec((seq, dk), lambda b, h, i: (b, h)),
            pl.BlockSpec((seq, dv), lambda b, h, i: (b, h)),
        ],
        out_specs=pl.BlockSpec((tq, dv), lambda b, h, i: (b * nq + i, h)),
        compiler_params=_params(("parallel", "parallel", "arbitrary")),
        name="mla_attn",
    )(q, k, v)


def _sb_attn_kernel(q_ref, k_ref, v_ref, o_ref):
    tq = q_ref.shape[0]
    qi = pl.program_id(2)
    q = q_ref[...] * jnp.asarray(HEAD_DIM ** -0.5, BF16)
    nh = SB_HEADS
    qs = [q[:, j * HEAD_DIM:(j + 1) * HEAD_DIM] for j in range(nh)]
    _, strict = _causal_mask(tq)
    row = lax.broadcasted_iota(jnp.int32, (tq, tq), 0)
    col = lax.broadcasted_iota(jnp.int32, (tq, tq), 1)
    later = (row > col).astype(BF16)

    def step(kb, carry, diag):
        ks = pl.multiple_of(kb * tq, tq)
        zs = [lax.dot_general(qs[h], k_ref[pl.ds(ks, tq), h * HEAD_DIM:(h + 1) * HEAD_DIM],
                              (((1,), (1,)), ((), ())), preferred_element_type=F32) for h in range(nh)]
        lss, lrs = [], []
        for h in range(nh):
            z = zs[h]
            ls = jnp.minimum(z, 0.0) - jnp.log(1.0 + jnp.exp(-jnp.abs(z)))
            lr = ls - z
            if diag:
                lr = jnp.where(strict, lr, 0.0)
            lss.append(ls)
            lrs.append(lr)
        afters = []
        for h in range(nh):
            hi = lrs[h].astype(BF16)
            lo = (lrs[h] - hi.astype(F32)).astype(BF16)
            afters.append((jnp.dot(hi, later, preferred_element_type=F32)
                           + jnp.dot(lo, later, preferred_element_type=F32)) + carry[h][0])
        out = []
        for h in range(nh):
            a = jnp.exp(lss[h] + afters[h])
            if diag:
                a = jnp.where(strict, a, 0.0)
            v = v_ref[pl.ds(ks, tq), h * HEAD_DIM:(h + 1) * HEAD_DIM]
            acc = carry[h][1] + jnp.dot(a.astype(BF16), v, preferred_element_type=F32)
            out.append((afters[h][:, 0:1] + lrs[h][:, 0:1], acc))
        return tuple(out)

    init = tuple((jnp.zeros((tq, 1), F32), jnp.zeros((tq, HEAD_DIM), F32)) for _ in range(nh))
    carry = step(qi, init, True)
    carry = lax.fori_loop(0, qi, lambda i, c: step(qi - 1 - i, c, False), carry)
    o_ref[...] = jnp.concatenate([acc for (_, acc) in carry], axis=-1)


def _sb_attention(qkv, batch, seq):
    t = qkv.shape[0]
    tq = ATT_TILE
    nq = seq // tq
    npair = N_SB // SB_HEADS
    w = SB_HEADS * HEAD_DIM
    base = 3 * D_FOX // w
    return pl.pallas_call(
        _sb_attn_kernel,
        out_shape=jax.ShapeDtypeStruct((t, D_SB), F32),
        grid=(batch, npair, nq),
        in_specs=[
            pl.BlockSpec((tq, w), lambda b, h, i: (b * nq + i, base + h)),
            pl.BlockSpec((seq, w), lambda b, h, i: (b, base + npair + h)),
            pl.BlockSpec((seq, w), lambda b, h, i: (b, base + 2 * npair + h)),
        ],
        out_specs=pl.BlockSpec((tq, w), lambda b, h, i: (b * nq + i, h)),
        compiler_params=_params(("parallel", "parallel", "arbitrary")),
        name="sb_attn",
    )(qkv, qkv, qkv)


def _mla_prep_kernel(cq_ref, ckv_ref, kr1_ref, kr2_ref, pos_ref, gq_ref, gkv_ref, wq_ref, wkv_ref,
                     invf_ref, sgn_ref, q_out, k_out, v_out):
    ang = pos_ref[...].astype(F32) * invf_ref[...]
    cosv = jnp.cos(ang)
    sinv = jnp.sin(ang) * sgn_ref[...]
    qn = (_rms(cq_ref[...]) * gq_ref[...]).astype(BF16)
    qa = jnp.dot(qn, wq_ref[...], preferred_element_type=F32)
    for h in range(N_MLA):
        o = h * 3 * LANES
        pe = qa[:, o + LANES:o + 2 * LANES] * cosv + qa[:, o + 2 * LANES:o + 3 * LANES] * sinv
        q_out[:, h * 2 * LANES:h * 2 * LANES + LANES] = qa[:, o:o + LANES].astype(BF16)
        q_out[:, h * 2 * LANES + LANES:(h + 1) * 2 * LANES] = pe.astype(BF16)
    kvn = (_rms(ckv_ref[...]) * gkv_ref[...]).astype(BF16)
    kva = jnp.dot(kvn, wkv_ref[...], preferred_element_type=F32)
    kpe = (kr1_ref[...] * cosv + kr2_ref[...] * sinv).astype(BF16)
    for h in range(N_MLA):
        k_out[:, h * 2 * LANES:h * 2 * LANES + LANES] = kva[:, h * LANES:(h + 1) * LANES].astype(BF16)
        k_out[:, h * 2 * LANES + LANES:(h + 1) * 2 * LANES] = kpe
    v_out[...] = kva[:, N_MLA * MLA_NOPE:].astype(BF16)


def _mla_prep(misc, pos, gq, gkv, wq, wkv, tm=512):
    t = misc.shape[0]
    half = MLA_ROPE // 2
    inv_freq = ROPE_THETA ** (-jnp.arange(half, dtype=F32) / half)
    invf = jnp.tile(inv_freq, LANES // half).reshape(1, LANES)
    sgn = jnp.tile(jnp.concatenate([-jnp.ones((half,), F32), jnp.ones((half,), F32)]),
                   LANES // MLA_ROPE).reshape(1, LANES)
    const = lambda i: (0, 0)
    return pl.pallas_call(
        _mla_prep_kernel,
        out_shape=(jax.ShapeDtypeStruct((t, N_MLA * 2 * LANES), BF16),
                   jax.ShapeDtypeStruct((t, N_MLA * 2 * LANES), BF16),
                   jax.ShapeDtypeStruct((t, D_MLA), BF16)),
        grid=(t // tm,),
        in_specs=[
            pl.BlockSpec((tm, Q_LORA), lambda i: (i, MISC_CQ // Q_LORA)),
            pl.BlockSpec((tm, KV_LORA), lambda i: (i, MISC_CKV // KV_LORA)),
            pl.BlockSpec((tm, LANES), lambda i: (i, MISC_KR1 // LANES)),
            pl.BlockSpec((tm, LANES), lambda i: (i, MISC_KR2 // LANES)),
            pl.BlockSpec((tm, 1), lambda i: (i, 0)),
            pl.BlockSpec((1, Q_LORA), const),
            pl.BlockSpec((1, KV_LORA), const),
            pl.BlockSpec(wq.shape, const),
            pl.BlockSpec(wkv.shape, const),
            pl.BlockSpec((1, LANES), const),
            pl.BlockSpec((1, LANES), const),
        ],
        out_specs=(pl.BlockSpec((tm, N_MLA * 2 * LANES), lambda i: (i, 0)),
                   pl.BlockSpec((tm, N_MLA * 2 * LANES), lambda i: (i, 0)),
                   pl.BlockSpec((tm, D_MLA), lambda i: (i, 0))),
        compiler_params=_params(("parallel",)),
        name="mla_prep",
    )(misc, misc, misc, misc, pos, gq.reshape(1, -1), gkv.reshape(1, -1), wq, wkv, invf, sgn)


def _gelu(x):
    return 0.5 * x * (1.0 + lax.erf(x * np.float32(np.sqrt(0.5))))


def _sgu_kernel(u_ref, v_ref, lng_ref, lnb_ref, w_ref, bias_ref, o_ref):
    tm = u_ref.shape[0]
    v = _gelu(v_ref[...])
    mu = jnp.mean(v, axis=-1, keepdims=True)
    xc = v - mu
    var = jnp.mean(xc * xc, axis=-1, keepdims=True)
    vb = (xc * lax.rsqrt(var + LN_EPS) * lng_ref[...] + lnb_ref[...]).astype(BF16)
    row = lax.broadcasted_iota(jnp.int32, (CHUNK, CHUNK), 0)
    col = lax.broadcasted_iota(jnp.int32, (CHUNK, LANES), 1)
    tril = lax.broadcasted_iota(jnp.int32, (CHUNK, CHUNK), 1) <= row
    first = col < SGU_CH
    ws = [jnp.where(tril, w_ref[g], 0.0).astype(BF16) for g in range(N_SGU)]
    zero = jnp.zeros((CHUNK, LANES), BF16)
    for c in range(tm // CHUNK):
        rows = slice(c * CHUNK, (c + 1) * CHUNK)
        for p in range(N_SGU // 2):
            cols = slice(p * LANES, (p + 1) * LANES)
            vp = vb[rows, cols]
            mixed = (jnp.dot(ws[2 * p], jnp.where(first, vp, zero), preferred_element_type=F32)
                     + jnp.dot(ws[2 * p + 1], jnp.where(first, zero, vp), preferred_element_type=F32))
            o_ref[rows, cols] = _gelu(u_ref[rows, cols]) * (mixed + bias_ref[:, cols])


def _sgu(misc, ln_g, ln_b, w_s, b_s, tm=512):
    t = misc.shape[0]
    bias = jnp.repeat(b_s.T, SGU_CH, axis=1)
    const2 = lambda i: (0, 0)
    return pl.pallas_call(
        _sgu_kernel,
        out_shape=jax.ShapeDtypeStruct((t, D_SGU), F32),
        grid=(t // tm,),
        in_specs=[
            pl.BlockSpec((tm, D_SGU), lambda i: (i, MISC_U // D_SGU)),
            pl.BlockSpec((tm, D_SGU), lambda i: (i, MISC_V // D_SGU)),
            pl.BlockSpec((1, D_SGU), const2),
            pl.BlockSpec((1, D_SGU), const2),
            pl.BlockSpec((N_SGU, CHUNK, CHUNK), lambda i: (0, 0, 0)),
            pl.BlockSpec((CHUNK, D_SGU), const2),
        ],
        out_specs=pl.BlockSpec((tm, D_SGU), lambda i: (i, 0)),
        compiler_params=_params(("parallel",)),
        name="sgu",
    )(misc, misc, ln_g.reshape(1, -1), ln_b.reshape(1, -1), w_s, bias)


def _pack_bf16_pairs(hb):
    n = hb.shape[1] // 2
    bits = pltpu.bitcast(hb.astype(F32), jnp.uint32)
    return (bits[:, n:] & jnp.uint32(0xFFFF0000)) | (bits[:, :n] >> 16)


def _unpack_bf16_pairs(xu):
    lo = pltpu.bitcast(xu << 16, F32).astype(BF16)
    hi = pltpu.bitcast(xu & jnp.uint32(0xFFFF0000), F32).astype(BF16)
    return jnp.concatenate([lo, hi], axis=1)


def _outproj_router_kernel(of_ref, os_ref, om_ref, og_ref, x_ref, ong_ref, wo_ref, nfg_ref, rw_ref,
                           rb_ref, xo_ref, h2_ref, idx_ref, gate_ref, rank_ref, cnt_ref, carry_ref):
    tm = x_ref.shape[0]

    @pl.when(pl.program_id(0) == 0)
    def _():
        carry_ref[...] = jnp.zeros_like(carry_ref)

    o = jnp.concatenate([_rms(r[...]) for r in (of_ref, os_ref, om_ref, og_ref)], axis=-1)
    o = (o * ong_ref[...]).astype(BF16)
    xn = x_ref[...] + jnp.dot(o, wo_ref[...], preferred_element_type=F32)
    xo_ref[...] = xn
    hb = (_rms(xn) * nfg_ref[...]).astype(BF16)
    h2_ref[...] = _pack_bf16_pairs(hb)

    lane = lax.broadcasted_iota(jnp.int32, (tm, LANES), 1)
    logits = jnp.dot(hb, rw_ref[...], preferred_element_type=F32) + rb_ref[...]
    vals = jnp.where(lane < N_EXPERTS, logits, -jnp.inf)
    sels, tops = [], []
    for _ in range(TOP_K):
        m = jnp.max(vals, axis=-1, keepdims=True)
        idx = jnp.min(jnp.where(vals == m, lane, LANES), axis=-1, keepdims=True)
        sel = lane == idx
        vals = jnp.where(sel, -jnp.inf, vals)
        sels.append(sel)
        tops.append((m, idx))
    es = [jnp.exp(m - tops[0][0]) for (m, _) in tops]
    den = es[0] + es[1] + es[2] + es[3]

    multi = sels[0] | sels[1] | sels[2] | sels[3]
    mh = jnp.where(multi, 1.0, 0.0)
    r = lax.broadcasted_iota(jnp.int32, (tm, tm), 0)
    c = lax.broadcasted_iota(jnp.int32, (tm, tm), 1)
    before = (c < r).astype(BF16)
    cnt = jnp.dot(before, mh.astype(BF16), preferred_element_type=F32) + carry_ref[...]
    total = cnt[tm - 1:tm, :] + mh[tm - 1:tm, :]
    carry_ref[...] = total
    cnt_ref[...] = jnp.broadcast_to(total, cnt_ref.shape).astype(jnp.int32)

    idx_o = jnp.zeros((tm, LANES), jnp.int32)
    gate_o = jnp.zeros((tm, LANES), F32)
    rank_o = jnp.zeros((tm, LANES), jnp.int32)
    for k in range(TOP_K):
        rank_k = jnp.sum(jnp.where(sels[k], cnt, 0.0), axis=-1, keepdims=True).astype(jnp.int32)
        idx_o = jnp.where(lane == k, tops[k][1], idx_o)
        gate_o = jnp.where(lane == k, es[k] / den, gate_o)
        rank_o = jnp.where(lane == k, rank_k, rank_o)
    idx_ref[...] = idx_o
    gate_ref[...] = gate_o
    rank_ref[...] = rank_o


def _outproj_router(o_fox, o_sb, o_mla, o_sgu, x, out_norm_g, w_o, norm_ffn_g, router_w, router_b, tm=512):
    t, d = x.shape
    dg = o_fox.shape[1]
    rw = jnp.zeros((d, LANES), BF16).at[:, :N_EXPERTS].set(router_w.astype(BF16))
    rb = jnp.zeros((1, LANES), F32).at[0, :N_EXPERTS].set(router_b)
    const = lambda i: (0, 0)
    grp = pl.BlockSpec((tm, dg), lambda i: (i, 0))
    lanes_out = pl.BlockSpec((tm, LANES), lambda i: (i, 0))
    return pl.pallas_call(
        _outproj_router_kernel,
        out_shape=(jax.ShapeDtypeStruct((t, d), F32),
                   jax.ShapeDtypeStruct((t, d // 2), jnp.uint32),
                   jax.ShapeDtypeStruct((t, LANES), jnp.int32),
                   jax.ShapeDtypeStruct((t, LANES), F32),
                   jax.ShapeDtypeStruct((t, LANES), jnp.int32),
                   jax.ShapeDtypeStruct((8, LANES), jnp.int32)),
        grid=(t // tm,),
        in_specs=[grp, grp, grp, grp,
                  pl.BlockSpec((tm, d), lambda i: (i, 0)),
                  pl.BlockSpec((1, 4 * dg), const),
                  pl.BlockSpec((4 * dg, d), const),
                  pl.BlockSpec((1, d), const),
                  pl.BlockSpec((d, LANES), const),
                  pl.BlockSpec((1, LANES), const)],
        out_specs=(pl.BlockSpec((tm, d), lambda i: (i, 0)),
                   pl.BlockSpec((tm, d // 2), lambda i: (i, 0)),
                   lanes_out, lanes_out, lanes_out,
                   pl.BlockSpec((8, LANES), const)),
        scratch_shapes=[pltpu.VMEM((1, LANES), F32)],
        compiler_params=_params(("arbitrary",)),
        name="outproj_router",
    )(o_fox, o_sb, o_mla, o_sgu, x, out_norm_g.reshape(1, -1), w_o, norm_ffn_g.reshape(1, -1), rw, rb)


def _scatter_rows_kernel(dest_hbm, x_ref, xs_init_hbm, xs_hbm, dest0, dest1, sems):
    del xs_init_hbm
    dest_smem = (dest0, dest1)
    i = pl.program_id(0)
    nsteps = pl.num_programs(0)
    tm = x_ref.shape[0]
    n = tm * TOP_K

    def idx_copy(step, sl):
        return pltpu.make_async_copy(dest_hbm.at[pl.ds(step * n, n)], dest_smem[sl], sems.at[sl])

    @pl.when(i == 0)
    def _():
        idx_copy(0, 0).start()

    for sl in range(2):
        @pl.when(i % 2 == sl)
        def _(sl=sl):
            @pl.when(i + 1 < nsteps)
            def _():
                idx_copy(i + 1, 1 - sl).start()

            idx_copy(i, sl).wait()

            def issue(t, _):
                for k in range(TOP_K):
                    pltpu.make_async_copy(x_ref.at[pl.ds(t, 1)],
                                          xs_hbm.at[pl.ds(dest_smem[sl][t * TOP_K + k], 1)], sems.at[2]).start()
                return 0

            lax.fori_loop(0, tm, issue, 0, unroll=2)

    for _ in range(TOP_K):
        pltpu.make_async_copy(x_ref, xs_hbm.at[pl.ds(0, tm)], sems.at[2]).wait()


def _scatter_rows(dest_flat, src, nrows, xs_init=None):
    t, w = src.shape
    tm = SCATTER_TM
    if xs_init is None:
        xs_init = jnp.zeros((nrows, w), src.dtype)
    return pl.pallas_call(
        _scatter_rows_kernel,
        out_shape=jax.ShapeDtypeStruct((nrows, w), src.dtype),
        grid=(t // tm,),
        in_specs=[pl.BlockSpec(memory_space=pl.ANY),
                  pl.BlockSpec((tm, w), lambda i: (i, 0)),
                  pl.BlockSpec(memory_space=pl.ANY)],
        out_specs=pl.BlockSpec(memory_space=pl.ANY),
        scratch_shapes=[pltpu.SMEM((tm * TOP_K,), jnp.int32), pltpu.SMEM((tm * TOP_K,), jnp.int32),
                        pltpu.SemaphoreType.DMA((3,))],
        input_output_aliases={2: 0},
        compiler_params=_params(("arbitrary",)),
        name="scatter_rows",
    )(dest_flat, src, xs_init)


def _expert_kernel(sbe_ref, nval_ref, nused_ref, x_ref, wg_ref, bg_ref, wu_ref, bu_ref, wd_ref, bd_ref, *rest,
                   reuse_y):
    y_ref = rest[-1]
    s = pl.program_id(0)
    f = pl.program_id(1)
    d = y_ref.shape[1]

    if not reuse_y:
        @pl.when(jnp.logical_and(s >= nused_ref[0], f == 0))
        def _():
            y_ref[...] = jnp.zeros_like(y_ref)

    @pl.when(s < nused_ref[0])
    def _():
        @pl.when(f == 0)
        def _():
            y_ref[...] = jnp.broadcast_to(bd_ref[...], y_ref.shape)

        nsub = (nval_ref[s] + EXP_SUB - 1) // EXP_SUB

        for n in range(1, EXP_ROWS // EXP_SUB + 1):
            @pl.when(nsub == n)
            def _(m=n * EXP_SUB):
                x = _unpack_bf16_pairs(x_ref[0:m, :])
                g = jnp.dot(x, wg_ref[...].astype(BF16), preferred_element_type=F32) + bg_ref[...]
                g = jnp.minimum(g, SWIGLU_LIMIT)
                u = jnp.dot(x, wu_ref[...].astype(BF16), preferred_element_type=F32) + bu_ref[...]
                u = jnp.clip(u, -SWIGLU_LIMIT, SWIGLU_LIMIT)
                a = (g * jax.nn.sigmoid(SWIGLU_ALPHA * g) * (u + 1.0)).astype(BF16)
                wd = wd_ref[...].astype(BF16)
                for c in range(d // EXP_DN):
                    cols = slice(c * EXP_DN, (c + 1) * EXP_DN)
                    y_ref[0:m, cols] += jnp.dot(a, wd[:, cols], preferred_element_type=F32)


def _experts(sb_expert, sb_nvalid, n_used, xs, layer, w_gate, b_gate, w_up, b_up, w_down, b_down, nsb_max,
             y_init=None):
    d = w_gate.shape[2]
    dff = w_gate.shape[3]
    assert d % EXP_DN == 0 and dff % EXP_FF == 0 and EXP_ROWS % EXP_SUB == 0
    nf = dff // EXP_FF
    reuse_y = y_init is not None

    def xmap(s, f, sbe, nval, nused):
        return (jnp.minimum(s, nused[0] - 1), 0)

    def ymap(s, f, sbe, nval, nused):
        return xmap(s, f, sbe, nval, nused) if reuse_y else (s, 0)

    def ff(s, f, nused):
        return jnp.where(s < nused[0], f, nf - 1)

    b_gate4 = b_gate.reshape(b_gate.shape[0], N_EXPERTS, 1, dff)
    b_up4 = b_up.reshape(b_up.shape[0], N_EXPERTS, 1, dff)
    b_down4 = b_down.reshape(b_down.shape[0], N_EXPERTS, 1, d)
    extra_in = [y_init] if reuse_y else []
    extra_specs = [pl.BlockSpec(memory_space=pl.ANY)] if reuse_y else []
    n_in = 3 + 7
    return pl.pallas_call(
        functools.partial(_expert_kernel, reuse_y=reuse_y),
        out_shape=jax.ShapeDtypeStruct((nsb_max * EXP_ROWS, d), F32),
        grid_spec=pltpu.PrefetchScalarGridSpec(
            num_scalar_prefetch=3,
            grid=(nsb_max, nf),
            in_specs=[
                pl.BlockSpec((EXP_ROWS, d // 2), xmap),
                pl.BlockSpec((None, None, d, EXP_FF), lambda s, f, sbe, nval, nused: (layer, sbe[s], 0, ff(s, f, nused))),
                pl.BlockSpec((None, None, 1, EXP_FF), lambda s, f, sbe, nval, nused: (layer, sbe[s], 0, ff(s, f, nused))),
                pl.BlockSpec((None, None, d, EXP_FF), lambda s, f, sbe, nval, nused: (layer, sbe[s], 0, ff(s, f, nused))),
                pl.BlockSpec((None, None, 1, EXP_FF), lambda s, f, sbe, nval, nused: (layer, sbe[s], 0, ff(s, f, nused))),
                pl.BlockSpec((None, None, EXP_FF, d), lambda s, f, sbe, nval, nused: (layer, sbe[s], ff(s, f, nused), 0)),
                pl.BlockSpec((None, None, 1, d), lambda s, f, sbe, nval, nused: (layer, sbe[s], 0, 0)),
            ] + extra_specs,
            out_specs=pl.BlockSpec((EXP_ROWS, d), ymap),
        ),
        input_output_aliases={n_in: 0} if reuse_y else {},
        compiler_params=_params(("arbitrary", "arbitrary")),
        name="experts",
    )(sb_expert, sb_nvalid, n_used, xs, w_gate, b_gate4, w_up, b_up4, w_down, b_down4, *extra_in)


def _combine_kernel(dest_hbm, y_hbm, x_ref, gate_ref, fg_ref, o_ref, dest0, dest1, ybuf0, ybuf1, sems, *,
                    final_norm):
    dest_smem = (dest0, dest1)
    ybuf = (ybuf0, ybuf1)
    i = pl.program_id(0)
    nsteps = pl.num_programs(0)
    tm = x_ref.shape[0]
    n = tm * TOP_K
    slot = i % 2
    nslot = 1 - slot

    def idx_copy(step, sl):
        return pltpu.make_async_copy(dest_hbm.at[pl.ds(step * n, n)], dest_smem[sl], sems.at[sl])

    def issue_rows(sl):
        def issue(t, _):
            for k in range(TOP_K):
                pltpu.make_async_copy(y_hbm.at[pl.ds(dest_smem[sl][t * TOP_K + k], 1)],
                                      ybuf[sl].at[pl.ds(k * tm + t, 1)], sems.at[2 + sl]).start()
            return 0

        lax.fori_loop(0, tm, issue, 0, unroll=2)

    @pl.when(i == 0)
    def _():
        first = idx_copy(0, 0)
        first.start()
        first.wait()
        issue_rows(0)

        @pl.when(nsteps > 1)
        def _():
            idx_copy(1, 1).start()

    for sl in range(2):
        @pl.when(jnp.logical_and(i + 1 < nsteps, nslot == sl))
        def _(sl=sl):
            idx_copy(i + 1, sl).wait()
            issue_rows(sl)

    for sl in range(2):
        @pl.when(slot == sl)
        def _(sl=sl):
            @pl.when(i + 2 < nsteps)
            def _():
                idx_copy(i + 2, sl).start()

            pltpu.make_async_copy(y_hbm.at[pl.ds(0, n)], ybuf[sl], sems.at[2 + sl]).wait()
            gate = gate_ref[...]
            acc = gate[:, 0:1] * ybuf[sl][0:tm]
            for k in range(1, TOP_K):
                acc = acc + gate[:, k:k + 1] * ybuf[sl][k * tm:(k + 1) * tm]
            out = x_ref[...] + acc
            if final_norm:
                out = _rms(out) * fg_ref[...]
            o_ref[...] = out


def _combine(dest_flat, y, x, gates, final_g, final_norm):
    t, d = x.shape
    tm = COMBINE_TM
    return pl.pallas_call(
        functools.partial(_combine_kernel, final_norm=final_norm),
        out_shape=jax.ShapeDtypeStruct((t, d), F32),
        grid=(t // tm,),
        in_specs=[
            pl.BlockSpec(memory_space=pl.ANY),
            pl.BlockSpec(memory_space=pl.ANY),
            pl.BlockSpec((tm, d), lambda i: (i, 0)),
            pl.BlockSpec((tm, LANES), lambda i: (i, 0)),
            pl.BlockSpec((1, d), lambda i: (0, 0)),
        ],
        out_specs=pl.BlockSpec((tm, d), lambda i: (i, 0)),
        scratch_shapes=[pltpu.SMEM((tm * TOP_K,), jnp.int32), pltpu.SMEM((tm * TOP_K,), jnp.int32),
                        pltpu.VMEM((TOP_K * tm, d), F32), pltpu.VMEM((TOP_K * tm, d), F32),
                        pltpu.SemaphoreType.DMA((4,))],
        compiler_params=_params(("arbitrary",)),
        name="combine",
    )(dest_flat, y, x, gates, final_g.reshape(1, d))


def _in_proj_weights(w_in):
    d = w_in.shape[0]
    o_gate = 3 * D_FOX
    o_sb = o_gate + N_FOX
    o_cq = o_sb + 3 * D_SB
    o_ckv = o_cq + Q_LORA
    o_kr = o_ckv + KV_LORA
    o_sgu = o_kr + MLA_ROPE
    half = MLA_ROPE // 2
    w_qkv = jnp.concatenate([w_in[:, :o_gate], w_in[:, o_sb:o_cq]], axis=1).astype(BF16)
    kr = w_in[:, o_kr:o_sgu]
    zpad = jnp.zeros((d, LANES - MLA_ROPE), w_in.dtype)
    w_misc = jnp.concatenate([
        w_in[:, o_sgu:o_sgu + 2 * D_SGU],
        w_in[:, o_cq:o_ckv],
        w_in[:, o_ckv:o_kr],
        kr, zpad,
        kr[:, half:], kr[:, :half], zpad,
        w_in[:, o_gate:o_sb], jnp.zeros((d, LANES - N_FOX), w_in.dtype),
    ], axis=1).astype(BF16)
    assert w_misc.shape[1] == MISC_W
    return w_qkv, w_misc


def _mla_weights(w_q_b, w_kv_b):
    half = MLA_ROPE // 2
    wq = w_q_b.reshape(Q_LORA, N_MLA, MLA_NOPE + MLA_ROPE)
    x1 = wq[:, :, MLA_NOPE:MLA_NOPE + half]
    x2 = wq[:, :, MLA_NOPE + half:]
    z = jnp.zeros((Q_LORA, N_MLA, LANES - MLA_ROPE), w_q_b.dtype)
    wq_r = jnp.concatenate([wq[:, :, :MLA_NOPE], x1, x2, z, x2, x1, z], axis=2)
    wq_r = wq_r.reshape(Q_LORA, N_MLA * 3 * LANES).astype(BF16)
    wkv = w_kv_b.reshape(KV_LORA, N_MLA, MLA_NOPE + MLA_V)
    wkv_r = jnp.concatenate([wkv[:, :, :MLA_NOPE].reshape(KV_LORA, -1),
                             wkv[:, :, MLA_NOPE:].reshape(KV_LORA, -1)], axis=1).astype(BF16)
    return wq_r, wkv_r


def _routing_tables(idx, rank, cnt, n_tokens):
    nsb_max = N_EXPERTS + (n_tokens * TOP_K) // EXP_ROWS
    counts = cnt[0, :N_EXPERTS]
    nsb_e = (counts + EXP_ROWS - 1) // EXP_ROWS
    sb_end = jnp.cumsum(nsb_e)
    sb_start = sb_end - nsb_e
    n_used = sb_end[-1]
    top_idx = idx[:, :TOP_K]
    dest = (sb_start * EXP_ROWS)[top_idx] + rank[:, :TOP_K]
    s_ids = jnp.arange(nsb_max, dtype=jnp.int32)
    sb_e = jnp.minimum(jnp.searchsorted(sb_end, s_ids, side='right'), N_EXPERTS - 1).astype(jnp.int32)
    last_e = sb_e[jnp.maximum(n_used - 1, 0)]
    sb_e = jnp.where(s_ids < n_used, sb_e, last_e)
    nval = jnp.clip(counts[sb_e] - (s_ids - sb_start[sb_e]) * EXP_ROWS, 0, EXP_ROWS)
    nval = jnp.where(s_ids < n_used, nval, 0).astype(jnp.int32)
    return dest.reshape(-1).astype(jnp.int32), sb_e, nval, n_used.reshape(1).astype(jnp.int32), nsb_max


def _layer(x, pos, l, p, final_g, final_norm, batch, seq, moe_bufs):
    t = x.shape[0]
    w_qkv, w_misc = _in_proj_weights(p["w_in"][l])
    qkv = _norm_matmul(x, p["norm_mix_g"][l], w_qkv, BF16, 512, 1536, "in_proj_qkv")
    misc = _norm_matmul(x, p["norm_mix_g"][l], w_misc, F32, 512, MISC_W, "in_proj_misc")

    crow = _fox_prep(misc, p["b_forget"][l], batch, seq)
    nkb = seq // ATT_TILE
    crow = crow.reshape(batch, nkb, N_FOX // FOX_HEADS, FOX_HEADS, ATT_TILE).transpose(0, 2, 1, 3, 4)
    o_fox = _fox_attention(qkv, crow, batch, seq)
    o_sb = _sb_attention(qkv, batch, seq)
    wq_r, wkv_r = _mla_weights(p["mla_w_q_b"][l], p["mla_w_kv_b"][l])
    q_m, k_m, v_m = _mla_prep(misc, pos, p["mla_q_norm_g"][l], p["mla_kv_norm_g"][l], wq_r, wkv_r)
    o_mla = _mla_attention(q_m, k_m, v_m, batch, seq)
    o_sgu = _sgu(misc, p["sgu_ln_g"][l], p["sgu_ln_b"][l], p["sgu_w_s"][l], p["sgu_b_s"][l])

    x_new, h2p, idx, gates, rank, cnt = _outproj_router(
        o_fox, o_sb, o_mla, o_sgu, x, p["out_norm_g"][l], p["w_o"][l].astype(BF16),
        p["norm_ffn_g"][l], p["router_w"][l], p["router_b"][l])

    dest, sb_e, nval, n_used, nsb_max = _routing_tables(idx, rank, cnt, t)
    xs_prev, y_prev = moe_bufs if moe_bufs is not None else (None, None)
    xs = _scatter_rows(dest, h2p, nsb_max * EXP_ROWS, xs_prev)
    y = _experts(sb_e, nval, n_used, xs, l, p["w_gate"], p["b_gate"], p["w_up"], p["b_up"],
                 p["w_down"], p["b_down"], nsb_max, y_prev)
    return _combine(dest, y, x_new, gates, final_g, final_norm), (xs, y)


def kernel(x, positions, norm_mix_g, w_in, b_forget, mla_q_norm_g, mla_kv_norm_g, mla_w_q_b, mla_w_kv_b,
           sgu_ln_g, sgu_ln_b, sgu_w_s, sgu_b_s, out_norm_g, w_o, norm_ffn_g, router_w, router_b, w_gate,
           b_gate, w_up, b_up, w_down, b_down, final_norm_g):
    batch, seq, d = x.shape
    depth = w_in.shape[0]
    p = dict(norm_mix_g=norm_mix_g, w_in=w_in, b_forget=b_forget, mla_q_norm_g=mla_q_norm_g,
             mla_kv_norm_g=mla_kv_norm_g, mla_w_q_b=mla_w_q_b, mla_w_kv_b=mla_w_kv_b, sgu_ln_g=sgu_ln_g,
             sgu_ln_b=sgu_ln_b, sgu_w_s=sgu_w_s, sgu_b_s=sgu_b_s, out_norm_g=out_norm_g, w_o=w_o,
             norm_ffn_g=norm_ffn_g, router_w=router_w, router_b=router_b, w_gate=w_gate, b_gate=b_gate,
             w_up=w_up, b_up=b_up, w_down=w_down, b_down=b_down)
    h = x.reshape(batch * seq, d)
    pos = positions.reshape(batch * seq, 1)
    moe_bufs = None
    for l in range(depth):
        h, moe_bufs = _layer(h, pos, l, p, final_norm_g, l == depth - 1, batch, seq, moe_bufs)
    return h.reshape(batch, seq, d)
```

```python
import functools

import numpy as np
import jax
import jax.numpy as jnp
from jax import lax
from jax.experimental import pallas as pl
from jax.experimental.pallas import tpu as pltpu

HEAD_DIM = 64
N_FOX = 8
N_SB = 8
N_MLA = 4
MLA_NOPE = 128
MLA_ROPE = 64
MLA_V = 128
Q_LORA = 512
KV_LORA = 256
N_SGU = 8
SGU_CH = 64
CHUNK = 128
ROPE_THETA = 10000.0
D_FOX = N_FOX * HEAD_DIM
D_SB = N_SB * HEAD_DIM
D_MLA = N_MLA * MLA_V
D_SGU = N_SGU * SGU_CH
N_EXPERTS = 32
TOP_K = 4
SWIGLU_LIMIT = 7.0
SWIGLU_ALPHA = 1.702
RMS_EPS = 1e-6
LN_EPS = 1e-5

LANES = 128
VMEM_LIMIT = 56 * 1024 * 1024

ATT_TILE = 256
FOX_HEADS = 8
MLA_HEADS = 4
SB_HEADS = 8
EXP_ROWS = 1536
EXP_SUB = 256
EXP_FF = 256
EXP_DN = 512
SCATTER_TM = 512
COMBINE_TM = 256

F32 = jnp.float32
BF16 = jnp.bfloat16

MISC_U = 0
MISC_V = 512
MISC_CQ = 1024
MISC_CKV = 1536
MISC_KR1 = 1792
MISC_KR2 = 1920
MISC_GATE = 2048
MISC_W = 2176


def _params(sem, vmem=VMEM_LIMIT):
    return pltpu.CompilerParams(dimension_semantics=sem, vmem_limit_bytes=vmem)


def _split3(x):
    hi = x.astype(BF16)
    r1 = x - hi.astype(F32)
    mid = r1.astype(BF16)
    lo = (r1 - mid.astype(F32)).astype(BF16)
    return hi, mid, lo


def _log_sigmoid_pair(z):
    t = jnp.log1p(jnp.exp(-jnp.abs(z)))
    return jnp.minimum(z, 0.0) - t, jnp.minimum(-z, 0.0) - t


def _rms(x):
    return x * lax.rsqrt(jnp.mean(x * x, axis=-1, keepdims=True) + RMS_EPS)


def _norm_matmul_kernel(x_ref, g_ref, w_ref, o_ref, xn_ref):
    @pl.when(pl.program_id(1) == 0)
    def _():
        xn_ref[...] = (_rms(x_ref[...]) * g_ref[...]).astype(BF16)

    o_ref[...] = jnp.dot(xn_ref[...], w_ref[...], preferred_element_type=F32).astype(o_ref.dtype)


def _norm_matmul(x, g, w, out_dtype, tm, tn, name):
    t, d = x.shape
    n = w.shape[1]
    return pl.pallas_call(
        _norm_matmul_kernel,
        out_shape=jax.ShapeDtypeStruct((t, n), out_dtype),
        grid=(t // tm, n // tn),
        in_specs=[
            pl.BlockSpec((tm, d), lambda i, j: (i, 0)),
            pl.BlockSpec((1, d), lambda i, j: (0, 0)),
            pl.BlockSpec((d, tn), lambda i, j: (0, j)),
        ],
        out_specs=pl.BlockSpec((tm, tn), lambda i, j: (i, j)),
        scratch_shapes=[pltpu.VMEM((tm, d), BF16)],
        compiler_params=_params(("parallel", "arbitrary")),
        name=name,
    )(x, g.reshape(1, d), w)


def _fox_prep_kernel(gate_ref, bf_ref, crow_ref):
    s_len = gate_ref.shape[0]
    bl = ATT_TILE
    r = lax.broadcasted_iota(jnp.int32, (bl, bl), 0)
    c = lax.broadcasted_iota(jnp.int32, (bl, bl), 1)
    tri = (c <= r).astype(BF16)
    carry = jnp.zeros((1, LANES), F32)
    for i in range(s_len // bl):
        z = gate_ref[i * bl:(i + 1) * bl, :] + bf_ref[...]
        lf, _ = _log_sigmoid_pair(z)
        hi, mid, lo = _split3(lf)
        cs = (jnp.dot(tri, hi, preferred_element_type=F32)
              + jnp.dot(tri, mid, preferred_element_type=F32)
              + jnp.dot(tri, lo, preferred_element_type=F32)) + carry
        crow_ref[i] = cs.T[0:N_FOX, :]
        carry = cs[bl - 1:bl, :]


def _fox_prep(misc, b_forget, batch, seq):
    nkb = seq // ATT_TILE
    bf = jnp.zeros((1, LANES), F32).at[0, :N_FOX].set(b_forget)
    return pl.pallas_call(
        _fox_prep_kernel,
        out_shape=jax.ShapeDtypeStruct((batch, nkb, N_FOX, ATT_TILE), F32),
        grid=(batch,),
        in_specs=[
            pl.BlockSpec((seq, LANES), lambda b: (b, MISC_GATE // LANES)),
            pl.BlockSpec((1, LANES), lambda b: (0, 0)),
        ],
        out_specs=pl.BlockSpec((None, nkb, N_FOX, ATT_TILE), lambda b: (b, 0, 0, 0)),
        compiler_params=_params(("parallel",)),
        name="fox_prep",
    )(misc, bf)


def _causal_mask(tq):
    row = lax.broadcasted_iota(jnp.int32, (tq, tq), 0)
    col = lax.broadcasted_iota(jnp.int32, (tq, tq), 1)
    return col <= row, col < row


def _softmax_sweep(qs, k_of, v_of, bias_of, scale, qi, tq, ones_lane=None):
    nh = len(qs)
    incl, _ = _causal_mask(tq)

    def step(kb, carry, diag):
        ks = pl.multiple_of(kb * tq, tq)
        ss = [lax.dot_general(qs[h], k_of(h, ks), (((1,), (1,)), ((), ())), preferred_element_type=F32)
              for h in range(nh)]
        ps, stats = [], []
        for h in range(nh):
            m, l, _ = carry[h]
            s = ss[h]
            if scale is not None:
                s = s * scale
            b = bias_of(h, kb)
            if b is not None:
                s = s + b
            if diag:
                s = jnp.where(incl, s, -jnp.inf)
            m_new = jnp.maximum(m, jnp.max(s, axis=-1, keepdims=True))
            alpha = jnp.exp(m - m_new)
            p = jnp.exp(s - m_new)
            if ones_lane is None:
                l = alpha * l + jnp.sum(p, axis=-1, keepdims=True)
            stats.append((m_new, l, alpha))
            ps.append(p.astype(BF16))
        out = []
        for h in range(nh):
            m_new, l, alpha = stats[h]
            acc = alpha * carry[h][2] + jnp.dot(ps[h], v_of(h, ks), preferred_element_type=F32)
            out.append((m_new, l, acc))
        return tuple(out)

    dv = v_of(0, 0).shape[-1]
    l0 = jnp.zeros((tq, 1), F32) if ones_lane is None else None
    init = tuple((jnp.full((tq, 1), -jnp.inf, F32), l0, jnp.zeros((tq, dv), F32)) for _ in range(nh))
    carry = lax.fori_loop(0, qi, lambda kb, c: step(kb, c, False), init)
    carry = step(qi, carry, True)
    if ones_lane is None:
        return [acc / l for (_, l, acc) in carry]
    return [acc / acc[:, ones_lane(h):ones_lane(h) + 1] for h, (_, _, acc) in enumerate(carry)]


def _fox_attn_kernel(q_ref, k_ref, v_ref, crow_ref, o_ref):
    tq = q_ref.shape[0]
    qi = pl.program_id(2)
    q = q_ref[...] * jnp.asarray(HEAD_DIM ** -0.5, BF16)
    qs = [q[:, j * HEAD_DIM:(j + 1) * HEAD_DIM] for j in range(FOX_HEADS)]

    def k_of(h, ks):
        return k_ref[pl.ds(ks, tq), h * HEAD_DIM:(h + 1) * HEAD_DIM]

    first = lax.broadcasted_iota(jnp.int32, (tq, LANES), 1) < HEAD_DIM
    one = jnp.ones((tq, LANES), BF16)

    def v_of(h, ks):
        pair = v_ref[pl.ds(ks, tq), (h // 2) * LANES:(h // 2 + 1) * LANES]
        return jnp.where(first, pair, one) if h % 2 == 0 else jnp.where(first, one, pair)

    def bias_of(h, kb):
        return -crow_ref[kb][h:h + 1, :]

    outs = _softmax_sweep(qs, k_of, v_of, bias_of, None, qi, tq,
                          ones_lane=lambda h: HEAD_DIM if h % 2 == 0 else 0)
    o_ref[...] = jnp.concatenate([jnp.where(first, outs[h], outs[h + 1]) for h in range(0, FOX_HEADS, 2)],
                                 axis=-1)


def _fox_attention(qkv, crow, batch, seq):
    t = qkv.shape[0]
    tq = ATT_TILE
    nq = seq // tq
    npair = N_FOX // FOX_HEADS
    w = FOX_HEADS * HEAD_DIM
    return pl.pallas_call(
        _fox_attn_kernel,
        out_shape=jax.ShapeDtypeStruct((t, D_FOX), F32),
        grid=(batch, npair, nq),
        in_specs=[
            pl.BlockSpec((tq, w), lambda b, h, i: (b * nq + i, h)),
            pl.BlockSpec((seq, w), lambda b, h, i: (b, npair + h)),
            pl.BlockSpec((seq, w), lambda b, h, i: (b, 2 * npair + h)),
            pl.BlockSpec((None, None, nq, FOX_HEADS, tq), lambda b, h, i: (b, h, 0, 0, 0)),
        ],
        out_specs=pl.BlockSpec((tq, w), lambda b, h, i: (b * nq + i, h)),
        compiler_params=_params(("parallel", "parallel", "arbitrary")),
        name="fox_attn",
    )(qkv, qkv, qkv, crow)


def _mla_attn_kernel(q_ref, k_ref, v_ref, o_ref):
    tq = q_ref.shape[0]
    qi = pl.program_id(2)
    dk = 2 * LANES
    scale = (MLA_NOPE + MLA_ROPE) ** -0.5
    qs = [q_ref[:, h * dk:(h + 1) * dk] for h in range(MLA_HEADS)]
    outs = _softmax_sweep(qs,
                          lambda h, ks: k_ref[pl.ds(ks, tq), h * dk:(h + 1) * dk],
                          lambda h, ks: v_ref[pl.ds(ks, tq), h * MLA_V:(h + 1) * MLA_V],
                          lambda h, kb: None, scale, qi, tq)
    o_ref[...] = jnp.concatenate(outs, axis=-1)


def _mla_attention(q, k, v, batch, seq):
    t = q.shape[0]
    tq = ATT_TILE
    nq = seq // tq
    dk = MLA_HEADS * 2 * LANES
    dv = MLA_HEADS * MLA_V
    return pl.pallas_call(
        _mla_attn_kernel,
        out_shape=jax.ShapeDtypeStruct((t, D_MLA), F32),
        grid=(batch, N_MLA // MLA_HEADS, nq),
        in_specs=[
            pl.BlockSpec((tq, dk), lambda b, h, i: (b * nq + i, h)),
            pl.BlockSpec((seq, dk), lambda b, h, i: (b, h)),
            pl.BlockSpec((seq, dv), lambda b, h, i: (b, h)),
        ],
        out_specs=pl.BlockSpec((tq, dv), lambda b, h, i: (b * nq + i, h)),
        compiler_params=_params(("parallel", "parallel", "arbitrary")),
        name="mla_attn",
    )(q, k, v)


def _sb_attn_kernel(q_ref, k_ref, v_ref, o_ref):
    tq = q_ref.shape[0]
    qi = pl.program_id(2)
    q = q_ref[...] * jnp.asarray(HEAD_DIM ** -0.5, BF16)
    nh = SB_HEADS
    qs = [q[:, j * HEAD_DIM:(j + 1) * HEAD_DIM] for j in range(nh)]
    _, strict = _causal_mask(tq)
    row = lax.broadcasted_iota(jnp.int32, (tq, tq), 0)
    col = lax.broadcasted_iota(jnp.int32, (tq, tq), 1)
    later = (row > col).astype(BF16)

    def step(kb, carry, diag):
        ks = pl.multiple_of(kb * tq, tq)
        zs = [lax.dot_general(qs[h], k_ref[pl.ds(ks, tq), h * HEAD_DIM:(h + 1) * HEAD_DIM],
                              (((1,), (1,)), ((), ())), preferred_element_type=F32) for h in range(nh)]
        lss, lrs = [], []
        for h in range(nh):
            z = zs[h]
            ls = jnp.minimum(z, 0.0) - jnp.log(1.0 + jnp.exp(-jnp.abs(z)))
            lr = ls - z
            if diag:
                lr = jnp.where(strict, lr, 0.0)
            lss.append(ls)
            lrs.append(lr)
        afters = []
        for h in range(nh):
            hi = lrs[h].astype(BF16)
            lo = (lrs[h] - hi.astype(F32)).astype(BF16)
            afters.append((jnp.dot(hi, later, preferred_element_type=F32)
                           + jnp.dot(lo, later, preferred_element_type=F32)) + carry[h][0])
        out = []
        for h in range(nh):
            a = jnp.exp(lss[h] + afters[h])
            if diag:
                a = jnp.where(strict, a, 0.0)
            v = v_ref[pl.ds(ks, tq), h * HEAD_DIM:(h + 1) * HEAD_DIM]
            acc = carry[h][1] + jnp.dot(a.astype(BF16), v, preferred_element_type=F32)
            out.append((afters[h][:, 0:1] + lrs[h][:, 0:1], acc))
        return tuple(out)

    init = tuple((jnp.zeros((tq, 1), F32), jnp.zeros((tq, HEAD_DIM), F32)) for _ in range(nh))
    carry = step(qi, init, True)
    carry = lax.fori_loop(0, qi, lambda i, c: step(qi - 1 - i, c, False), carry)
    o_ref[...] = jnp.concatenate([acc for (_, acc) in carry], axis=-1)


def _sb_attention(qkv, batch, seq):
    t = qkv.shape[0]
    tq = ATT_TILE
    nq = seq // tq
    npair = N_SB // SB_HEADS
    w = SB_HEADS * HEAD_DIM
    base = 3 * D_FOX // w
    return pl.pallas_call(
        _sb_attn_kernel,
        out_shape=jax.ShapeDtypeStruct((t, D_SB), F32),
        grid=(batch, npair, nq),
        in_specs=[
            pl.BlockSpec((tq, w), lambda b, h, i: (b * nq + i, base + h)),
            pl.BlockSpec((seq, w), lambda b, h, i: (b, base + npair + h)),
            pl.BlockSpec((seq, w), lambda b, h, i: (b, base + 2 * npair + h)),
        ],
        out_specs=pl.BlockSpec((tq, w), lambda b, h, i: (b * nq + i, h)),
        compiler_params=_params(("parallel", "parallel", "arbitrary")),
        name="sb_attn",
    )(qkv, qkv, qkv)


def _mla_prep_kernel(cq_ref, ckv_ref, kr1_ref, kr2_ref, pos_ref, gq_ref, gkv_ref, wq_ref, wkv_ref,
                     invf_ref, sgn_ref, q_out, k_out, v_out):
    ang = pos_ref[...].astype(F32) * invf_ref[...]
    cosv = jnp.cos(ang)
    sinv = jnp.sin(ang) * sgn_ref[...]
    qn = (_rms(cq_ref[...]) * gq_ref[...]).astype(BF16)
    qa = jnp.dot(qn, wq_ref[...], preferred_element_type=F32)
    for h in range(N_MLA):
        o = h * 3 * LANES
        pe = qa[:, o + LANES:o + 2 * LANES] * cosv + qa[:, o + 2 * LANES:o + 3 * LANES] * sinv
        q_out[:, h * 2 * LANES:h * 2 * LANES + LANES] = qa[:, o:o + LANES].astype(BF16)
        q_out[:, h * 2 * LANES + LANES:(h + 1) * 2 * LANES] = pe.astype(BF16)
    kvn = (_rms(ckv_ref[...]) * gkv_ref[...]).astype(BF16)
    kva = jnp.dot(kvn, wkv_ref[...], preferred_element_type=F32)
    kpe = (kr1_ref[...] * cosv + kr2_ref[...] * sinv).astype(BF16)
    for h in range(N_MLA):
        k_out[:, h * 2 * LANES:h * 2 * LANES + LANES] = kva[:, h * LANES:(h + 1) * LANES].astype(BF16)
        k_out[:, h * 2 * LANES + LANES:(h + 1) * 2 * LANES] = kpe
    v_out[...] = kva[:, N_MLA * MLA_NOPE:].astype(BF16)


def _mla_prep(misc, pos, gq, gkv, wq, wkv, tm=512):
    t = misc.shape[0]
    half = MLA_ROPE // 2
    inv_freq = ROPE_THETA ** (-jnp.arange(half, dtype=F32) / half)
    invf = jnp.tile(inv_freq, LANES // half).reshape(1, LANES)
    sgn = jnp.tile(jnp.concatenate([-jnp.ones((half,), F32), jnp.ones((half,), F32)]),
                   LANES // MLA_ROPE).reshape(1, LANES)
    const = lambda i: (0, 0)
    return pl.pallas_call(
        _mla_prep_kernel,
        out_shape=(jax.ShapeDtypeStruct((t, N_MLA * 2 * LANES), BF16),
                   jax.ShapeDtypeStruct((t, N_MLA * 2 * LANES), BF16),
                   jax.ShapeDtypeStruct((t, D_MLA), BF16)),
        grid=(t // tm,),
        in_specs=[
            pl.BlockSpec((tm, Q_LORA), lambda i: (i, MISC_CQ // Q_LORA)),
            pl.BlockSpec((tm, KV_LORA), lambda i: (i, MISC_CKV // KV_LORA)),
            pl.BlockSpec((tm, LANES), lambda i: (i, MISC_KR1 // LANES)),
            pl.BlockSpec((tm, LANES), lambda i: (i, MISC_KR2 // LANES)),
            pl.BlockSpec((tm, 1), lambda i: (i, 0)),
            pl.BlockSpec((1, Q_LORA), const),
            pl.BlockSpec((1, KV_LORA), const),
            pl.BlockSpec(wq.shape, const),
            pl.BlockSpec(wkv.shape, const),
            pl.BlockSpec((1, LANES), const),
            pl.BlockSpec((1, LANES), const),
        ],
        out_specs=(pl.BlockSpec((tm, N_MLA * 2 * LANES), lambda i: (i, 0)),
                   pl.BlockSpec((tm, N_MLA * 2 * LANES), lambda i: (i, 0)),
                   pl.BlockSpec((tm, D_MLA), lambda i: (i, 0))),
        compiler_params=_params(("parallel",)),
        name="mla_prep",
    )(misc, misc, misc, misc, pos, gq.reshape(1, -1), gkv.reshape(1, -1), wq, wkv, invf, sgn)


def _gelu(x):
    return 0.5 * x * (1.0 + lax.erf(x * np.float32(np.sqrt(0.5))))


def _sgu_kernel(u_ref, v_ref, lng_ref, lnb_ref, w_ref, bias_ref, o_ref):
    tm = u_ref.shape[0]
    v = _gelu(v_ref[...])
    mu = jnp.mean(v, axis=-1, keepdims=True)
    xc = v - mu
    var = jnp.mean(xc * xc, axis=-1, keepdims=True)
    vb = (xc * lax.rsqrt(var + LN_EPS) * lng_ref[...] + lnb_ref[...]).astype(BF16)
    row = lax.broadcasted_iota(jnp.int32, (CHUNK, CHUNK), 0)
    col = lax.broadcasted_iota(jnp.int32, (CHUNK, LANES), 1)
    tril = lax.broadcasted_iota(jnp.int32, (CHUNK, CHUNK), 1) <= row
    first = col < SGU_CH
    ws = [jnp.where(tril, w_ref[g], 0.0).astype(BF16) for g in range(N_SGU)]
    zero = jnp.zeros((CHUNK, LANES), BF16)
    for c in range(tm // CHUNK):
        rows = slice(c * CHUNK, (c + 1) * CHUNK)
        for p in range(N_SGU // 2):
            cols = slice(p * LANES, (p + 1) * LANES)
            vp = vb[rows, cols]
            mixed = (jnp.dot(ws[2 * p], jnp.where(first, vp, zero), preferred_element_type=F32)
                     + jnp.dot(ws[2 * p + 1], jnp.where(first, zero, vp), preferred_element_type=F32))
            o_ref[rows, cols] = _gelu(u_ref[rows, cols]) * (mixed + bias_ref[:, cols])


def _sgu(misc, ln_g, ln_b, w_s, b_s, tm=512):
    t = misc.shape[0]
    bias = jnp.repeat(b_s.T, SGU_CH, axis=1)
    const2 = lambda i: (0, 0)
    return pl.pallas_call(
        _sgu_kernel,
        out_shape=jax.ShapeDtypeStruct((t, D_SGU), F32),
        grid=(t // tm,),
        in_specs=[
            pl.BlockSpec((tm, D_SGU), lambda i: (i, MISC_U // D_SGU)),
            pl.BlockSpec((tm, D_SGU), lambda i: (i, MISC_V // D_SGU)),
            pl.BlockSpec((1, D_SGU), const2),
            pl.BlockSpec((1, D_SGU), const2),
            pl.BlockSpec((N_SGU, CHUNK, CHUNK), lambda i: (0, 0, 0)),
            pl.BlockSpec((CHUNK, D_SGU), const2),
        ],
        out_specs=pl.BlockSpec((tm, D_SGU), lambda i: (i, 0)),
        compiler_params=_params(("parallel",)),
        name="sgu",
    )(misc, misc, ln_g.reshape(1, -1), ln_b.reshape(1, -1), w_s, bias)


def _pack_bf16_pairs(hb):
    n = hb.shape[1] // 2
    bits = pltpu.bitcast(hb.astype(F32), jnp.uint32)
    return (bits[:, n:] & jnp.uint32(0xFFFF0000)) | (bits[:, :n] >> 16)


def _unpack_bf16_pairs(xu):
    lo = pltpu.bitcast(xu << 16, F32).astype(BF16)
    hi = pltpu.bitcast(xu & jnp.uint32(0xFFFF0000), F32).astype(BF16)
    return jnp.concatenate([lo, hi], axis=1)


def _outproj_router_kernel(of_ref, os_ref, om_ref, og_ref, x_ref, ong_ref, wo_ref, nfg_ref, rw_ref,
                           rb_ref, xo_ref, h2_ref, idx_ref, gate_ref, rank_ref, cnt_ref, carry_ref):
    tm = x_ref.shape[0]
    th = tm // 2
    halves = [slice(h * th, (h + 1) * th) for h in range(2)]

    @pl.when(pl.program_id(0) == 0)
    def _():
        carry_ref[...] = jnp.zeros_like(carry_ref)

    os_ = []
    for rows in halves:
        o = jnp.concatenate([_rms(r[rows, :]) for r in (of_ref, os_ref, om_ref, og_ref)], axis=-1)
        os_.append((o * ong_ref[...]).astype(BF16))
    xns = [x_ref[rows, :] + jnp.dot(o, wo_ref[...], preferred_element_type=F32) for rows, o in zip(halves, os_)]
    hbs = []
    for rows, xn in zip(halves, xns):
        xo_ref[rows, :] = xn
        hb = (_rms(xn) * nfg_ref[...]).astype(BF16)
        h2_ref[rows, :] = _pack_bf16_pairs(hb)
        hbs.append(hb)

    lane = lax.broadcasted_iota(jnp.int32, (th, LANES), 1)
    all_logits = [jnp.dot(hb, rw_ref[...], preferred_element_type=F32) + rb_ref[...] for hb in hbs]
    picks = []
    for logits in all_logits:
        vals = jnp.where(lane < N_EXPERTS, logits, -jnp.inf)
        sels, tops = [], []
        for _ in range(TOP_K):
            m = jnp.max(vals, axis=-1, keepdims=True)
            idx = jnp.min(jnp.where(vals == m, lane, LANES), axis=-1, keepdims=True)
            sel = lane == idx
            vals = jnp.where(sel, -jnp.inf, vals)
            sels.append(sel)
            tops.append((m, idx))
        es = [jnp.exp(m - tops[0][0]) for (m, _) in tops]
        den = es[0] + es[1] + es[2] + es[3]
        mh = jnp.where(sels[0] | sels[1] | sels[2] | sels[3], 1.0, 0.0)
        picks.append((sels, tops, es, den, mh))

    r = lax.broadcasted_iota(jnp.int32, (th, th), 0)
    c = lax.broadcasted_iota(jnp.int32, (th, th), 1)
    before = (c < r).astype(BF16)
    parts = [jnp.dot(before, p[4].astype(BF16), preferred_element_type=F32) for p in picks]
    total = carry_ref[...]
    for rows, part, (sels, tops, es, den, mh) in zip(halves, parts, picks):
        cnt = part + total
        total = cnt[th - 1:th, :] + mh[th - 1:th, :]
        idx_o = jnp.zeros((th, LANES), jnp.int32)
        gate_o = jnp.zeros((th, LANES), F32)
        rank_o = jnp.zeros((th, LANES), jnp.int32)
        for k in range(TOP_K):
            rank_k = jnp.sum(jnp.where(sels[k], cnt, 0.0), axis=-1, keepdims=True).astype(jnp.int32)
            idx_o = jnp.where(lane == k, tops[k][1], idx_o)
            gate_o = jnp.where(lane == k, es[k] / den, gate_o)
            rank_o = jnp.where(lane == k, rank_k, rank_o)
        idx_ref[rows, :] = idx_o
        gate_ref[rows, :] = gate_o
        rank_ref[rows, :] = rank_o
    carry_ref[...] = total
    cnt_ref[...] = jnp.broadcast_to(total, cnt_ref.shape).astype(jnp.int32)


def _outproj_router(o_fox, o_sb, o_mla, o_sgu, x, out_norm_g, w_o, norm_ffn_g, router_w, router_b, tm=512):
    t, d = x.shape
    dg = o_fox.shape[1]
    rw = jnp.zeros((d, LANES), BF16).at[:, :N_EXPERTS].set(router_w.astype(BF16))
    rb = jnp.zeros((1, LANES), F32).at[0, :N_EXPERTS].set(router_b)
    const = lambda i: (0, 0)
    grp = pl.BlockSpec((tm, dg), lambda i: (i, 0))
    lanes_out = pl.BlockSpec((tm, LANES), lambda i: (i, 0))
    return pl.pallas_call(
        _outproj_router_kernel,
        out_shape=(jax.ShapeDtypeStruct((t, d), F32),
                   jax.ShapeDtypeStruct((t, d // 2), jnp.uint32),
                   jax.ShapeDtypeStruct((t, LANES), jnp.int32),
                   jax.ShapeDtypeStruct((t, LANES), F32),
                   jax.ShapeDtypeStruct((t, LANES), jnp.int32),
                   jax.ShapeDtypeStruct((8, LANES), jnp.int32)),
        grid=(t // tm,),
        in_specs=[grp, grp, grp, grp,
                  pl.BlockSpec((tm, d), lambda i: (i, 0)),
                  pl.BlockSpec((1, 4 * dg), const),
                  pl.BlockSpec((4 * dg, d), const),
                  pl.BlockSpec((1, d), const),
                  pl.BlockSpec((d, LANES), const),
                  pl.BlockSpec((1, LANES), const)],
        out_specs=(pl.BlockSpec((tm, d), lambda i: (i, 0)),
                   pl.BlockSpec((tm, d // 2), lambda i: (i, 0)),
                   lanes_out, lanes_out, lanes_out,
                   pl.BlockSpec((8, LANES), const)),
        scratch_shapes=[pltpu.VMEM((1, LANES), F32)],
        compiler_params=_params(("arbitrary",)),
        name="outproj_router",
    )(o_fox, o_sb, o_mla, o_sgu, x, out_norm_g.reshape(1, -1), w_o, norm_ffn_g.reshape(1, -1), rw, rb)


def _scatter_rows_kernel(dest_hbm, x_ref, xs_init_hbm, xs_hbm, dest0, dest1, sems):
    del xs_init_hbm
    dest_smem = (dest0, dest1)
    i = pl.program_id(0)
    nsteps = pl.num_programs(0)
    tm = x_ref.shape[0]
    n = tm * TOP_K

    def idx_copy(step, sl):
        return pltpu.make_async_copy(dest_hbm.at[pl.ds(step * n, n)], dest_smem[sl], sems.at[sl])

    @pl.when(i == 0)
    def _():
        idx_copy(0, 0).start()

    for sl in range(2):
        @pl.when(i % 2 == sl)
        def _(sl=sl):
            @pl.when(i + 1 < nsteps)
            def _():
                idx_copy(i + 1, 1 - sl).start()

            idx_copy(i, sl).wait()

            def issue(t, _):
                for k in range(TOP_K):
                    pltpu.make_async_copy(x_ref.at[pl.ds(t, 1)],
                                          xs_hbm.at[pl.ds(dest_smem[sl][t * TOP_K + k], 1)], sems.at[2]).start()
                return 0

            lax.fori_loop(0, tm, issue, 0, unroll=2)

    for _ in range(TOP_K):
        pltpu.make_async_copy(x_ref, xs_hbm.at[pl.ds(0, tm)], sems.at[2]).wait()


def _scatter_rows(dest_flat, src, nrows, xs_init=None):
    t, w = src.shape
    tm = SCATTER_TM
    if xs_init is None:
        xs_init = jnp.zeros((nrows, w), src.dtype)
    return pl.pallas_call(
        _scatter_rows_kernel,
        out_shape=jax.ShapeDtypeStruct((nrows, w), src.dtype),
        grid=(t // tm,),
        in_specs=[pl.BlockSpec(memory_space=pl.ANY),
                  pl.BlockSpec((tm, w), lambda i: (i, 0)),
                  pl.BlockSpec(memory_space=pl.ANY)],
        out_specs=pl.BlockSpec(memory_space=pl.ANY),
        scratch_shapes=[pltpu.SMEM((tm * TOP_K,), jnp.int32), pltpu.SMEM((tm * TOP_K,), jnp.int32),
                        pltpu.SemaphoreType.DMA((3,))],
        input_output_aliases={2: 0},
        compiler_params=_params(("arbitrary",)),
        name="scatter_rows",
    )(dest_flat, src, xs_init)


def _expert_kernel(sbe_ref, nval_ref, nused_ref, x_ref, wg_ref, bg_ref, wu_ref, bu_ref, wd_ref, bd_ref, *rest,
                   reuse_y):
    y_ref = rest[-1]
    s = pl.program_id(0)
    f = pl.program_id(1)
    d = y_ref.shape[1]

    if not reuse_y:
        @pl.when(jnp.logical_and(s >= nused_ref[0], f == 0))
        def _():
            y_ref[...] = jnp.zeros_like(y_ref)

    @pl.when(s < nused_ref[0])
    def _():
        @pl.when(f == 0)
        def _():
            y_ref[...] = jnp.broadcast_to(bd_ref[...], y_ref.shape)

        nsub = (nval_ref[s] + EXP_SUB - 1) // EXP_SUB

        for n in range(1, EXP_ROWS // EXP_SUB + 1):
            @pl.when(nsub == n)
            def _(m=n * EXP_SUB):
                x = _unpack_bf16_pairs(x_ref[0:m, :])
                g = jnp.dot(x, wg_ref[...].astype(BF16), preferred_element_type=F32) + bg_ref[...]
                g = jnp.minimum(g, SWIGLU_LIMIT)
                u = jnp.dot(x, wu_ref[...].astype(BF16), preferred_element_type=F32) + bu_ref[...]
                u = jnp.clip(u, -SWIGLU_LIMIT, SWIGLU_LIMIT)
                a = (g * jax.nn.sigmoid(SWIGLU_ALPHA * g) * (u + 1.0)).astype(BF16)
                wd = wd_ref[...].astype(BF16)
                for c in range(d // EXP_DN):
                    cols = slice(c * EXP_DN, (c + 1) * EXP_DN)
                    y_ref[0:m, cols] += jnp.dot(a, wd[:, cols], preferred_element_type=F32)


def _experts(sb_expert, sb_nvalid, n_used, xs, layer, w_gate, b_gate, w_up, b_up, w_down, b_down, nsb_max,
             y_init=None):
    d = w_gate.shape[2]
    dff = w_gate.shape[3]
    assert d % EXP_DN == 0 and dff % EXP_FF == 0 and EXP_ROWS % EXP_SUB == 0
    nf = dff // EXP_FF
    reuse_y = y_init is not None

    def xmap(s, f, sbe, nval, nused):
        return (jnp.minimum(s, nused[0] - 1), 0)

    def ymap(s, f, sbe, nval, nused):
        return xmap(s, f, sbe, nval, nused) if reuse_y else (s, 0)

    def ff(s, f, nused):
        return jnp.where(s < nused[0], f, nf - 1)

    b_gate4 = b_gate.reshape(b_gate.shape[0], N_EXPERTS, 1, dff)
    b_up4 = b_up.reshape(b_up.shape[0], N_EXPERTS, 1, dff)
    b_down4 = b_down.reshape(b_down.shape[0], N_EXPERTS, 1, d)
    extra_in = [y_init] if reuse_y else []
    extra_specs = [pl.BlockSpec(memory_space=pl.ANY)] if reuse_y else []
    n_in = 3 + 7
    return pl.pallas_call(
        functools.partial(_expert_kernel, reuse_y=reuse_y),
        out_shape=jax.ShapeDtypeStruct((nsb_max * EXP_ROWS, d), F32),
        grid_spec=pltpu.PrefetchScalarGridSpec(
            num_scalar_prefetch=3,
            grid=(nsb_max, nf),
            in_specs=[
                pl.BlockSpec((EXP_ROWS, d // 2), xmap),
                pl.BlockSpec((None, None, d, EXP_FF), lambda s, f, sbe, nval, nused: (layer, sbe[s], 0, ff(s, f, nused))),
                pl.BlockSpec((None, None, 1, EXP_FF), lambda s, f, sbe, nval, nused: (layer, sbe[s], 0, ff(s, f, nused))),
                pl.BlockSpec((None, None, d, EXP_FF), lambda s, f, sbe, nval, nused: (layer, sbe[s], 0, ff(s, f, nused))),
                pl.BlockSpec((None, None, 1, EXP_FF), lambda s, f, sbe, nval, nused: (layer, sbe[s], 0, ff(s, f, nused))),
                pl.BlockSpec((None, None, EXP_FF, d), lambda s, f, sbe, nval, nused: (layer, sbe[s], ff(s, f, nused), 0)),
                pl.BlockSpec((None, None, 1, d), lambda s, f, sbe, nval, nused: (layer, sbe[s], 0, 0)),
            ] + extra_specs,
            out_specs=pl.BlockSpec((EXP_ROWS, d), ymap),
        ),
        input_output_aliases={n_in: 0} if reuse_y else {},
        compiler_params=_params(("arbitrary", "arbitrary")),
        name="experts",
    )(sb_expert, sb_nvalid, n_used, xs, w_gate, b_gate4, w_up, b_up4, w_down, b_down4, *extra_in)


def _combine_kernel(dest_hbm, y_hbm, x_ref, gate_ref, fg_ref, o_ref, dest0, dest1, ybuf0, ybuf1, sems, *,
                    final_norm):
    dest_smem = (dest0, dest1)
    ybuf = (ybuf0, ybuf1)
    i = pl.program_id(0)
    nsteps = pl.num_programs(0)
    tm = x_ref.shape[0]
    n = tm * TOP_K
    slot = i % 2
    nslot = 1 - slot

    def idx_copy(step, sl):
        return pltpu.make_async_copy(dest_hbm.at[pl.ds(step * n, n)], dest_smem[sl], sems.at[sl])

    def issue_rows(sl):
        def issue(t, _):
            for k in range(TOP_K):
                pltpu.make_async_copy(y_hbm.at[pl.ds(dest_smem[sl][t * TOP_K + k], 1)],
                                      ybuf[sl].at[pl.ds(k * tm + t, 1)], sems.at[2 + sl]).start()
            return 0

        lax.fori_loop(0, tm, issue, 0, unroll=2)

    @pl.when(i == 0)
    def _():
        first = idx_copy(0, 0)
        first.start()
        first.wait()
        issue_rows(0)

        @pl.when(nsteps > 1)
        def _():
            idx_copy(1, 1).start()

    for sl in range(2):
        @pl.when(jnp.logical_and(i + 1 < nsteps, nslot == sl))
        def _(sl=sl):
            idx_copy(i + 1, sl).wait()
            issue_rows(sl)

    for sl in range(2):
        @pl.when(slot == sl)
        def _(sl=sl):
            @pl.when(i + 2 < nsteps)
            def _():
                idx_copy(i + 2, sl).start()

            pltpu.make_async_copy(y_hbm.at[pl.ds(0, n)], ybuf[sl], sems.at[2 + sl]).wait()
            gate = gate_ref[...]
            acc = gate[:, 0:1] * ybuf[sl][0:tm]
            for k in range(1, TOP_K):
                acc = acc + gate[:, k:k + 1] * ybuf[sl][k * tm:(k + 1) * tm]
            out = x_ref[...] + acc
            if final_norm:
                out = _rms(out) * fg_ref[...]
            o_ref[...] = out


def _combine(dest_flat, y, x, gates, final_g, final_norm):
    t, d = x.shape
    tm = COMBINE_TM
    return pl.pallas_call(
        functools.partial(_combine_kernel, final_norm=final_norm),
        out_shape=jax.ShapeDtypeStruct((t, d), F32),
        grid=(t // tm,),
        in_specs=[
            pl.BlockSpec(memory_space=pl.ANY),
            pl.BlockSpec(memory_space=pl.ANY),
            pl.BlockSpec((tm, d), lambda i: (i, 0)),
            pl.BlockSpec((tm, LANES), lambda i: (i, 0)),
            pl.BlockSpec((1, d), lambda i: (0, 0)),
        ],
        out_specs=pl.BlockSpec((tm, d), lambda i: (i, 0)),
        scratch_shapes=[pltpu.SMEM((tm * TOP_K,), jnp.int32), pltpu.SMEM((tm * TOP_K,), jnp.int32),
                        pltpu.VMEM((TOP_K * tm, d), F32), pltpu.VMEM((TOP_K * tm, d), F32),
                        pltpu.SemaphoreType.DMA((4,))],
        compiler_params=_params(("arbitrary",)),
        name="combine",
    )(dest_flat, y, x, gates, final_g.reshape(1, d))


def _in_proj_weights(w_in):
    d = w_in.shape[0]
    o_gate = 3 * D_FOX
    o_sb = o_gate + N_FOX
    o_cq = o_sb + 3 * D_SB
    o_ckv = o_cq + Q_LORA
    o_kr = o_ckv + KV_LORA
    o_sgu = o_kr + MLA_ROPE
    half = MLA_ROPE // 2
    w_qkv = jnp.concatenate([w_in[:, :o_gate], w_in[:, o_sb:o_cq]], axis=1).astype(BF16)
    kr = w_in[:, o_kr:o_sgu]
    zpad = jnp.zeros((d, LANES - MLA_ROPE), w_in.dtype)
    w_misc = jnp.concatenate([
        w_in[:, o_sgu:o_sgu + 2 * D_SGU],
        w_in[:, o_cq:o_ckv],
        w_in[:, o_ckv:o_kr],
        kr, zpad,
        kr[:, half:], kr[:, :half], zpad,
        w_in[:, o_gate:o_sb], jnp.zeros((d, LANES - N_FOX), w_in.dtype),
    ], axis=1).astype(BF16)
    assert w_misc.shape[1] == MISC_W
    return w_qkv, w_misc


def _mla_weights(w_q_b, w_kv_b):
    half = MLA_ROPE // 2
    wq = w_q_b.reshape(Q_LORA, N_MLA, MLA_NOPE + MLA_ROPE)
    x1 = wq[:, :, MLA_NOPE:MLA_NOPE + half]
    x2 = wq[:, :, MLA_NOPE + half:]
    z = jnp.zeros((Q_LORA, N_MLA, LANES - MLA_ROPE), w_q_b.dtype)
    wq_r = jnp.concatenate([wq[:, :, :MLA_NOPE], x1, x2, z, x2, x1, z], axis=2)
    wq_r = wq_r.reshape(Q_LORA, N_MLA * 3 * LANES).astype(BF16)
    wkv = w_kv_b.reshape(KV_LORA, N_MLA, MLA_NOPE + MLA_V)
    wkv_r = jnp.concatenate([wkv[:, :, :MLA_NOPE].reshape(KV_LORA, -1),
                             wkv[:, :, MLA_NOPE:].reshape(KV_LORA, -1)], axis=1).astype(BF16)
    return wq_r, wkv_r


def _routing_tables(idx, rank, cnt, n_tokens):
    nsb_max = N_EXPERTS + (n_tokens * TOP_K) // EXP_ROWS
    counts = cnt[0, :N_EXPERTS]
    nsb_e = (counts + EXP_ROWS - 1) // EXP_ROWS
    sb_end = jnp.cumsum(nsb_e)
    sb_start = sb_end - nsb_e
    n_used = sb_end[-1]
    top_idx = idx[:, :TOP_K].reshape(-1)
    dest = (sb_start * EXP_ROWS)[top_idx] + rank[:, :TOP_K].reshape(-1)
    s_ids = jnp.arange(nsb_max, dtype=jnp.int32)
    sb_e = jnp.minimum(jnp.searchsorted(sb_end, s_ids, side='right'), N_EXPERTS - 1).astype(jnp.int32)
    last_e = sb_e[jnp.maximum(n_used - 1, 0)]
    sb_e = jnp.where(s_ids < n_used, sb_e, last_e)
    nval = jnp.clip(counts[sb_e] - (s_ids - sb_start[sb_e]) * EXP_ROWS, 0, EXP_ROWS)
    nval = jnp.where(s_ids < n_used, nval, 0).astype(jnp.int32)
    return dest.astype(jnp.int32), sb_e, nval, n_used.reshape(1).astype(jnp.int32), nsb_max


def _layer(x, pos, l, p, final_g, final_norm, batch, seq, moe_bufs):
    t = x.shape[0]
    w_qkv, w_misc = _in_proj_weights(p["w_in"][l])
    qkv = _norm_matmul(x, p["norm_mix_g"][l], w_qkv, BF16, 512, 1536, "in_proj_qkv")
    misc = _norm_matmul(x, p["norm_mix_g"][l], w_misc, F32, 512, MISC_W, "in_proj_misc")

    crow = _fox_prep(misc, p["b_forget"][l], batch, seq)
    nkb = seq // ATT_TILE
    crow = crow.reshape(batch, nkb, N_FOX // FOX_HEADS, FOX_HEADS, ATT_TILE).transpose(0, 2, 1, 3, 4)
    o_fox = _fox_attention(qkv, crow, batch, seq)
    o_sb = _sb_attention(qkv, batch, seq)
    wq_r, wkv_r = _mla_weights(p["mla_w_q_b"][l], p["mla_w_kv_b"][l])
    q_m, k_m, v_m = _mla_prep(misc, pos, p["mla_q_norm_g"][l], p["mla_kv_norm_g"][l], wq_r, wkv_r)
    o_mla = _mla_attention(q_m, k_m, v_m, batch, seq)
    o_sgu = _sgu(misc, p["sgu_ln_g"][l], p["sgu_ln_b"][l], p["sgu_w_s"][l], p["sgu_b_s"][l])

    x_new, h2p, idx, gates, rank, cnt = _outproj_router(
        o_fox, o_sb, o_mla, o_sgu, x, p["out_norm_g"][l], p["w_o"][l].astype(BF16),
        p["norm_ffn_g"][l], p["router_w"][l], p["router_b"][l])

    dest, sb_e, nval, n_used, nsb_max = _routing_tables(idx, rank, cnt, t)
    xs_prev, y_prev = moe_bufs if moe_bufs is not None else (None, None)
    xs = _scatter_rows(dest, h2p, nsb_max * EXP_ROWS, xs_prev)
    y = _experts(sb_e, nval, n_used, xs, l, p["w_gate"], p["b_gate"], p["w_up"], p["b_up"],
                 p["w_down"], p["b_down"], nsb_max, y_prev)
    return _combine(dest, y, x_new, gates, final_g, final_norm), (xs, y)


def kernel(x, positions, norm_mix_g, w_in, b_forget, mla_q_norm_g, mla_kv_norm_g, mla_w_q_b, mla_w_kv_b,
           sgu_ln_g, sgu_ln_b, sgu_w_s, sgu_b_s, out_norm_g, w_o, norm_ffn_g, router_w, router_b, w_gate,
           b_gate, w_up, b_up, w_down, b_down, final_norm_g):
    batch, seq, d = x.shape
    depth = w_in.shape[0]
    p = dict(norm_mix_g=norm_mix_g, w_in=w_in, b_forget=b_forget, mla_q_norm_g=mla_q_norm_g,
             mla_kv_norm_g=mla_kv_norm_g, mla_w_q_b=mla_w_q_b, mla_w_kv_b=mla_w_kv_b, sgu_ln_g=sgu_ln_g,
             sgu_ln_b=sgu_ln_b, sgu_w_s=sgu_w_s, sgu_b_s=sgu_b_s, out_norm_g=out_norm_g, w_o=w_o,
             norm_ffn_g=norm_ffn_g, router_w=router_w, router_b=router_b, w_gate=w_gate, b_gate=b_gate,
             w_up=w_up, b_up=b_up, w_down=w_down, b_down=b_down)
    h = x.reshape(batch * seq, d)
    pos = positions.reshape(batch * seq, 1)
    moe_bufs = None
    for l in range(depth):
        h, moe_bufs = _layer(h, pos, l, p, final_norm_g, l == depth - 1, batch, seq, moe_bufs)
    return h.reshape(batch, seq, d)
```

```python
import functools

import numpy as np
import jax
import jax.numpy as jnp
from jax import lax
from jax.experimental import pallas as pl
from jax.experimental.pallas import tpu as pltpu

HEAD_DIM = 64
N_FOX = 8
N_SB = 8
N_MLA = 4
MLA_NOPE = 128
MLA_ROPE = 64
MLA_V = 128
Q_LORA = 512
KV_LORA = 256
N_SGU = 8
SGU_CH = 64
CHUNK = 128
ROPE_THETA = 10000.0
D_FOX = N_FOX * HEAD_DIM
D_SB = N_SB * HEAD_DIM
D_MLA = N_MLA * MLA_V
D_SGU = N_SGU * SGU_CH
N_EXPERTS = 32
TOP_K = 4
SWIGLU_LIMIT = 7.0
SWIGLU_ALPHA = 1.702
RMS_EPS = 1e-6
LN_EPS = 1e-5

LANES = 128
VMEM_LIMIT = 56 * 1024 * 1024

ATT_TILE = 256
FOX_HEADS = 8
MLA_HEADS = 4
SB_HEADS = 8
EXP_ROWS = 1536
EXP_SUB = 256
EXP_FF = 256
EXP_DN = 512
SCATTER_TM = 512
ZERO_ROWS = 512
COMBINE_TM = 256

F32 = jnp.float32
BF16 = jnp.bfloat16

MISC_U = 0
MISC_V = 512
MISC_CQ = 1024
MISC_CKV = 1536
MISC_KR1 = 1792
MISC_KR2 = 1920
MISC_GATE = 2048
MISC_W = 2176


def _params(sem, vmem=VMEM_LIMIT):
    return pltpu.CompilerParams(dimension_semantics=sem, vmem_limit_bytes=vmem)


def _split3(x):
    hi = x.astype(BF16)
    r1 = x - hi.astype(F32)
    mid = r1.astype(BF16)
    lo = (r1 - mid.astype(F32)).astype(BF16)
    return hi, mid, lo


def _log_sigmoid_pair(z):
    t = jnp.log1p(jnp.exp(-jnp.abs(z)))
    return jnp.minimum(z, 0.0) - t, jnp.minimum(-z, 0.0) - t


def _rms(x):
    return x * lax.rsqrt(jnp.mean(x * x, axis=-1, keepdims=True) + RMS_EPS)


def _norm_matmul_kernel(x_ref, g_ref, w_ref, o_ref, xn_ref):
    @pl.when(pl.program_id(1) == 0)
    def _():
        xn_ref[...] = (_rms(x_ref[...]) * g_ref[...]).astype(BF16)

    o_ref[...] = jnp.dot(xn_ref[...], w_ref[...], preferred_element_type=F32).astype(o_ref.dtype)


def _norm_matmul(x, g, w, out_dtype, tm, tn, name):
    t, d = x.shape
    n = w.shape[1]
    return pl.pallas_call(
        _norm_matmul_kernel,
        out_shape=jax.ShapeDtypeStruct((t, n), out_dtype),
        grid=(t // tm, n // tn),
        in_specs=[
            pl.BlockSpec((tm, d), lambda i, j: (i, 0)),
            pl.BlockSpec((1, d), lambda i, j: (0, 0)),
            pl.BlockSpec((d, tn), lambda i, j: (0, j)),
        ],
        out_specs=pl.BlockSpec((tm, tn), lambda i, j: (i, j)),
        scratch_shapes=[pltpu.VMEM((tm, d), BF16)],
        compiler_params=_params(("parallel", "arbitrary")),
        name=name,
    )(x, g.reshape(1, d), w)


def _fox_prep_kernel(gate_ref, bf_ref, crow_ref):
    s_len = gate_ref.shape[0]
    bl = ATT_TILE
    r = lax.broadcasted_iota(jnp.int32, (bl, bl), 0)
    c = lax.broadcasted_iota(jnp.int32, (bl, bl), 1)
    tri = (c <= r).astype(BF16)
    carry = jnp.zeros((1, LANES), F32)
    for i in range(s_len // bl):
        z = gate_ref[i * bl:(i + 1) * bl, :] + bf_ref[...]
        lf, _ = _log_sigmoid_pair(z)
        hi, mid, lo = _split3(lf)
        cs = (jnp.dot(tri, hi, preferred_element_type=F32)
              + jnp.dot(tri, mid, preferred_element_type=F32)
              + jnp.dot(tri, lo, preferred_element_type=F32)) + carry
        crow_ref[i] = cs.T[0:N_FOX, :]
        carry = cs[bl - 1:bl, :]


def _fox_prep(misc, b_forget, batch, seq):
    nkb = seq // ATT_TILE
    bf = jnp.zeros((1, LANES), F32).at[0, :N_FOX].set(b_forget)
    return pl.pallas_call(
        _fox_prep_kernel,
        out_shape=jax.ShapeDtypeStruct((batch, nkb, N_FOX, ATT_TILE), F32),
        grid=(batch,),
        in_specs=[
            pl.BlockSpec((seq, LANES), lambda b: (b, MISC_GATE // LANES)),
            pl.BlockSpec((1, LANES), lambda b: (0, 0)),
        ],
        out_specs=pl.BlockSpec((None, nkb, N_FOX, ATT_TILE), lambda b: (b, 0, 0, 0)),
        compiler_params=_params(("parallel",)),
        name="fox_prep",
    )(misc, bf)


def _causal_mask(tq):
    row = lax.broadcasted_iota(jnp.int32, (tq, tq), 0)
    col = lax.broadcasted_iota(jnp.int32, (tq, tq), 1)
    return col <= row, col < row


def _softmax_sweep(qs, k_of, v_of, bias_of, scale, qi, tq, ones_lane=None):
    nh = len(qs)
    incl, _ = _causal_mask(tq)

    def step(kb, carry, diag):
        ks = pl.multiple_of(kb * tq, tq)
        ss = [lax.dot_general(qs[h], k_of(h, ks), (((1,), (1,)), ((), ())), preferred_element_type=F32)
              for h in range(nh)]
        ps, stats = [], []
        for h in range(nh):
            m, l, _ = carry[h]
            s = ss[h]
            if scale is not None:
                s = s * scale
            b = bias_of(h, kb)
            if b is not None:
                s = s + b
            if diag:
                s = jnp.where(incl, s, -jnp.inf)
            m_new = jnp.maximum(m, jnp.max(s, axis=-1, keepdims=True))
            alpha = jnp.exp(m - m_new)
            p = jnp.exp(s - m_new)
            if ones_lane is None:
                l = alpha * l + jnp.sum(p, axis=-1, keepdims=True)
            stats.append((m_new, l, alpha))
            ps.append(p.astype(BF16))
        out = []
        for h in range(nh):
            m_new, l, alpha = stats[h]
            acc = alpha * carry[h][2] + jnp.dot(ps[h], v_of(h, ks), preferred_element_type=F32)
            out.append((m_new, l, acc))
        return tuple(out)

    dv = v_of(0, 0).shape[-1]
    l0 = jnp.zeros((tq, 1), F32) if ones_lane is None else None
    init = tuple((jnp.full((tq, 1), -jnp.inf, F32), l0, jnp.zeros((tq, dv), F32)) for _ in range(nh))
    carry = lax.fori_loop(0, qi, lambda kb, c: step(kb, c, False), init)
    carry = step(qi, carry, True)
    if ones_lane is None:
        return [acc / l for (_, l, acc) in carry]
    return [acc / acc[:, ones_lane(h):ones_lane(h) + 1] for h, (_, _, acc) in enumerate(carry)]


def _zero_fill_copies(fill_hbm, zeros, sem, grid):
    nq = grid[2]
    ngroups = grid[0] * grid[1]
    rows = zeros.shape[0]
    nblocks = fill_hbm.shape[0] // rows
    per_group = (nblocks + ngroups - 1) // ngroups
    work = nq * (nq + 1) // 2
    max_per_step = (per_group * nq + work - 1) // work + 1
    group = pl.program_id(0) * grid[1] + pl.program_id(1)
    qi = pl.program_id(2)
    first = per_group * (qi * (qi + 1) // 2) // work
    count = per_group * ((qi + 1) * (qi + 2) // 2) // work - first
    copies = []
    for j in range(max_per_step):
        block = group * per_group + first + j
        valid = jnp.logical_and(j < count, block < nblocks)
        at = jnp.minimum(block, nblocks - 1) * rows
        copies.append((valid, pltpu.make_async_copy(zeros, fill_hbm.at[pl.ds(at, rows)], sem)))
    return copies


def _zero_fill_start(fill, grid):
    if not fill:
        return []
    fill_hbm, zeros, sem = fill
    zeros[...] = jnp.zeros_like(zeros)
    copies = _zero_fill_copies(fill_hbm, zeros, sem, grid)
    for valid, cp in copies:
        @pl.when(valid)
        def _(cp=cp):
            cp.start()
    return copies


def _zero_fill_wait(copies):
    for valid, cp in copies:
        @pl.when(valid)
        def _(cp=cp):
            cp.wait()


def _zero_fill_specs(fill):
    if fill is None:
        return [], [], []
    assert fill.shape[0] % ZERO_ROWS == 0
    return ([fill], [pl.BlockSpec(memory_space=pl.ANY)],
            [pltpu.VMEM((ZERO_ROWS, fill.shape[1]), fill.dtype), pltpu.SemaphoreType.DMA(())])


def _fox_attn_kernel(q_ref, k_ref, v_ref, crow_ref, o_ref, *fill, grid):
    copies = _zero_fill_start(fill, grid)
    tq = q_ref.shape[0]
    qi = pl.program_id(2)
    q = q_ref[...] * jnp.asarray(HEAD_DIM ** -0.5, BF16)
    qs = [q[:, j * HEAD_DIM:(j + 1) * HEAD_DIM] for j in range(FOX_HEADS)]

    def k_of(h, ks):
        return k_ref[pl.ds(ks, tq), h * HEAD_DIM:(h + 1) * HEAD_DIM]

    first = lax.broadcasted_iota(jnp.int32, (tq, LANES), 1) < HEAD_DIM
    one = jnp.ones((tq, LANES), BF16)

    def v_of(h, ks):
        pair = v_ref[pl.ds(ks, tq), (h // 2) * LANES:(h // 2 + 1) * LANES]
        return jnp.where(first, pair, one) if h % 2 == 0 else jnp.where(first, one, pair)

    def bias_of(h, kb):
        return -crow_ref[kb][h:h + 1, :]

    outs = _softmax_sweep(qs, k_of, v_of, bias_of, None, qi, tq,
                          ones_lane=lambda h: HEAD_DIM if h % 2 == 0 else 0)
    o_ref[...] = jnp.concatenate([jnp.where(first, outs[h], outs[h + 1]) for h in range(0, FOX_HEADS, 2)],
                                 axis=-1)
    _zero_fill_wait(copies)


def _fox_attention(qkv, crow, batch, seq, fill=None):
    t = qkv.shape[0]
    tq = ATT_TILE
    nq = seq // tq
    npair = N_FOX // FOX_HEADS
    w = FOX_HEADS * HEAD_DIM
    grid = (batch, npair, nq)
    fill_shape, fill_specs, fill_scratch = _zero_fill_specs(fill)
    outs = pl.pallas_call(
        functools.partial(_fox_attn_kernel, grid=grid),
        out_shape=tuple([jax.ShapeDtypeStruct((t, D_FOX), F32)] + fill_shape),
        grid=grid,
        in_specs=[
            pl.BlockSpec((tq, w), lambda b, h, i: (b * nq + i, h)),
            pl.BlockSpec((seq, w), lambda b, h, i: (b, npair + h)),
            pl.BlockSpec((seq, w), lambda b, h, i: (b, 2 * npair + h)),
            pl.BlockSpec((None, None, nq, FOX_HEADS, tq), lambda b, h, i: (b, h, 0, 0, 0)),
        ],
        out_specs=tuple([pl.BlockSpec((tq, w), lambda b, h, i: (b * nq + i, h))] + fill_specs),
        scratch_shapes=fill_scratch,
        compiler_params=_params(("parallel", "parallel", "arbitrary")),
        name="fox_attn",
    )(qkv, qkv, qkv, crow)
    return outs if fill is not None else outs[0]


def _mla_attn_kernel(q_ref, k_ref, v_ref, o_ref):
    tq = q_ref.shape[0]
    qi = pl.program_id(2)
    dk = 2 * LANES
    scale = (MLA_NOPE + MLA_ROPE) ** -0.5
    qs = [q_ref[:, h * dk:(h + 1) * dk] for h in range(MLA_HEADS)]
    outs = _softmax_sweep(qs,
                          lambda h, ks: k_ref[pl.ds(ks, tq), h * dk:(h + 1) * dk],
                          lambda h, ks: v_ref[pl.ds(ks, tq), h * MLA_V:(h + 1) * MLA_V],
                          lambda h, kb: None, scale, qi, tq)
    o_ref[...] = jnp.concatenate(outs, axis=-1)


def _mla_attention(q, k, v, batch, seq):
    t = q.shape[0]
    tq = ATT_TILE
    nq = seq // tq
    dk = MLA_HEADS * 2 * LANES
    dv = MLA_HEADS * MLA_V
    return pl.pallas_call(
        _mla_attn_kernel,
        out_shape=jax.ShapeDtypeStruct((t, D_MLA), F32),
        grid=(batch, N_MLA // MLA_HEADS, nq),
        in_specs=[
            pl.BlockSpec((tq, dk), lambda b, h, i: (b * nq + i, h)),
            pl.BlockSpec((seq, dk), lambda b, h, i: (b, h)),
            pl.BlockSpec((seq, dv), lambda b, h, i: (b, h)),
        ],
        out_specs=pl.BlockSpec((tq, dv), lambda b, h, i: (b * nq + i, h)),
        compiler_params=_params(("parallel", "parallel", "arbitrary")),
        name="mla_attn",
    )(q, k, v)


def _sb_attn_kernel(q_ref, k_ref, v_ref, o_ref, *fill, grid):
    copies = _zero_fill_start(fill, grid)
    tq = q_ref.shape[0]
    qi = pl.program_id(2)
    q = q_ref[...] * jnp.asarray(HEAD_DIM ** -0.5, BF16)
    nh = SB_HEADS
    qs = [q[:, j * HEAD_DIM:(j + 1) * HEAD_DIM] for j in range(nh)]
    _, strict = _causal_mask(tq)
    row = lax.broadcasted_iota(jnp.int32, (tq, tq), 0)
    col = lax.broadcasted_iota(jnp.int32, (tq, tq), 1)
    later = (row > col).astype(BF16)

    def step(kb, carry, diag):
        ks = pl.multiple_of(kb * tq, tq)
        zs = [lax.dot_general(qs[h], k_ref[pl.ds(ks, tq), h * HEAD_DIM:(h + 1) * HEAD_DIM],
                              (((1,), (1,)), ((), ())), preferred_element_type=F32) for h in range(nh)]
        lss, lrs = [], []
        for h in range(nh):
            z = zs[h]
            ls = jnp.minimum(z, 0.0) - jnp.log(1.0 + jnp.exp(-jnp.abs(z)))
            lr = ls - z
            if diag:
                lr = jnp.where(strict, lr, 0.0)
            lss.append(ls)
            lrs.append(lr)
        afters = []
        for h in range(nh):
            hi = lrs[h].astype(BF16)
            lo = (lrs[h] - hi.astype(F32)).astype(BF16)
            afters.append((jnp.dot(hi, later, preferred_element_type=F32)
                           + jnp.dot(lo, later, preferred_element_type=F32)) + carry[h][0])
        out = []
        for h in range(nh):
            a = jnp.exp(lss[h] + afters[h])
            if diag:
                a = jnp.where(strict, a, 0.0)
            v = v_ref[pl.ds(ks, tq), h * HEAD_DIM:(h + 1) * HEAD_DIM]
            acc = carry[h][1] + jnp.dot(a.astype(BF16), v, preferred_element_type=F32)
            out.append((afters[h][:, 0:1] + lrs[h][:, 0:1], acc))
        return tuple(out)

    init = tuple((jnp.zeros((tq, 1), F32), jnp.zeros((tq, HEAD_DIM), F32)) for _ in range(nh))
    carry = step(qi, init, True)
    carry = lax.fori_loop(0, qi, lambda i, c: step(qi - 1 - i, c, False), carry)
    o_ref[...] = jnp.concatenate([acc for (_, acc) in carry], axis=-1)
    _zero_fill_wait(copies)


def _sb_attention(qkv, batch, seq, fill=None):
    t = qkv.shape[0]
    tq = ATT_TILE
    nq = seq // tq
    npair = N_SB // SB_HEADS
    w = SB_HEADS * HEAD_DIM
    base = 3 * D_FOX // w
    grid = (batch, npair, nq)
    fill_shape, fill_specs, fill_scratch = _zero_fill_specs(fill)
    outs = pl.pallas_call(
        functools.partial(_sb_attn_kernel, grid=grid),
        out_shape=tuple([jax.ShapeDtypeStruct((t, D_SB), F32)] + fill_shape),
        grid=grid,
        in_specs=[
            pl.BlockSpec((tq, w), lambda b, h, i: (b * nq + i, base + h)),
            pl.BlockSpec((seq, w), lambda b, h, i: (b, base + npair + h)),
            pl.BlockSpec((seq, w), lambda b, h, i: (b, base + 2 * npair + h)),
        ],
        out_specs=tuple([pl.BlockSpec((tq, w), lambda b, h, i: (b * nq + i, h))] + fill_specs),
        scratch_shapes=fill_scratch,
        compiler_params=_params(("parallel", "parallel", "arbitrary")),
        name="sb_attn",
    )(qkv, qkv, qkv)
    return outs if fill is not None else outs[0]


def _mla_prep_kernel(cq_ref, ckv_ref, kr1_ref, kr2_ref, pos_ref, gq_ref, gkv_ref, wq_ref, wkv_ref,
                     invf_ref, sgn_ref, q_out, k_out, v_out):
    ang = pos_ref[...].astype(F32) * invf_ref[...]
    cosv = jnp.cos(ang)
    sinv = jnp.sin(ang) * sgn_ref[...]
    qn = (_rms(cq_ref[...]) * gq_ref[...]).astype(BF16)
    qa = jnp.dot(qn, wq_ref[...], preferred_element_type=F32)
    for h in range(N_MLA):
        o = h * 3 * LANES
        pe = qa[:, o + LANES:o + 2 * LANES] * cosv + qa[:, o + 2 * LANES:o + 3 * LANES] * sinv
        q_out[:, h * 2 * LANES:h * 2 * LANES + LANES] = qa[:, o:o + LANES].astype(BF16)
        q_out[:, h * 2 * LANES + LANES:(h + 1) * 2 * LANES] = pe.astype(BF16)
    kvn = (_rms(ckv_ref[...]) * gkv_ref[...]).astype(BF16)
    kva = jnp.dot(kvn, wkv_ref[...], preferred_element_type=F32)
    kpe = (kr1_ref[...] * cosv + kr2_ref[...] * sinv).astype(BF16)
    for h in range(N_MLA):
        k_out[:, h * 2 * LANES:h * 2 * LANES + LANES] = kva[:, h * LANES:(h + 1) * LANES].astype(BF16)
        k_out[:, h * 2 * LANES + LANES:(h + 1) * 2 * LANES] = kpe
    v_out[...] = kva[:, N_MLA * MLA_NOPE:].astype(BF16)


def _mla_prep(misc, pos, gq, gkv, wq, wkv, tm=512):
    t = misc.shape[0]
    half = MLA_ROPE // 2
    inv_freq = ROPE_THETA ** (-jnp.arange(half, dtype=F32) / half)
    invf = jnp.tile(inv_freq, LANES // half).reshape(1, LANES)
    sgn = jnp.tile(jnp.concatenate([-jnp.ones((half,), F32), jnp.ones((half,), F32)]),
                   LANES // MLA_ROPE).reshape(1, LANES)
    const = lambda i: (0, 0)
    return pl.pallas_call(
        _mla_prep_kernel,
        out_shape=(jax.ShapeDtypeStruct((t, N_MLA * 2 * LANES), BF16),
                   jax.ShapeDtypeStruct((t, N_MLA * 2 * LANES), BF16),
                   jax.ShapeDtypeStruct((t, D_MLA), BF16)),
        grid=(t // tm,),
        in_specs=[
            pl.BlockSpec((tm, Q_LORA), lambda i: (i, MISC_CQ // Q_LORA)),
            pl.BlockSpec((tm, KV_LORA), lambda i: (i, MISC_CKV // KV_LORA)),
            pl.BlockSpec((tm, LANES), lambda i: (i, MISC_KR1 // LANES)),
            pl.BlockSpec((tm, LANES), lambda i: (i, MISC_KR2 // LANES)),
            pl.BlockSpec((tm, 1), lambda i: (i, 0)),
            pl.BlockSpec((1, Q_LORA), const),
            pl.BlockSpec((1, KV_LORA), const),
            pl.BlockSpec(wq.shape, const),
            pl.BlockSpec(wkv.shape, const),
            pl.BlockSpec((1, LANES), const),
            pl.BlockSpec((1, LANES), const),
        ],
        out_specs=(pl.BlockSpec((tm, N_MLA * 2 * LANES), lambda i: (i, 0)),
                   pl.BlockSpec((tm, N_MLA * 2 * LANES), lambda i: (i, 0)),
                   pl.BlockSpec((tm, D_MLA), lambda i: (i, 0))),
        compiler_params=_params(("parallel",)),
        name="mla_prep",
    )(misc, misc, misc, misc, pos, gq.reshape(1, -1), gkv.reshape(1, -1), wq, wkv, invf, sgn)


def _gelu(x):
    return 0.5 * x * (1.0 + lax.erf(x * np.float32(np.sqrt(0.5))))


def _sgu_kernel(u_ref, v_ref, lng_ref, lnb_ref, w_ref, bias_ref, o_ref):
    tm = u_ref.shape[0]
    v = _gelu(v_ref[...])
    mu = jnp.mean(v, axis=-1, keepdims=True)
    xc = v - mu
    var = jnp.mean(xc * xc, axis=-1, keepdims=True)
    vb = (xc * lax.rsqrt(var + LN_EPS) * lng_ref[...] + lnb_ref[...]).astype(BF16)
    row = lax.broadcasted_iota(jnp.int32, (CHUNK, CHUNK), 0)
    col = lax.broadcasted_iota(jnp.int32, (CHUNK, LANES), 1)
    tril = lax.broadcasted_iota(jnp.int32, (CHUNK, CHUNK), 1) <= row
    first = col < SGU_CH
    ws = [jnp.where(tril, w_ref[g], 0.0).astype(BF16) for g in range(N_SGU)]
    zero = jnp.zeros((CHUNK, LANES), BF16)
    for c in range(tm // CHUNK):
        rows = slice(c * CHUNK, (c + 1) * CHUNK)
        for p in range(N_SGU // 2):
            cols = slice(p * LANES, (p + 1) * LANES)
            vp = vb[rows, cols]
            mixed = (jnp.dot(ws[2 * p], jnp.where(first, vp, zero), preferred_element_type=F32)
                     + jnp.dot(ws[2 * p + 1], jnp.where(first, zero, vp), preferred_element_type=F32))
            o_ref[rows, cols] = _gelu(u_ref[rows, cols]) * (mixed + bias_ref[:, cols])


def _sgu(misc, ln_g, ln_b, w_s, b_s, tm=512):
    t = misc.shape[0]
    bias = jnp.repeat(b_s.T, SGU_CH, axis=1)
    const2 = lambda i: (0, 0)
    return pl.pallas_call(
        _sgu_kernel,
        out_shape=jax.ShapeDtypeStruct((t, D_SGU), F32),
        grid=(t // tm,),
        in_specs=[
            pl.BlockSpec((tm, D_SGU), lambda i: (i, MISC_U // D_SGU)),
            pl.BlockSpec((tm, D_SGU), lambda i: (i, MISC_V // D_SGU)),
            pl.BlockSpec((1, D_SGU), const2),
            pl.BlockSpec((1, D_SGU), const2),
            pl.BlockSpec((N_SGU, CHUNK, CHUNK), lambda i: (0, 0, 0)),
            pl.BlockSpec((CHUNK, D_SGU), const2),
        ],
        out_specs=pl.BlockSpec((tm, D_SGU), lambda i: (i, 0)),
        compiler_params=_params(("parallel",)),
        name="sgu",
    )(misc, misc, ln_g.reshape(1, -1), ln_b.reshape(1, -1), w_s, bias)


def _pack_bf16_pairs(hb):
    n = hb.shape[1] // 2
    bits = pltpu.bitcast(hb.astype(F32), jnp.uint32)
    return (bits[:, n:] & jnp.uint32(0xFFFF0000)) | (bits[:, :n] >> 16)


def _unpack_bf16_pairs(xu):
    lo = pltpu.bitcast(xu << 16, F32).astype(BF16)
    hi = pltpu.bitcast(xu & jnp.uint32(0xFFFF0000), F32).astype(BF16)
    return jnp.concatenate([lo, hi], axis=1)


def _outproj_router_kernel(of_ref, os_ref, om_ref, og_ref, x_ref, ong_ref, wo_ref, nfg_ref, rw_ref,
                           rb_ref, xo_ref, h2_ref, idx_ref, gate_ref, rank_ref, cnt_ref, carry_ref):
    tm = x_ref.shape[0]
    th = tm // 2
    halves = [slice(h * th, (h + 1) * th) for h in range(2)]

    @pl.when(pl.program_id(0) == 0)
    def _():
        carry_ref[...] = jnp.zeros_like(carry_ref)

    os_ = []
    for rows in halves:
        o = jnp.concatenate([_rms(r[rows, :]) for r in (of_ref, os_ref, om_ref, og_ref)], axis=-1)
        os_.append((o * ong_ref[...]).astype(BF16))
    xns = [x_ref[rows, :] + jnp.dot(o, wo_ref[...], preferred_element_type=F32) for rows, o in zip(halves, os_)]
    hbs = []
    for rows, xn in zip(halves, xns):
        xo_ref[rows, :] = xn
        hb = (_rms(xn) * nfg_ref[...]).astype(BF16)
        h2_ref[rows, :] = _pack_bf16_pairs(hb)
        hbs.append(hb)

    lane = lax.broadcasted_iota(jnp.int32, (th, LANES), 1)
    all_logits = [jnp.dot(hb, rw_ref[...], preferred_element_type=F32) + rb_ref[...] for hb in hbs]
    picks = []
    for logits in all_logits:
        vals = jnp.where(lane < N_EXPERTS, logits, -jnp.inf)
        sels, tops = [], []
        for _ in range(TOP_K):
            m = jnp.max(vals, axis=-1, keepdims=True)
            idx = jnp.min(jnp.where(vals == m, lane, LANES), axis=-1, keepdims=True)
            sel = lane == idx
            vals = jnp.where(sel, -jnp.inf, vals)
            sels.append(sel)
            tops.append((m, idx))
        es = [jnp.exp(m - tops[0][0]) for (m, _) in tops]
        den = es[0] + es[1] + es[2] + es[3]
        mh = jnp.where(sels[0] | sels[1] | sels[2] | sels[3], 1.0, 0.0)
        picks.append((sels, tops, es, den, mh))

    r = lax.broadcasted_iota(jnp.int32, (th, th), 0)
    c = lax.broadcasted_iota(jnp.int32, (th, th), 1)
    before = (c < r).astype(BF16)
    parts = [jnp.dot(before, p[4].astype(BF16), preferred_element_type=F32) for p in picks]
    total = carry_ref[...]
    for rows, part, (sels, tops, es, den, mh) in zip(halves, parts, picks):
        cnt = part + total
        total = cnt[th - 1:th, :] + mh[th - 1:th, :]
        idx_o = jnp.zeros((th, LANES), jnp.int32)
        gate_o = jnp.zeros((th, LANES), F32)
        rank_o = jnp.zeros((th, LANES), jnp.int32)
        for k in range(TOP_K):
            rank_k = jnp.sum(jnp.where(sels[k], cnt, 0.0), axis=-1, keepdims=True).astype(jnp.int32)
            idx_o = jnp.where(lane == k, tops[k][1], idx_o)
            gate_o = jnp.where(lane == k, es[k] / den, gate_o)
            rank_o = jnp.where(lane == k, rank_k, rank_o)
        idx_ref[rows, :] = idx_o
        gate_ref[rows, :] = gate_o
        rank_ref[rows, :] = rank_o
    carry_ref[...] = total
    cnt_ref[...] = jnp.broadcast_to(total, cnt_ref.shape).astype(jnp.int32)


def _outproj_router(o_fox, o_sb, o_mla, o_sgu, x, out_norm_g, w_o, norm_ffn_g, router_w, router_b, tm=512):
    t, d = x.shape
    dg = o_fox.shape[1]
    rw = jnp.zeros((d, LANES), BF16).at[:, :N_EXPERTS].set(router_w.astype(BF16))
    rb = jnp.zeros((1, LANES), F32).at[0, :N_EXPERTS].set(router_b)
    const = lambda i: (0, 0)
    grp = pl.BlockSpec((tm, dg), lambda i: (i, 0))
    lanes_out = pl.BlockSpec((tm, LANES), lambda i: (i, 0))
    return pl.pallas_call(
        _outproj_router_kernel,
        out_shape=(jax.ShapeDtypeStruct((t, d), F32),
                   jax.ShapeDtypeStruct((t, d // 2), jnp.uint32),
                   jax.ShapeDtypeStruct((t, LANES), jnp.int32),
                   jax.ShapeDtypeStruct((t, LANES), F32),
                   jax.ShapeDtypeStruct((t, LANES), jnp.int32),
                   jax.ShapeDtypeStruct((8, LANES), jnp.int32)),
        grid=(t // tm,),
        in_specs=[grp, grp, grp, grp,
                  pl.BlockSpec((tm, d), lambda i: (i, 0)),
                  pl.BlockSpec((1, 4 * dg), const),
                  pl.BlockSpec((4 * dg, d), const),
                  pl.BlockSpec((1, d), const),
                  pl.BlockSpec((d, LANES), const),
                  pl.BlockSpec((1, LANES), const)],
        out_specs=(pl.BlockSpec((tm, d), lambda i: (i, 0)),
                   pl.BlockSpec((tm, d // 2), lambda i: (i, 0)),
                   lanes_out, lanes_out, lanes_out,
                   pl.BlockSpec((8, LANES), const)),
        scratch_shapes=[pltpu.VMEM((1, LANES), F32)],
        compiler_params=_params(("arbitrary",)),
        name="outproj_router",
    )(o_fox, o_sb, o_mla, o_sgu, x, out_norm_g.reshape(1, -1), w_o, norm_ffn_g.reshape(1, -1), rw, rb)


def _scatter_rows_kernel(dest_hbm, x_ref, xs_init_hbm, xs_hbm, dest0, dest1, sems):
    del xs_init_hbm
    dest_smem = (dest0, dest1)
    i = pl.program_id(0)
    nsteps = pl.num_programs(0)
    tm = x_ref.shape[0]
    n = tm * TOP_K

    def idx_copy(step, sl):
        return pltpu.make_async_copy(dest_hbm.at[pl.ds(step * n, n)], dest_smem[sl], sems.at[sl])

    @pl.when(i == 0)
    def _():
        idx_copy(0, 0).start()

    for sl in range(2):
        @pl.when(i % 2 == sl)
        def _(sl=sl):
            @pl.when(i + 1 < nsteps)
            def _():
                idx_copy(i + 1, 1 - sl).start()

            idx_copy(i, sl).wait()

            def issue(t, _):
                for k in range(TOP_K):
                    pltpu.make_async_copy(x_ref.at[pl.ds(t, 1)],
                                          xs_hbm.at[pl.ds(dest_smem[sl][t * TOP_K + k], 1)], sems.at[2]).start()
                return 0

            lax.fori_loop(0, tm, issue, 0, unroll=2)

    for _ in range(TOP_K):
        pltpu.make_async_copy(x_ref, xs_hbm.at[pl.ds(0, tm)], sems.at[2]).wait()


def _scatter_rows(dest_flat, src, nrows, xs_init):
    t, w = src.shape
    tm = SCATTER_TM
    assert xs_init.shape == (nrows, w) and xs_init.dtype == src.dtype
    return pl.pallas_call(
        _scatter_rows_kernel,
        out_shape=jax.ShapeDtypeStruct((nrows, w), src.dtype),
        grid=(t // tm,),
        in_specs=[pl.BlockSpec(memory_space=pl.ANY),
                  pl.BlockSpec((tm, w), lambda i: (i, 0)),
                  pl.BlockSpec(memory_space=pl.ANY)],
        out_specs=pl.BlockSpec(memory_space=pl.ANY),
        scratch_shapes=[pltpu.SMEM((tm * TOP_K,), jnp.int32), pltpu.SMEM((tm * TOP_K,), jnp.int32),
                        pltpu.SemaphoreType.DMA((3,))],
        input_output_aliases={2: 0},
        compiler_params=_params(("arbitrary",)),
        name="scatter_rows",
    )(dest_flat, src, xs_init)


def _expert_kernel(sbe_ref, nval_ref, nused_ref, x_ref, wg_ref, bg_ref, wu_ref, bu_ref, wd_ref, bd_ref,
                   y_init_hbm, y_ref):
    del y_init_hbm
    s = pl.program_id(0)
    f = pl.program_id(1)
    d = y_ref.shape[1]

    @pl.when(s < nused_ref[0])
    def _():
        @pl.when(f == 0)
        def _():
            y_ref[...] = jnp.broadcast_to(bd_ref[...], y_ref.shape)

        nsub = (nval_ref[s] + EXP_SUB - 1) // EXP_SUB

        for n in range(1, EXP_ROWS // EXP_SUB + 1):
            @pl.when(nsub == n)
            def _(m=n * EXP_SUB):
                x = _unpack_bf16_pairs(x_ref[0:m, :])
                g = jnp.dot(x, wg_ref[...].astype(BF16), preferred_element_type=F32) + bg_ref[...]
                g = jnp.minimum(g, SWIGLU_LIMIT)
                u = jnp.dot(x, wu_ref[...].astype(BF16), preferred_element_type=F32) + bu_ref[...]
                u = jnp.clip(u, -SWIGLU_LIMIT, SWIGLU_LIMIT)
                a = (g * jax.nn.sigmoid(SWIGLU_ALPHA * g) * (u + 1.0)).astype(BF16)
                wd = wd_ref[...].astype(BF16)
                for c in range(d // EXP_DN):
                    cols = slice(c * EXP_DN, (c + 1) * EXP_DN)
                    y_ref[0:m, cols] += jnp.dot(a, wd[:, cols], preferred_element_type=F32)


def _experts(sb_expert, sb_nvalid, n_used, xs, layer, w_gate, b_gate, w_up, b_up, w_down, b_down, nsb_max, y_init):
    d = w_gate.shape[2]
    dff = w_gate.shape[3]
    assert d % EXP_DN == 0 and dff % EXP_FF == 0 and EXP_ROWS % EXP_SUB == 0
    assert y_init.shape == (nsb_max * EXP_ROWS, d) and y_init.dtype == F32
    nf = dff // EXP_FF

    def xmap(s, f, sbe, nval, nused):
        return (jnp.minimum(s, nused[0] - 1), 0)

    def ff(s, f, nused):
        return jnp.where(s < nused[0], f, nf - 1)

    b_gate4 = b_gate.reshape(b_gate.shape[0], N_EXPERTS, 1, dff)
    b_up4 = b_up.reshape(b_up.shape[0], N_EXPERTS, 1, dff)
    b_down4 = b_down.reshape(b_down.shape[0], N_EXPERTS, 1, d)
    y_init_operand = 3 + 7
    return pl.pallas_call(
        _expert_kernel,
        out_shape=jax.ShapeDtypeStruct((nsb_max * EXP_ROWS, d), F32),
        grid_spec=pltpu.PrefetchScalarGridSpec(
            num_scalar_prefetch=3,
            grid=(nsb_max, nf),
            in_specs=[
                pl.BlockSpec((EXP_ROWS, d // 2), xmap),
                pl.BlockSpec((None, None, d, EXP_FF), lambda s, f, sbe, nval, nused: (layer, sbe[s], 0, ff(s, f, nused))),
                pl.BlockSpec((None, None, 1, EXP_FF), lambda s, f, sbe, nval, nused: (layer, sbe[s], 0, ff(s, f, nused))),
                pl.BlockSpec((None, None, d, EXP_FF), lambda s, f, sbe, nval, nused: (layer, sbe[s], 0, ff(s, f, nused))),
                pl.BlockSpec((None, None, 1, EXP_FF), lambda s, f, sbe, nval, nused: (layer, sbe[s], 0, ff(s, f, nused))),
                pl.BlockSpec((None, None, EXP_FF, d), lambda s, f, sbe, nval, nused: (layer, sbe[s], ff(s, f, nused), 0)),
                pl.BlockSpec((None, None, 1, d), lambda s, f, sbe, nval, nused: (layer, sbe[s], 0, 0)),
                pl.BlockSpec(memory_space=pl.ANY),
            ],
            out_specs=pl.BlockSpec((EXP_ROWS, d), xmap),
        ),
        input_output_aliases={y_init_operand: 0},
        compiler_params=_params(("arbitrary", "arbitrary")),
        name="experts",
    )(sb_expert, sb_nvalid, n_used, xs, w_gate, b_gate4, w_up, b_up4, w_down, b_down4, y_init)


def _combine_kernel(dest_hbm, y_hbm, x_ref, gate_ref, fg_ref, o_ref, dest0, dest1, ybuf0, ybuf1, sems, *,
                    final_norm):
    dest_smem = (dest0, dest1)
    ybuf = (ybuf0, ybuf1)
    i = pl.program_id(0)
    nsteps = pl.num_programs(0)
    tm = x_ref.shape[0]
    n = tm * TOP_K
    slot = i % 2
    nslot = 1 - slot

    def idx_copy(step, sl):
        return pltpu.make_async_copy(dest_hbm.at[pl.ds(step * n, n)], dest_smem[sl], sems.at[sl])

    def issue_rows(sl):
        def issue(t, _):
            for k in range(TOP_K):
                pltpu.make_async_copy(y_hbm.at[pl.ds(dest_smem[sl][t * TOP_K + k], 1)],
                                      ybuf[sl].at[pl.ds(k * tm + t, 1)], sems.at[2 + sl]).start()
            return 0

        lax.fori_loop(0, tm, issue, 0, unroll=2)

    @pl.when(i == 0)
    def _():
        first = idx_copy(0, 0)
        first.start()
        first.wait()
        issue_rows(0)

        @pl.when(nsteps > 1)
        def _():
            idx_copy(1, 1).start()

    for sl in range(2):
        @pl.when(jnp.logical_and(i + 1 < nsteps, nslot == sl))
        def _(sl=sl):
            idx_copy(i + 1, sl).wait()
            issue_rows(sl)

    for sl in range(2):
        @pl.when(slot == sl)
        def _(sl=sl):
            @pl.when(i + 2 < nsteps)
            def _():
                idx_copy(i + 2, sl).start()

            pltpu.make_async_copy(y_hbm.at[pl.ds(0, n)], ybuf[sl], sems.at[2 + sl]).wait()
            gate = gate_ref[...]
            acc = gate[:, 0:1] * ybuf[sl][0:tm]
            for k in range(1, TOP_K):
                acc = acc + gate[:, k:k + 1] * ybuf[sl][k * tm:(k + 1) * tm]
            out = x_ref[...] + acc
            if final_norm:
                out = _rms(out) * fg_ref[...]
            o_ref[...] = out


def _combine(dest_flat, y, x, gates, final_g, final_norm):
    t, d = x.shape
    tm = COMBINE_TM
    return pl.pallas_call(
        functools.partial(_combine_kernel, final_norm=final_norm),
        out_shape=jax.ShapeDtypeStruct((t, d), F32),
        grid=(t // tm,),
        in_specs=[
            pl.BlockSpec(memory_space=pl.ANY),
            pl.BlockSpec(memory_space=pl.ANY),
            pl.BlockSpec((tm, d), lambda i: (i, 0)),
            pl.BlockSpec((tm, LANES), lambda i: (i, 0)),
            pl.BlockSpec((1, d), lambda i: (0, 0)),
        ],
        out_specs=pl.BlockSpec((tm, d), lambda i: (i, 0)),
        scratch_shapes=[pltpu.SMEM((tm * TOP_K,), jnp.int32), pltpu.SMEM((tm * TOP_K,), jnp.int32),
                        pltpu.VMEM((TOP_K * tm, d), F32), pltpu.VMEM((TOP_K * tm, d), F32),
                        pltpu.SemaphoreType.DMA((4,))],
        compiler_params=_params(("arbitrary",)),
        name="combine",
    )(dest_flat, y, x, gates, final_g.reshape(1, d))


def _in_proj_weights(w_in):
    d = w_in.shape[0]
    o_gate = 3 * D_FOX
    o_sb = o_gate + N_FOX
    o_cq = o_sb + 3 * D_SB
    o_ckv = o_cq + Q_LORA
    o_kr = o_ckv + KV_LORA
    o_sgu = o_kr + MLA_ROPE
    half = MLA_ROPE // 2
    w_qkv = jnp.concatenate([w_in[:, :o_gate], w_in[:, o_sb:o_cq]], axis=1).astype(BF16)
    kr = w_in[:, o_kr:o_sgu]
    zpad = jnp.zeros((d, LANES - MLA_ROPE), w_in.dtype)
    w_misc = jnp.concatenate([
        w_in[:, o_sgu:o_sgu + 2 * D_SGU],
        w_in[:, o_cq:o_ckv],
        w_in[:, o_ckv:o_kr],
        kr, zpad,
        kr[:, half:], kr[:, :half], zpad,
        w_in[:, o_gate:o_sb], jnp.zeros((d, LANES - N_FOX), w_in.dtype),
    ], axis=1).astype(BF16)
    assert w_misc.shape[1] == MISC_W
    return w_qkv, w_misc


def _mla_weights(w_q_b, w_kv_b):
    half = MLA_ROPE // 2
    wq = w_q_b.reshape(Q_LORA, N_MLA, MLA_NOPE + MLA_ROPE)
    x1 = wq[:, :, MLA_NOPE:MLA_NOPE + half]
    x2 = wq[:, :, MLA_NOPE + half:]
    z = jnp.zeros((Q_LORA, N_MLA, LANES - MLA_ROPE), w_q_b.dtype)
    wq_r = jnp.concatenate([wq[:, :, :MLA_NOPE], x1, x2, z, x2, x1, z], axis=2)
    wq_r = wq_r.reshape(Q_LORA, N_MLA * 3 * LANES).astype(BF16)
    wkv = w_kv_b.reshape(KV_LORA, N_MLA, MLA_NOPE + MLA_V)
    wkv_r = jnp.concatenate([wkv[:, :, :MLA_NOPE].reshape(KV_LORA, -1),
                             wkv[:, :, MLA_NOPE:].reshape(KV_LORA, -1)], axis=1).astype(BF16)
    return wq_r, wkv_r


def _routing_tables(idx, rank, cnt, n_tokens):
    nsb_max = N_EXPERTS + (n_tokens * TOP_K) // EXP_ROWS
    counts = cnt[0, :N_EXPERTS]
    nsb_e = (counts + EXP_ROWS - 1) // EXP_ROWS
    sb_end = jnp.cumsum(nsb_e)
    sb_start = sb_end - nsb_e
    n_used = sb_end[-1]
    top_idx = idx[:, :TOP_K].reshape(-1)
    dest = (sb_start * EXP_ROWS)[top_idx] + rank[:, :TOP_K].reshape(-1)
    s_ids = jnp.arange(nsb_max, dtype=jnp.int32)
    sb_e = jnp.minimum(jnp.sum(s_ids[:, None] >= sb_end[None, :], axis=1), N_EXPERTS - 1).astype(jnp.int32)
    last_e = sb_e[jnp.maximum(n_used - 1, 0)]
    sb_e = jnp.where(s_ids < n_used, sb_e, last_e)
    nval = jnp.clip(counts[sb_e] - (s_ids - sb_start[sb_e]) * EXP_ROWS, 0, EXP_ROWS)
    nval = jnp.where(s_ids < n_used, nval, 0).astype(jnp.int32)
    return dest.astype(jnp.int32), sb_e, nval, n_used.reshape(1).astype(jnp.int32), nsb_max


def _layer(x, pos, l, p, final_g, final_norm, batch, seq, moe_bufs):
    t = x.shape[0]
    w_qkv, w_misc = _in_proj_weights(p["w_in"][l])
    qkv = _norm_matmul(x, p["norm_mix_g"][l], w_qkv, BF16, 512, 1536, "in_proj_qkv")
    misc = _norm_matmul(x, p["norm_mix_g"][l], w_misc, F32, 512, MISC_W, "in_proj_misc")

    crow = _fox_prep(misc, p["b_forget"][l], batch, seq)
    nkb = seq // ATT_TILE
    crow = crow.reshape(batch, nkb, N_FOX // FOX_HEADS, FOX_HEADS, ATT_TILE).transpose(0, 2, 1, 3, 4)
    if moe_bufs is None:
        nrows = (N_EXPERTS + (t * TOP_K) // EXP_ROWS) * EXP_ROWS
        o_fox, xs_prev = _fox_attention(qkv, crow, batch, seq,
                                        jax.ShapeDtypeStruct((nrows, x.shape[1] // 2), jnp.uint32))
        o_sb, y_prev = _sb_attention(qkv, batch, seq, jax.ShapeDtypeStruct((nrows, x.shape[1]), F32))
    else:
        xs_prev, y_prev = moe_bufs
        o_fox = _fox_attention(qkv, crow, batch, seq)
        o_sb = _sb_attention(qkv, batch, seq)
    wq_r, wkv_r = _mla_weights(p["mla_w_q_b"][l], p["mla_w_kv_b"][l])
    q_m, k_m, v_m = _mla_prep(misc, pos, p["mla_q_norm_g"][l], p["mla_kv_norm_g"][l], wq_r, wkv_r)
    o_mla = _mla_attention(q_m, k_m, v_m, batch, seq)
    o_sgu = _sgu(misc, p["sgu_ln_g"][l], p["sgu_ln_b"][l], p["sgu_w_s"][l], p["sgu_b_s"][l])

    x_new, h2p, idx, gates, rank, cnt = _outproj_router(
        o_fox, o_sb, o_mla, o_sgu, x, p["out_norm_g"][l], p["w_o"][l].astype(BF16),
        p["norm_ffn_g"][l], p["router_w"][l], p["router_b"][l])

    dest, sb_e, nval, n_used, nsb_max = _routing_tables(idx, rank, cnt, t)
    xs = _scatter_rows(dest, h2p, nsb_max * EXP_ROWS, xs_prev)
    y = _experts(sb_e, nval, n_used, xs, l, p["w_gate"], p["b_gate"], p["w_up"], p["b_up"],
                 p["w_down"], p["b_down"], nsb_max, y_prev)
    return _combine(dest, y, x_new, gates, final_g, final_norm), (xs, y)


def kernel(x, positions, norm_mix_g, w_in, b_forget, mla_q_norm_g, mla_kv_norm_g, mla_w_q_b, mla_w_kv_b,
           sgu_ln_g, sgu_ln_b, sgu_w_s, sgu_b_s, out_norm_g, w_o, norm_ffn_g, router_w, router_b, w_gate,
           b_gate, w_up, b_up, w_down, b_down, final_norm_g):
    batch, seq, d = x.shape
    depth = w_in.shape[0]
    p = dict(norm_mix_g=norm_mix_g, w_in=w_in, b_forget=b_forget, mla_q_norm_g=mla_q_norm_g,
             mla_kv_norm_g=mla_kv_norm_g, mla_w_q_b=mla_w_q_b, mla_w_kv_b=mla_w_kv_b, sgu_ln_g=sgu_ln_g,
             sgu_ln_b=sgu_ln_b, sgu_w_s=sgu_w_s, sgu_b_s=sgu_b_s, out_norm_g=out_norm_g, w_o=w_o,
             norm_ffn_g=norm_ffn_g, router_w=router_w, router_b=router_b, w_gate=w_gate, b_gate=b_gate,
             w_up=w_up, b_up=b_up, w_down=w_down, b_down=b_down)
    h = x.reshape(batch * seq, d)
    pos = positions.reshape(batch * seq, 1)
    moe_bufs = None
    for l in range(depth):
        h, moe_bufs = _layer(h, pos, l, p, final_norm_g, l == depth - 1, batch, seq, moe_bufs)
    return h.reshape(batch, seq, d)
```

```python
import functools

import numpy as np
import jax
import jax.numpy as jnp
from jax import lax
from jax.experimental import pallas as pl
from jax.experimental.pallas import tpu as pltpu

HEAD_DIM = 64
N_FOX = 8
N_SB = 8
N_MLA = 4
MLA_NOPE = 128
MLA_ROPE = 64
MLA_V = 128
Q_LORA = 512
KV_LORA = 256
N_SGU = 8
SGU_CH = 64
CHUNK = 128
ROPE_THETA = 10000.0
D_FOX = N_FOX * HEAD_DIM
D_SB = N_SB * HEAD_DIM
D_MLA = N_MLA * MLA_V
D_SGU = N_SGU * SGU_CH
N_EXPERTS = 32
TOP_K = 4
SWIGLU_LIMIT = 7.0
SWIGLU_ALPHA = 1.702
RMS_EPS = 1e-6
LN_EPS = 1e-5

LANES = 128
VMEM_LIMIT = 56 * 1024 * 1024

ATT_TILE = 256
FOX_HEADS = 8
MLA_HEADS = 4
SB_HEADS = 8
EXP_ROWS = 1536
EXP_SUB = 256
EXP_FF = 256
EXP_DN = 512
SCATTER_TM = 512
ZERO_ROWS = 512
COMBINE_TM = 256

F32 = jnp.float32
BF16 = jnp.bfloat16

MISC_U = 0
MISC_V = 512
MISC_CQ = 1024
MISC_CKV = 1536
MISC_KR1 = 1792
MISC_KR2 = 1920
MISC_GATE = 2048
MISC_W = 2176


def _params(sem, vmem=VMEM_LIMIT):
    return pltpu.CompilerParams(dimension_semantics=sem, vmem_limit_bytes=vmem)


def _split3(x):
    hi = x.astype(BF16)
    r1 = x - hi.astype(F32)
    mid = r1.astype(BF16)
    lo = (r1 - mid.astype(F32)).astype(BF16)
    return hi, mid, lo


def _log_sigmoid_pair(z):
    t = jnp.log1p(jnp.exp(-jnp.abs(z)))
    return jnp.minimum(z, 0.0) - t, jnp.minimum(-z, 0.0) - t


def _rms(x):
    return x * lax.rsqrt(jnp.mean(x * x, axis=-1, keepdims=True) + RMS_EPS)


def _norm_matmul_kernel(x_ref, g_ref, w_ref, o_ref, xn_ref):
    tm = x_ref.shape[0]

    @pl.when(pl.program_id(1) == 0)
    def _():
        for rows in (slice(0, tm // 2), slice(tm // 2, tm)):
            xn = (_rms(x_ref[rows, :]) * g_ref[...]).astype(BF16)
            xn_ref[rows, :] = xn
            o_ref[rows, :] = jnp.dot(xn, w_ref[...], preferred_element_type=F32).astype(o_ref.dtype)

    @pl.when(pl.program_id(1) != 0)
    def _():
        o_ref[...] = jnp.dot(xn_ref[...], w_ref[...], preferred_element_type=F32).astype(o_ref.dtype)


def _norm_matmul(x, g, w, out_dtype, tm, tn, name):
    t, d = x.shape
    n = w.shape[1]
    return pl.pallas_call(
        _norm_matmul_kernel,
        out_shape=jax.ShapeDtypeStruct((t, n), out_dtype),
        grid=(t // tm, n // tn),
        in_specs=[
            pl.BlockSpec((tm, d), lambda i, j: (i, 0)),
            pl.BlockSpec((1, d), lambda i, j: (0, 0)),
            pl.BlockSpec((d, tn), lambda i, j: (0, j)),
        ],
        out_specs=pl.BlockSpec((tm, tn), lambda i, j: (i, j)),
        scratch_shapes=[pltpu.VMEM((tm, d), BF16)],
        compiler_params=_params(("parallel", "arbitrary")),
        name=name,
    )(x, g.reshape(1, d), w)


def _fox_prep_kernel(gate_ref, bf_ref, crow_ref):
    s_len = gate_ref.shape[0]
    bl = ATT_TILE
    r = lax.broadcasted_iota(jnp.int32, (bl, bl), 0)
    c = lax.broadcasted_iota(jnp.int32, (bl, bl), 1)
    tri = (c <= r).astype(BF16)
    carry = jnp.zeros((1, LANES), F32)
    for i in range(s_len // bl):
        z = gate_ref[i * bl:(i + 1) * bl, :] + bf_ref[...]
        lf, _ = _log_sigmoid_pair(z)
        hi, mid, lo = _split3(lf)
        cs = (jnp.dot(tri, hi, preferred_element_type=F32)
              + jnp.dot(tri, mid, preferred_element_type=F32)
              + jnp.dot(tri, lo, preferred_element_type=F32)) + carry
        crow_ref[i] = cs.T[0:N_FOX, :]
        carry = cs[bl - 1:bl, :]


def _fox_prep(misc, b_forget, batch, seq):
    nkb = seq // ATT_TILE
    bf = jnp.zeros((1, LANES), F32).at[0, :N_FOX].set(b_forget)
    return pl.pallas_call(
        _fox_prep_kernel,
        out_shape=jax.ShapeDtypeStruct((batch, nkb, N_FOX, ATT_TILE), F32),
        grid=(batch,),
        in_specs=[
            pl.BlockSpec((seq, LANES), lambda b: (b, MISC_GATE // LANES)),
            pl.BlockSpec((1, LANES), lambda b: (0, 0)),
        ],
        out_specs=pl.BlockSpec((None, nkb, N_FOX, ATT_TILE), lambda b: (b, 0, 0, 0)),
        compiler_params=_params(("parallel",)),
        name="fox_prep",
    )(misc, bf)


def _causal_mask(tq):
    row = lax.broadcasted_iota(jnp.int32, (tq, tq), 0)
    col = lax.broadcasted_iota(jnp.int32, (tq, tq), 1)
    return col <= row, col < row


def _softmax_sweep(qs, k_of, v_of, bias_of, scale, qi, tq, ones_lane=None):
    nh = len(qs)
    incl, _ = _causal_mask(tq)

    def step(kb, carry, diag):
        ks = pl.multiple_of(kb * tq, tq)
        ss = [lax.dot_general(qs[h], k_of(h, ks), (((1,), (1,)), ((), ())), preferred_element_type=F32)
              for h in range(nh)]
        ps, stats = [], []
        for h in range(nh):
            m, l, _ = carry[h]
            s = ss[h]
            if scale is not None:
                s = s * scale
            b = bias_of(h, kb)
            if b is not None:
                s = s + b
            if diag:
                s = jnp.where(incl, s, -jnp.inf)
            m_new = jnp.maximum(m, jnp.max(s, axis=-1, keepdims=True))
            alpha = jnp.exp(m - m_new)
            p = jnp.exp(s - m_new)
            if ones_lane is None:
                l = alpha * l + jnp.sum(p, axis=-1, keepdims=True)
            stats.append((m_new, l, alpha))
            ps.append(p.astype(BF16))
        out = []
        for h in range(nh):
            m_new, l, alpha = stats[h]
            acc = alpha * carry[h][2] + jnp.dot(ps[h], v_of(h, ks), preferred_element_type=F32)
            out.append((m_new, l, acc))
        return tuple(out)

    dv = v_of(0, 0).shape[-1]
    l0 = jnp.zeros((tq, 1), F32) if ones_lane is None else None
    init = tuple((jnp.full((tq, 1), -jnp.inf, F32), l0, jnp.zeros((tq, dv), F32)) for _ in range(nh))
    carry = lax.fori_loop(0, qi, lambda kb, c: step(kb, c, False), init)
    carry = step(qi, carry, True)
    if ones_lane is None:
        return [acc / l for (_, l, acc) in carry]
    return [acc / acc[:, ones_lane(h):ones_lane(h) + 1] for h, (_, _, acc) in enumerate(carry)]


def _zero_fill_copies(fill_hbm, zeros, sem, grid):
    nq = grid[2]
    ngroups = grid[0] * grid[1]
    rows = zeros.shape[0]
    nblocks = fill_hbm.shape[0] // rows
    per_group = (nblocks + ngroups - 1) // ngroups
    work = nq * (nq + 1) // 2
    max_per_step = (per_group * nq + work - 1) // work + 1
    group = pl.program_id(0) * grid[1] + pl.program_id(1)
    qi = pl.program_id(2)
    first = per_group * (qi * (qi + 1) // 2) // work
    count = per_group * ((qi + 1) * (qi + 2) // 2) // work - first
    copies = []
    for j in range(max_per_step):
        block = group * per_group + first + j
        valid = jnp.logical_and(j < count, block < nblocks)
        at = jnp.minimum(block, nblocks - 1) * rows
        copies.append((valid, pltpu.make_async_copy(zeros, fill_hbm.at[pl.ds(at, rows)], sem)))
    return copies


def _zero_fill_start(fill, grid):
    if not fill:
        return []
    fill_hbm, zeros, sem = fill
    zeros[...] = jnp.zeros_like(zeros)
    copies = _zero_fill_copies(fill_hbm, zeros, sem, grid)
    for valid, cp in copies:
        @pl.when(valid)
        def _(cp=cp):
            cp.start()
    return copies


def _zero_fill_wait(copies):
    for valid, cp in copies:
        @pl.when(valid)
        def _(cp=cp):
            cp.wait()


def _zero_fill_specs(fill):
    if fill is None:
        return [], [], []
    assert fill.shape[0] % ZERO_ROWS == 0
    return ([fill], [pl.BlockSpec(memory_space=pl.ANY)],
            [pltpu.VMEM((ZERO_ROWS, fill.shape[1]), fill.dtype), pltpu.SemaphoreType.DMA(())])


def _fox_attn_kernel(q_ref, k_ref, v_ref, crow_ref, o_ref, *fill, grid):
    copies = _zero_fill_start(fill, grid)
    tq = q_ref.shape[0]
    qi = pl.program_id(2)
    q = q_ref[...] * jnp.asarray(HEAD_DIM ** -0.5, BF16)
    qs = [q[:, j * HEAD_DIM:(j + 1) * HEAD_DIM] for j in range(FOX_HEADS)]

    def k_of(h, ks):
        return k_ref[pl.ds(ks, tq), h * HEAD_DIM:(h + 1) * HEAD_DIM]

    first = lax.broadcasted_iota(jnp.int32, (tq, LANES), 1) < HEAD_DIM
    one = jnp.ones((tq, LANES), BF16)

    def v_of(h, ks):
        pair = v_ref[pl.ds(ks, tq), (h // 2) * LANES:(h // 2 + 1) * LANES]
        return jnp.where(first, pair, one) if h % 2 == 0 else jnp.where(first, one, pair)

    def bias_of(h, kb):
        return -crow_ref[kb][h:h + 1, :]

    outs = _softmax_sweep(qs, k_of, v_of, bias_of, None, qi, tq,
                          ones_lane=lambda h: HEAD_DIM if h % 2 == 0 else 0)
    o_ref[...] = jnp.concatenate([jnp.where(first, outs[h], outs[h + 1]) for h in range(0, FOX_HEADS, 2)],
                                 axis=-1)
    _zero_fill_wait(copies)


def _fox_attention(qkv, crow, batch, seq, fill=None):
    t = qkv.shape[0]
    tq = ATT_TILE
    nq = seq // tq
    npair = N_FOX // FOX_HEADS
    w = FOX_HEADS * HEAD_DIM
    grid = (batch, npair, nq)
    fill_shape, fill_specs, fill_scratch = _zero_fill_specs(fill)
    outs = pl.pallas_call(
        functools.partial(_fox_attn_kernel, grid=grid),
        out_shape=tuple([jax.ShapeDtypeStruct((t, D_FOX), F32)] + fill_shape),
        grid=grid,
        in_specs=[
            pl.BlockSpec((tq, w), lambda b, h, i: (b * nq + i, h)),
            pl.BlockSpec((seq, w), lambda b, h, i: (b, npair + h)),
            pl.BlockSpec((seq, w), lambda b, h, i: (b, 2 * npair + h)),
            pl.BlockSpec((None, None, nq, FOX_HEADS, tq), lambda b, h, i: (b, h, 0, 0, 0)),
        ],
        out_specs=tuple([pl.BlockSpec((tq, w), lambda b, h, i: (b * nq + i, h))] + fill_specs),
        scratch_shapes=fill_scratch,
        compiler_params=_params(("parallel", "parallel", "arbitrary")),
        name="fox_attn",
    )(qkv, qkv, qkv, crow)
    return outs if fill is not None else outs[0]


def _mla_attn_kernel(q_ref, k_ref, v_ref, o_ref):
    tq = q_ref.shape[0]
    qi = pl.program_id(2)
    dk = 2 * LANES
    scale = (MLA_NOPE + MLA_ROPE) ** -0.5
    qs = [q_ref[:, h * dk:(h + 1) * dk] for h in range(MLA_HEADS)]
    outs = _softmax_sweep(qs,
                          lambda h, ks: k_ref[pl.ds(ks, tq), h * dk:(h + 1) * dk],
                          lambda h, ks: v_ref[pl.ds(ks, tq), h * MLA_V:(h + 1) * MLA_V],
                          lambda h, kb: None, scale, qi, tq)
    o_ref[...] = jnp.concatenate(outs, axis=-1)


def _mla_attention(q, k, v, batch, seq):
    t = q.shape[0]
    tq = ATT_TILE
    nq = seq // tq
    dk = MLA_HEADS * 2 * LANES
    dv = MLA_HEADS * MLA_V
    return pl.pallas_call(
        _mla_attn_kernel,
        out_shape=jax.ShapeDtypeStruct((t, D_MLA), F32),
        grid=(batch, N_MLA // MLA_HEADS, nq),
        in_specs=[
            pl.BlockSpec((tq, dk), lambda b, h, i: (b * nq + i, h)),
            pl.BlockSpec((seq, dk), lambda b, h, i: (b, h)),
            pl.BlockSpec((seq, dv), lambda b, h, i: (b, h)),
        ],
        out_specs=pl.BlockSpec((tq, dv), lambda b, h, i: (b * nq + i, h)),
        compiler_params=_params(("parallel", "parallel", "arbitrary")),
        name="mla_attn",
    )(q, k, v)


def _sb_attn_kernel(q_ref, k_ref, v_ref, o_ref, *fill, grid):
    copies = _zero_fill_start(fill, grid)
    tq = q_ref.shape[0]
    qi = pl.program_id(2)
    q = q_ref[...] * jnp.asarray(HEAD_DIM ** -0.5, BF16)
    nh = SB_HEADS
    qs = [q[:, j * HEAD_DIM:(j + 1) * HEAD_DIM] for j in range(nh)]
    _, strict = _causal_mask(tq)
    row = lax.broadcasted_iota(jnp.int32, (tq, tq), 0)
    col = lax.broadcasted_iota(jnp.int32, (tq, tq), 1)
    later = (row > col).astype(BF16)

    def step(kb, carry, diag):
        ks = pl.multiple_of(kb * tq, tq)
        zs = [lax.dot_general(qs[h], k_ref[pl.ds(ks, tq), h * HEAD_DIM:(h + 1) * HEAD_DIM],
                              (((1,), (1,)), ((), ())), preferred_element_type=F32) for h in range(nh)]
        lss, lrs = [], []
        for h in range(nh):
            z = zs[h]
            ls = jnp.minimum(z, 0.0) - jnp.log(1.0 + jnp.exp(-jnp.abs(z)))
            lr = ls - z
            if diag:
                lr = jnp.where(strict, lr, 0.0)
            lss.append(ls)
            lrs.append(lr)
        afters = []
        for h in range(nh):
            hi = lrs[h].astype(BF16)
            lo = (lrs[h] - hi.astype(F32)).astype(BF16)
            afters.append((jnp.dot(hi, later, preferred_element_type=F32)
                           + jnp.dot(lo, later, preferred_element_type=F32)) + carry[h][0])
        out = []
        for h in range(nh):
            a = jnp.exp(lss[h] + afters[h])
            if diag:
                a = jnp.where(strict, a, 0.0)
            v = v_ref[pl.ds(ks, tq), h * HEAD_DIM:(h + 1) * HEAD_DIM]
            acc = carry[h][1] + jnp.dot(a.astype(BF16), v, preferred_element_type=F32)
            out.append((afters[h][:, 0:1] + lrs[h][:, 0:1], acc))
        return tuple(out)

    init = tuple((jnp.zeros((tq, 1), F32), jnp.zeros((tq, HEAD_DIM), F32)) for _ in range(nh))
    carry = step(qi, init, True)
    carry = lax.fori_loop(0, qi, lambda i, c: step(qi - 1 - i, c, False), carry)
    o_ref[...] = jnp.concatenate([acc for (_, acc) in carry], axis=-1)
    _zero_fill_wait(copies)


def _sb_attention(qkv, batch, seq, fill=None):
    t = qkv.shape[0]
    tq = ATT_TILE
    nq = seq // tq
    npair = N_SB // SB_HEADS
    w = SB_HEADS * HEAD_DIM
    base = 3 * D_FOX // w
    grid = (batch, npair, nq)
    fill_shape, fill_specs, fill_scratch = _zero_fill_specs(fill)
    outs = pl.pallas_call(
        functools.partial(_sb_attn_kernel, grid=grid),
        out_shape=tuple([jax.ShapeDtypeStruct((t, D_SB), F32)] + fill_shape),
        grid=grid,
        in_specs=[
            pl.BlockSpec((tq, w), lambda b, h, i: (b * nq + i, base + h)),
            pl.BlockSpec((seq, w), lambda b, h, i: (b, base + npair + h)),
            pl.BlockSpec((seq, w), lambda b, h, i: (b, base + 2 * npair + h)),
        ],
        out_specs=tuple([pl.BlockSpec((tq, w), lambda b, h, i: (b * nq + i, h))] + fill_specs),
        scratch_shapes=fill_scratch,
        compiler_params=_params(("parallel", "parallel", "arbitrary")),
        name="sb_attn",
    )(qkv, qkv, qkv)
    return outs if fill is not None else outs[0]


def _mla_prep_kernel(cq_ref, ckv_ref, kr1_ref, kr2_ref, pos_ref, gq_ref, gkv_ref, wq_ref, wkv_ref,
                     invf_ref, sgn_ref, q_out, k_out, v_out):
    ang = pos_ref[...].astype(F32) * invf_ref[...]
    cosv = jnp.cos(ang)
    sinv = jnp.sin(ang) * sgn_ref[...]
    qn = (_rms(cq_ref[...]) * gq_ref[...]).astype(BF16)
    qa = jnp.dot(qn, wq_ref[...], preferred_element_type=F32)
    for h in range(N_MLA):
        o = h * 3 * LANES
        pe = qa[:, o + LANES:o + 2 * LANES] * cosv + qa[:, o + 2 * LANES:o + 3 * LANES] * sinv
        q_out[:, h * 2 * LANES:h * 2 * LANES + LANES] = qa[:, o:o + LANES].astype(BF16)
        q_out[:, h * 2 * LANES + LANES:(h + 1) * 2 * LANES] = pe.astype(BF16)
    kvn = (_rms(ckv_ref[...]) * gkv_ref[...]).astype(BF16)
    kva = jnp.dot(kvn, wkv_ref[...], preferred_element_type=F32)
    kpe = (kr1_ref[...] * cosv + kr2_ref[...] * sinv).astype(BF16)
    for h in range(N_MLA):
        k_out[:, h * 2 * LANES:h * 2 * LANES + LANES] = kva[:, h * LANES:(h + 1) * LANES].astype(BF16)
        k_out[:, h * 2 * LANES + LANES:(h + 1) * 2 * LANES] = kpe
    v_out[...] = kva[:, N_MLA * MLA_NOPE:].astype(BF16)


def _mla_prep(misc, pos, gq, gkv, wq, wkv, tm=512):
    t = misc.shape[0]
    half = MLA_ROPE // 2
    inv_freq = ROPE_THETA ** (-jnp.arange(half, dtype=F32) / half)
    invf = jnp.tile(inv_freq, LANES // half).reshape(1, LANES)
    sgn = jnp.tile(jnp.concatenate([-jnp.ones((half,), F32), jnp.ones((half,), F32)]),
                   LANES // MLA_ROPE).reshape(1, LANES)
    const = lambda i: (0, 0)
    return pl.pallas_call(
        _mla_prep_kernel,
        out_shape=(jax.ShapeDtypeStruct((t, N_MLA * 2 * LANES), BF16),
                   jax.ShapeDtypeStruct((t, N_MLA * 2 * LANES), BF16),
                   jax.ShapeDtypeStruct((t, D_MLA), BF16)),
        grid=(t // tm,),
        in_specs=[
            pl.BlockSpec((tm, Q_LORA), lambda i: (i, MISC_CQ // Q_LORA)),
            pl.BlockSpec((tm, KV_LORA), lambda i: (i, MISC_CKV // KV_LORA)),
            pl.BlockSpec((tm, LANES), lambda i: (i, MISC_KR1 // LANES)),
            pl.BlockSpec((tm, LANES), lambda i: (i, MISC_KR2 // LANES)),
            pl.BlockSpec((tm, 1), lambda i: (i, 0)),
            pl.BlockSpec((1, Q_LORA), const),
            pl.BlockSpec((1, KV_LORA), const),
            pl.BlockSpec(wq.shape, const),
            pl.BlockSpec(wkv.shape, const),
            pl.BlockSpec((1, LANES), const),
            pl.BlockSpec((1, LANES), const),
        ],
        out_specs=(pl.BlockSpec((tm, N_MLA * 2 * LANES), lambda i: (i, 0)),
                   pl.BlockSpec((tm, N_MLA * 2 * LANES), lambda i: (i, 0)),
                   pl.BlockSpec((tm, D_MLA), lambda i: (i, 0))),
        compiler_params=_params(("parallel",)),
        name="mla_prep",
    )(misc, misc, misc, misc, pos, gq.reshape(1, -1), gkv.reshape(1, -1), wq, wkv, invf, sgn)


def _gelu(x):
    return 0.5 * x * (1.0 + lax.erf(x * np.float32(np.sqrt(0.5))))


def _sgu_kernel(u_ref, v_ref, lng_ref, lnb_ref, w_ref, bias_ref, o_ref):
    tm = u_ref.shape[0]
    v = _gelu(v_ref[...])
    mu = jnp.mean(v, axis=-1, keepdims=True)
    xc = v - mu
    var = jnp.mean(xc * xc, axis=-1, keepdims=True)
    vb = (xc * lax.rsqrt(var + LN_EPS) * lng_ref[...] + lnb_ref[...]).astype(BF16)
    row = lax.broadcasted_iota(jnp.int32, (CHUNK, CHUNK), 0)
    col = lax.broadcasted_iota(jnp.int32, (CHUNK, LANES), 1)
    tril = lax.broadcasted_iota(jnp.int32, (CHUNK, CHUNK), 1) <= row
    first = col < SGU_CH
    ws = [jnp.where(tril, w_ref[g], 0.0).astype(BF16) for g in range(N_SGU)]
    zero = jnp.zeros((CHUNK, LANES), BF16)
    for c in range(tm // CHUNK):
        rows = slice(c * CHUNK, (c + 1) * CHUNK)
        for p in range(N_SGU // 2):
            cols = slice(p * LANES, (p + 1) * LANES)
            vp = vb[rows, cols]
            mixed = (jnp.dot(ws[2 * p], jnp.where(first, vp, zero), preferred_element_type=F32)
                     + jnp.dot(ws[2 * p + 1], jnp.where(first, zero, vp), preferred_element_type=F32))
            o_ref[rows, cols] = _gelu(u_ref[rows, cols]) * (mixed + bias_ref[:, cols])


def _sgu(misc, ln_g, ln_b, w_s, b_s, tm=512):
    t = misc.shape[0]
    bias = jnp.repeat(b_s.T, SGU_CH, axis=1)
    const2 = lambda i: (0, 0)
    return pl.pallas_call(
        _sgu_kernel,
        out_shape=jax.ShapeDtypeStruct((t, D_SGU), F32),
        grid=(t // tm,),
        in_specs=[
            pl.BlockSpec((tm, D_SGU), lambda i: (i, MISC_U // D_SGU)),
            pl.BlockSpec((tm, D_SGU), lambda i: (i, MISC_V // D_SGU)),
            pl.BlockSpec((1, D_SGU), const2),
            pl.BlockSpec((1, D_SGU), const2),
            pl.BlockSpec((N_SGU, CHUNK, CHUNK), lambda i: (0, 0, 0)),
            pl.BlockSpec((CHUNK, D_SGU), const2),
        ],
        out_specs=pl.BlockSpec((tm, D_SGU), lambda i: (i, 0)),
        compiler_params=_params(("parallel",)),
        name="sgu",
    )(misc, misc, ln_g.reshape(1, -1), ln_b.reshape(1, -1), w_s, bias)


def _pack_bf16_pairs(hb):
    n = hb.shape[1] // 2
    bits = pltpu.bitcast(hb.astype(F32), jnp.uint32)
    return (bits[:, n:] & jnp.uint32(0xFFFF0000)) | (bits[:, :n] >> 16)


def _unpack_bf16_pairs(xu):
    lo = pltpu.bitcast(xu << 16, F32).astype(BF16)
    hi = pltpu.bitcast(xu & jnp.uint32(0xFFFF0000), F32).astype(BF16)
    return jnp.concatenate([lo, hi], axis=1)


def _outproj_router_kernel(of_ref, os_ref, om_ref, og_ref, x_ref, ong_ref, wo_ref, nfg_ref, rw_ref,
                           rb_ref, xo_ref, h2_ref, idx_ref, gate_ref, rank_ref, cnt_ref, carry_ref):
    tm = x_ref.shape[0]
    th = tm // 2
    halves = [slice(h * th, (h + 1) * th) for h in range(2)]

    @pl.when(pl.program_id(0) == 0)
    def _():
        carry_ref[...] = jnp.zeros_like(carry_ref)

    os_ = []
    for rows in halves:
        o = jnp.concatenate([_rms(r[rows, :]) for r in (of_ref, os_ref, om_ref, og_ref)], axis=-1)
        os_.append((o * ong_ref[...]).astype(BF16))
    xns = [x_ref[rows, :] + jnp.dot(o, wo_ref[...], preferred_element_type=F32) for rows, o in zip(halves, os_)]
    hbs = []
    for rows, xn in zip(halves, xns):
        xo_ref[rows, :] = xn
        hb = (_rms(xn) * nfg_ref[...]).astype(BF16)
        h2_ref[rows, :] = _pack_bf16_pairs(hb)
        hbs.append(hb)

    lane = lax.broadcasted_iota(jnp.int32, (th, LANES), 1)
    all_logits = [jnp.dot(hb, rw_ref[...], preferred_element_type=F32) + rb_ref[...] for hb in hbs]
    picks = []
    for logits in all_logits:
        vals = jnp.where(lane < N_EXPERTS, logits, -jnp.inf)
        sels, tops = [], []
        for _ in range(TOP_K):
            m = jnp.max(vals, axis=-1, keepdims=True)
            idx = jnp.min(jnp.where(vals == m, lane, LANES), axis=-1, keepdims=True)
            sel = lane == idx
            vals = jnp.where(sel, -jnp.inf, vals)
            sels.append(sel)
            tops.append((m, idx))
        es = [jnp.exp(m - tops[0][0]) for (m, _) in tops]
        den = es[0] + es[1] + es[2] + es[3]
        mh = jnp.where(sels[0] | sels[1] | sels[2] | sels[3], 1.0, 0.0)
        picks.append((sels, tops, es, den, mh))

    r = lax.broadcasted_iota(jnp.int32, (th, th), 0)
    c = lax.broadcasted_iota(jnp.int32, (th, th), 1)
    before = (c < r).astype(BF16)
    parts = [jnp.dot(before, p[4].astype(BF16), preferred_element_type=F32) for p in picks]
    total = carry_ref[...]
    for rows, part, (sels, tops, es, den, mh) in zip(halves, parts, picks):
        cnt = part + total
        total = cnt[th - 1:th, :] + mh[th - 1:th, :]
        idx_o = jnp.zeros((th, LANES), jnp.int32)
        gate_o = jnp.zeros((th, LANES), F32)
        rank_o = jnp.zeros((th, LANES), jnp.int32)
        for k in range(TOP_K):
            rank_k = jnp.sum(jnp.where(sels[k], cnt, 0.0), axis=-1, keepdims=True).astype(jnp.int32)
            idx_o = jnp.where(lane == k, tops[k][1], idx_o)
            gate_o = jnp.where(lane == k, es[k] / den, gate_o)
            rank_o = jnp.where(lane == k, rank_k, rank_o)
        idx_ref[rows, :] = idx_o
        gate_ref[rows, :] = gate_o
        rank_ref[rows, :] = rank_o
    carry_ref[...] = total
    cnt_ref[...] = jnp.broadcast_to(total, cnt_ref.shape).astype(jnp.int32)


def _outproj_router(o_fox, o_sb, o_mla, o_sgu, x, out_norm_g, w_o, norm_ffn_g, router_w, router_b, tm=512):
    t, d = x.shape
    dg = o_fox.shape[1]
    rw = jnp.zeros((d, LANES), BF16).at[:, :N_EXPERTS].set(router_w.astype(BF16))
    rb = jnp.zeros((1, LANES), F32).at[0, :N_EXPERTS].set(router_b)
    const = lambda i: (0, 0)
    grp = pl.BlockSpec((tm, dg), lambda i: (i, 0))
    lanes_out = pl.BlockSpec((tm, LANES), lambda i: (i, 0))
    return pl.pallas_call(
        _outproj_router_kernel,
        out_shape=(jax.ShapeDtypeStruct((t, d), F32),
                   jax.ShapeDtypeStruct((t, d // 2), jnp.uint32),
                   jax.ShapeDtypeStruct((t, LANES), jnp.int32),
                   jax.ShapeDtypeStruct((t, LANES), F32),
                   jax.ShapeDtypeStruct((t, LANES), jnp.int32),
                   jax.ShapeDtypeStruct((8, LANES), jnp.int32)),
        grid=(t // tm,),
        in_specs=[grp, grp, grp, grp,
                  pl.BlockSpec((tm, d), lambda i: (i, 0)),
                  pl.BlockSpec((1, 4 * dg), const),
                  pl.BlockSpec((4 * dg, d), const),
                  pl.BlockSpec((1, d), const),
                  pl.BlockSpec((d, LANES), const),
                  pl.BlockSpec((1, LANES), const)],
        out_specs=(pl.BlockSpec((tm, d), lambda i: (i, 0)),
                   pl.BlockSpec((tm, d // 2), lambda i: (i, 0)),
                   lanes_out, lanes_out, lanes_out,
                   pl.BlockSpec((8, LANES), const)),
        scratch_shapes=[pltpu.VMEM((1, LANES), F32)],
        compiler_params=_params(("arbitrary",)),
        name="outproj_router",
    )(o_fox, o_sb, o_mla, o_sgu, x, out_norm_g.reshape(1, -1), w_o, norm_ffn_g.reshape(1, -1), rw, rb)


def _scatter_rows_kernel(dest_hbm, x_ref, xs_init_hbm, xs_hbm, dest0, dest1, sems):
    del xs_init_hbm
    dest_smem = (dest0, dest1)
    i = pl.program_id(0)
    nsteps = pl.num_programs(0)
    tm = x_ref.shape[0]
    n = tm * TOP_K

    def idx_copy(step, sl):
        return pltpu.make_async_copy(dest_hbm.at[pl.ds(step * n, n)], dest_smem[sl], sems.at[sl])

    @pl.when(i == 0)
    def _():
        idx_copy(0, 0).start()

    for sl in range(2):
        @pl.when(i % 2 == sl)
        def _(sl=sl):
            @pl.when(i + 1 < nsteps)
            def _():
                idx_copy(i + 1, 1 - sl).start()

            idx_copy(i, sl).wait()

            def issue(t, _):
                for k in range(TOP_K):
                    pltpu.make_async_copy(x_ref.at[pl.ds(t, 1)],
                                          xs_hbm.at[pl.ds(dest_smem[sl][t * TOP_K + k], 1)], sems.at[2]).start()
                return 0

            lax.fori_loop(0, tm, issue, 0, unroll=2)

    for _ in range(TOP_K):
        pltpu.make_async_copy(x_ref, xs_hbm.at[pl.ds(0, tm)], sems.at[2]).wait()


def _scatter_rows(dest_flat, src, nrows, xs_init):
    t, w = src.shape
    tm = SCATTER_TM
    assert xs_init.shape == (nrows, w) and xs_init.dtype == src.dtype
    return pl.pallas_call(
        _scatter_rows_kernel,
        out_shape=jax.ShapeDtypeStruct((nrows, w), src.dtype),
        grid=(t // tm,),
        in_specs=[pl.BlockSpec(memory_space=pl.ANY),
                  pl.BlockSpec((tm, w), lambda i: (i, 0)),
                  pl.BlockSpec(memory_space=pl.ANY)],
        out_specs=pl.BlockSpec(memory_space=pl.ANY),
        scratch_shapes=[pltpu.SMEM((tm * TOP_K,), jnp.int32), pltpu.SMEM((tm * TOP_K,), jnp.int32),
                        pltpu.SemaphoreType.DMA((3,))],
        input_output_aliases={2: 0},
        compiler_params=_params(("arbitrary",)),
        name="scatter_rows",
    )(dest_flat, src, xs_init)


def _expert_kernel(sbe_ref, nval_ref, nused_ref, x_ref, wg_ref, bg_ref, wu_ref, bu_ref, wd_ref, bd_ref,
                   y_init_hbm, y_ref):
    del y_init_hbm
    s = pl.program_id(0)
    f = pl.program_id(1)
    d = y_ref.shape[1]

    @pl.when(s < nused_ref[0])
    def _():
        @pl.when(f == 0)
        def _():
            y_ref[...] = jnp.broadcast_to(bd_ref[...], y_ref.shape)

        nsub = (nval_ref[s] + EXP_SUB - 1) // EXP_SUB

        for n in range(1, EXP_ROWS // EXP_SUB + 1):
            @pl.when(nsub == n)
            def _(m=n * EXP_SUB):
                x = _unpack_bf16_pairs(x_ref[0:m, :])
                g = jnp.dot(x, wg_ref[...].astype(BF16), preferred_element_type=F32) + bg_ref[...]
                g = jnp.minimum(g, SWIGLU_LIMIT)
                u = jnp.dot(x, wu_ref[...].astype(BF16), preferred_element_type=F32) + bu_ref[...]
                u = jnp.clip(u, -SWIGLU_LIMIT, SWIGLU_LIMIT)
                a = (g * jax.nn.sigmoid(SWIGLU_ALPHA * g) * (u + 1.0)).astype(BF16)
                wd = wd_ref[...].astype(BF16)
                for c in range(d // EXP_DN):
                    cols = slice(c * EXP_DN, (c + 1) * EXP_DN)
                    y_ref[0:m, cols] += jnp.dot(a, wd[:, cols], preferred_element_type=F32)


def _experts(sb_expert, sb_nvalid, n_used, xs, layer, w_gate, b_gate, w_up, b_up, w_down, b_down, nsb_max, y_init):
    d = w_gate.shape[2]
    dff = w_gate.shape[3]
    assert d % EXP_DN == 0 and dff % EXP_FF == 0 and EXP_ROWS % EXP_SUB == 0
    assert y_init.shape == (nsb_max * EXP_ROWS, d) and y_init.dtype == F32
    nf = dff // EXP_FF

    def xmap(s, f, sbe, nval, nused):
        return (jnp.minimum(s, nused[0] - 1), 0)

    def ff(s, f, nused):
        return jnp.where(s < nused[0], f, nf - 1)

    b_gate4 = b_gate.reshape(b_gate.shape[0], N_EXPERTS, 1, dff)
    b_up4 = b_up.reshape(b_up.shape[0], N_EXPERTS, 1, dff)
    b_down4 = b_down.reshape(b_down.shape[0], N_EXPERTS, 1, d)
    y_init_operand = 3 + 7
    return pl.pallas_call(
        _expert_kernel,
        out_shape=jax.ShapeDtypeStruct((nsb_max * EXP_ROWS, d), F32),
        grid_spec=pltpu.PrefetchScalarGridSpec(
            num_scalar_prefetch=3,
            grid=(n_used[0], nf),
            in_specs=[
                pl.BlockSpec((EXP_ROWS, d // 2), xmap),
                pl.BlockSpec((None, None, d, EXP_FF), lambda s, f, sbe, nval, nused: (layer, sbe[s], 0, ff(s, f, nused))),
                pl.BlockSpec((None, None, 1, EXP_FF), lambda s, f, sbe, nval, nused: (layer, sbe[s], 0, ff(s, f, nused))),
                pl.BlockSpec((None, None, d, EXP_FF), lambda s, f, sbe, nval, nused: (layer, sbe[s], 0, ff(s, f, nused))),
                pl.BlockSpec((None, None, 1, EXP_FF), lambda s, f, sbe, nval, nused: (layer, sbe[s], 0, ff(s, f, nused))),
                pl.BlockSpec((None, None, EXP_FF, d), lambda s, f, sbe, nval, nused: (layer, sbe[s], ff(s, f, nused), 0)),
                pl.BlockSpec((None, None, 1, d), lambda s, f, sbe, nval, nused: (layer, sbe[s], 0, 0)),
                pl.BlockSpec(memory_space=pl.ANY),
            ],
            out_specs=pl.BlockSpec((EXP_ROWS, d), xmap),
        ),
        input_output_aliases={y_init_operand: 0},
        compiler_params=_params(("arbitrary", "arbitrary")),
        name="experts",
    )(sb_expert, sb_nvalid, n_used, xs, w_gate, b_gate4, w_up, b_up4, w_down, b_down4, y_init)


def _combine_kernel(dest_hbm, y_hbm, x_ref, gate_ref, fg_ref, o_ref, dest0, dest1, ybuf0, ybuf1, sems, *,
                    final_norm):
    dest_smem = (dest0, dest1)
    ybuf = (ybuf0, ybuf1)
    i = pl.program_id(0)
    nsteps = pl.num_programs(0)
    tm = x_ref.shape[0]
    n = tm * TOP_K
    slot = i % 2
    nslot = 1 - slot

    def idx_copy(step, sl):
        return pltpu.make_async_copy(dest_hbm.at[pl.ds(step * n, n)], dest_smem[sl], sems.at[sl])

    def issue_rows(sl):
        def issue(t, _):
            for k in range(TOP_K):
                pltpu.make_async_copy(y_hbm.at[pl.ds(dest_smem[sl][t * TOP_K + k], 1)],
                                      ybuf[sl].at[pl.ds(k * tm + t, 1)], sems.at[2 + sl]).start()
            return 0

        lax.fori_loop(0, tm, issue, 0, unroll=2)

    @pl.when(i == 0)
    def _():
        first = idx_copy(0, 0)
        first.start()
        first.wait()
        issue_rows(0)

        @pl.when(nsteps > 1)
        def _():
            idx_copy(1, 1).start()

    for sl in range(2):
        @pl.when(jnp.logical_and(i + 1 < nsteps, nslot == sl))
        def _(sl=sl):
            idx_copy(i + 1, sl).wait()
            issue_rows(sl)

    for sl in range(2):
        @pl.when(slot == sl)
        def _(sl=sl):
            @pl.when(i + 2 < nsteps)
            def _():
                idx_copy(i + 2, sl).start()

            pltpu.make_async_copy(y_hbm.at[pl.ds(0, n)], ybuf[sl], sems.at[2 + sl]).wait()
            gate = gate_ref[...]
            acc = gate[:, 0:1] * ybuf[sl][0:tm]
            for k in range(1, TOP_K):
                acc = acc + gate[:, k:k + 1] * ybuf[sl][k * tm:(k + 1) * tm]
            out = x_ref[...] + acc
            if final_norm:
                out = _rms(out) * fg_ref[...]
            o_ref[...] = out


def _combine(dest_flat, y, x, gates, final_g, final_norm):
    t, d = x.shape
    tm = COMBINE_TM
    return pl.pallas_call(
        functools.partial(_combine_kernel, final_norm=final_norm),
        out_shape=jax.ShapeDtypeStruct((t, d), F32),
        grid=(t // tm,),
        in_specs=[
            pl.BlockSpec(memory_space=pl.ANY),
            pl.BlockSpec(memory_space=pl.ANY),
            pl.BlockSpec((tm, d), lambda i: (i, 0)),
            pl.BlockSpec((tm, LANES), lambda i: (i, 0)),
            pl.BlockSpec((1, d), lambda i: (0, 0)),
        ],
        out_specs=pl.BlockSpec((tm, d), lambda i: (i, 0)),
        scratch_shapes=[pltpu.SMEM((tm * TOP_K,), jnp.int32), pltpu.SMEM((tm * TOP_K,), jnp.int32),
                        pltpu.VMEM((TOP_K * tm, d), F32), pltpu.VMEM((TOP_K * tm, d), F32),
                        pltpu.SemaphoreType.DMA((4,))],
        compiler_params=_params(("arbitrary",)),
        name="combine",
    )(dest_flat, y, x, gates, final_g.reshape(1, d))


def _in_proj_weights(w_in):
    d = w_in.shape[0]
    o_gate = 3 * D_FOX
    o_sb = o_gate + N_FOX
    o_cq = o_sb + 3 * D_SB
    o_ckv = o_cq + Q_LORA
    o_kr = o_ckv + KV_LORA
    o_sgu = o_kr + MLA_ROPE
    half = MLA_ROPE // 2
    w_qkv = jnp.concatenate([w_in[:, :o_gate], w_in[:, o_sb:o_cq]], axis=1).astype(BF16)
    kr = w_in[:, o_kr:o_sgu]
    zpad = jnp.zeros((d, LANES - MLA_ROPE), w_in.dtype)
    w_misc = jnp.concatenate([
        w_in[:, o_sgu:o_sgu + 2 * D_SGU],
        w_in[:, o_cq:o_ckv],
        w_in[:, o_ckv:o_kr],
        kr, zpad,
        kr[:, half:], kr[:, :half], zpad,
        w_in[:, o_gate:o_sb], jnp.zeros((d, LANES - N_FOX), w_in.dtype),
    ], axis=1).astype(BF16)
    assert w_misc.shape[1] == MISC_W
    return w_qkv, w_misc


def _mla_weights(w_q_b, w_kv_b):
    half = MLA_ROPE // 2
    wq = w_q_b.reshape(Q_LORA, N_MLA, MLA_NOPE + MLA_ROPE)
    x1 = wq[:, :, MLA_NOPE:MLA_NOPE + half]
    x2 = wq[:, :, MLA_NOPE + half:]
    z = jnp.zeros((Q_LORA, N_MLA, LANES - MLA_ROPE), w_q_b.dtype)
    wq_r = jnp.concatenate([wq[:, :, :MLA_NOPE], x1, x2, z, x2, x1, z], axis=2)
    wq_r = wq_r.reshape(Q_LORA, N_MLA * 3 * LANES).astype(BF16)
    wkv = w_kv_b.reshape(KV_LORA, N_MLA, MLA_NOPE + MLA_V)
    wkv_r = jnp.concatenate([wkv[:, :, :MLA_NOPE].reshape(KV_LORA, -1),
                             wkv[:, :, MLA_NOPE:].reshape(KV_LORA, -1)], axis=1).astype(BF16)
    return wq_r, wkv_r


def _routing_tables(idx, rank, cnt, n_tokens):
    nsb_max = N_EXPERTS + (n_tokens * TOP_K) // EXP_ROWS
    counts = cnt[0, :N_EXPERTS]
    nsb_e = (counts + EXP_ROWS - 1) // EXP_ROWS
    sb_end = jnp.cumsum(nsb_e)
    sb_start = sb_end - nsb_e
    n_used = sb_end[-1]
    top_idx = idx[:, :TOP_K].reshape(-1)
    dest = (sb_start * EXP_ROWS)[top_idx] + rank[:, :TOP_K].reshape(-1)
    s_ids = jnp.arange(nsb_max, dtype=jnp.int32)
    sb_e = jnp.minimum(jnp.sum(s_ids[:, None] >= sb_end[None, :], axis=1), N_EXPERTS - 1).astype(jnp.int32)
    last_e = sb_e[jnp.maximum(n_used - 1, 0)]
    sb_e = jnp.where(s_ids < n_used, sb_e, last_e)
    nval = jnp.clip(counts[sb_e] - (s_ids - sb_start[sb_e]) * EXP_ROWS, 0, EXP_ROWS)
    nval = jnp.where(s_ids < n_used, nval, 0).astype(jnp.int32)
    return dest.astype(jnp.int32), sb_e, nval, n_used.reshape(1).astype(jnp.int32), nsb_max


def _layer(x, pos, l, p, final_g, final_norm, batch, seq, moe_bufs):
    t = x.shape[0]
    w_qkv, w_misc = _in_proj_weights(p["w_in"][l])
    qkv = _norm_matmul(x, p["norm_mix_g"][l], w_qkv, BF16, 512, 1536, "in_proj_qkv")
    misc = _norm_matmul(x, p["norm_mix_g"][l], w_misc, F32, 512, MISC_W, "in_proj_misc")

    crow = _fox_prep(misc, p["b_forget"][l], batch, seq)
    nkb = seq // ATT_TILE
    crow = crow.reshape(batch, nkb, N_FOX // FOX_HEADS, FOX_HEADS, ATT_TILE).transpose(0, 2, 1, 3, 4)
    if moe_bufs is None:
        nrows = (N_EXPERTS + (t * TOP_K) // EXP_ROWS) * EXP_ROWS
        o_fox, xs_prev = _fox_attention(qkv, crow, batch, seq,
                                        jax.ShapeDtypeStruct((nrows, x.shape[1] // 2), jnp.uint32))
        o_sb, y_prev = _sb_attention(qkv, batch, seq, jax.ShapeDtypeStruct((nrows, x.shape[1]), F32))
    else:
        xs_prev, y_prev = moe_bufs
        o_fox = _fox_attention(qkv, crow, batch, seq)
        o_sb = _sb_attention(qkv, batch, seq)
    wq_r, wkv_r = _mla_weights(p["mla_w_q_b"][l], p["mla_w_kv_b"][l])
    q_m, k_m, v_m = _mla_prep(misc, pos, p["mla_q_norm_g"][l], p["mla_kv_norm_g"][l], wq_r, wkv_r)
    o_mla = _mla_attention(q_m, k_m, v_m, batch, seq)
    o_sgu = _sgu(misc, p["sgu_ln_g"][l], p["sgu_ln_b"][l], p["sgu_w_s"][l], p["sgu_b_s"][l])

    x_new, h2p, idx, gates, rank, cnt = _outproj_router(
        o_fox, o_sb, o_mla, o_sgu, x, p["out_norm_g"][l], p["w_o"][l].astype(BF16),
        p["norm_ffn_g"][l], p["router_w"][l], p["router_b"][l])

    dest, sb_e, nval, n_used, nsb_max = _routing_tables(idx, rank, cnt, t)
    xs = _scatter_rows(dest, h2p, nsb_max * EXP_ROWS, xs_prev)
    y = _experts(sb_e, nval, n_used, xs, l, p["w_gate"], p["b_gate"], p["w_up"], p["b_up"],
                 p["w_down"], p["b_down"], nsb_max, y_prev)
    return _combine(dest, y, x_new, gates, final_g, final_norm), (xs, y)


def kernel(x, positions, norm_mix_g, w_in, b_forget, mla_q_norm_g, mla_kv_norm_g, mla_w_q_b, mla_w_kv_b,
           sgu_ln_g, sgu_ln_b, sgu_w_s, sgu_b_s, out_norm_g, w_o, norm_ffn_g, router_w, router_b, w_gate,
           b_gate, w_up, b_up, w_down, b_down, final_norm_g):
    batch, seq, d = x.shape
    depth = w_in.shape[0]
    p = dict(norm_mix_g=norm_mix_g, w_in=w_in, b_forget=b_forget, mla_q_norm_g=mla_q_norm_g,
             mla_kv_norm_g=mla_kv_norm_g, mla_w_q_b=mla_w_q_b, mla_w_kv_b=mla_w_kv_b, sgu_ln_g=sgu_ln_g,
             sgu_ln_b=sgu_ln_b, sgu_w_s=sgu_w_s, sgu_b_s=sgu_b_s, out_norm_g=out_norm_g, w_o=w_o,
             norm_ffn_g=norm_ffn_g, router_w=router_w, router_b=router_b, w_gate=w_gate, b_gate=b_gate,
             w_up=w_up, b_up=b_up, w_down=w_down, b_down=b_down)
    h = x.reshape(batch * seq, d)
    pos = positions.reshape(batch * seq, 1)
    moe_bufs = None
    for l in range(depth):
        h, moe_bufs = _layer(h, pos, l, p, final_norm_g, l == depth - 1, batch, seq, moe_bufs)
    return h.reshape(batch, seq, d)
```

```python
import functools

import numpy as np
import jax
import jax.numpy as jnp
from jax import lax
from jax.experimental import pallas as pl
from jax.experimental.pallas import tpu as pltpu

HEAD_DIM = 64
N_FOX = 8
N_SB = 8
N_MLA = 4
MLA_NOPE = 128
MLA_ROPE = 64
MLA_V = 128
Q_LORA = 512
KV_LORA = 256
N_SGU = 8
SGU_CH = 64
CHUNK = 128
ROPE_THETA = 10000.0
D_FOX = N_FOX * HEAD_DIM
D_SB = N_SB * HEAD_DIM
D_MLA = N_MLA * MLA_V
D_SGU = N_SGU * SGU_CH
N_EXPERTS = 32
TOP_K = 4
SWIGLU_LIMIT = 7.0
SWIGLU_ALPHA = 1.702
RMS_EPS = 1e-6
LN_EPS = 1e-5

LANES = 128
VMEM_LIMIT = 56 * 1024 * 1024

ATT_TILE = 256
FOX_HEADS = 8
MLA_HEADS = 4
SB_HEADS = 8
EXP_ROWS = 1536
EXP_SUB = 256
EXP_FF = 256
EXP_DN = 512
SCATTER_TM = 512
ZERO_ROWS = 512
COMBINE_TM = 256

F32 = jnp.float32
BF16 = jnp.bfloat16

MISC_U = 0
MISC_V = 512
MISC_CQ = 1024
MISC_CKV = 1536
MISC_KR1 = 1792
MISC_KR2 = 1920
MISC_GATE = 2048
MISC_W = 2176


def _params(sem, vmem=VMEM_LIMIT):
    return pltpu.CompilerParams(dimension_semantics=sem, vmem_limit_bytes=vmem)


def _split3(x):
    hi = x.astype(BF16)
    r1 = x - hi.astype(F32)
    mid = r1.astype(BF16)
    lo = (r1 - mid.astype(F32)).astype(BF16)
    return hi, mid, lo


def _log_sigmoid_pair(z):
    t = jnp.log1p(jnp.exp(-jnp.abs(z)))
    return jnp.minimum(z, 0.0) - t, jnp.minimum(-z, 0.0) - t


def _rms(x):
    return x * lax.rsqrt(jnp.mean(x * x, axis=-1, keepdims=True) + RMS_EPS)


def _norm_matmul_kernel(x_ref, g_ref, w_ref, o_ref, xn_ref):
    tm = x_ref.shape[0]

    @pl.when(pl.program_id(1) == 0)
    def _():
        for rows in (slice(0, tm // 2), slice(tm // 2, tm)):
            xn = (_rms(x_ref[rows, :]) * g_ref[...]).astype(BF16)
            xn_ref[rows, :] = xn
            o_ref[rows, :] = jnp.dot(xn, w_ref[...], preferred_element_type=F32).astype(o_ref.dtype)

    @pl.when(pl.program_id(1) != 0)
    def _():
        o_ref[...] = jnp.dot(xn_ref[...], w_ref[...], preferred_element_type=F32).astype(o_ref.dtype)


def _norm_matmul(x, g, w, out_dtype, tm, tn, name):
    t, d = x.shape
    n = w.shape[1]
    return pl.pallas_call(
        _norm_matmul_kernel,
        out_shape=jax.ShapeDtypeStruct((t, n), out_dtype),
        grid=(t // tm, n // tn),
        in_specs=[
            pl.BlockSpec((tm, d), lambda i, j: (i, 0)),
            pl.BlockSpec((1, d), lambda i, j: (0, 0)),
            pl.BlockSpec((d, tn), lambda i, j: (0, j)),
        ],
        out_specs=pl.BlockSpec((tm, tn), lambda i, j: (i, j)),
        scratch_shapes=[pltpu.VMEM((tm, d), BF16)],
        compiler_params=_params(("parallel", "arbitrary")),
        name=name,
    )(x, g.reshape(1, d), w)


def _fox_prep_kernel(gate_ref, bf_ref, crow_ref):
    s_len = gate_ref.shape[0]
    bl = ATT_TILE
    r = lax.broadcasted_iota(jnp.int32, (bl, bl), 0)
    c = lax.broadcasted_iota(jnp.int32, (bl, bl), 1)
    tri = (c <= r).astype(BF16)
    carry = jnp.zeros((1, LANES), F32)
    for i in range(s_len // bl):
        z = gate_ref[i * bl:(i + 1) * bl, :] + bf_ref[...]
        lf, _ = _log_sigmoid_pair(z)
        hi, mid, lo = _split3(lf)
        cs = (jnp.dot(tri, hi, preferred_element_type=F32)
              + jnp.dot(tri, mid, preferred_element_type=F32)
              + jnp.dot(tri, lo, preferred_element_type=F32)) + carry
        crow_ref[i] = cs.T[0:N_FOX, :]
        carry = cs[bl - 1:bl, :]


def _fox_prep(misc, b_forget, batch, seq):
    nkb = seq // ATT_TILE
    bf = jnp.zeros((1, LANES), F32).at[0, :N_FOX].set(b_forget)
    return pl.pallas_call(
        _fox_prep_kernel,
        out_shape=jax.ShapeDtypeStruct((batch, nkb, N_FOX, ATT_TILE), F32),
        grid=(batch,),
        in_specs=[
            pl.BlockSpec((seq, LANES), lambda b: (b, MISC_GATE // LANES)),
            pl.BlockSpec((1, LANES), lambda b: (0, 0)),
        ],
        out_specs=pl.BlockSpec((None, nkb, N_FOX, ATT_TILE), lambda b: (b, 0, 0, 0)),
        compiler_params=_params(("parallel",)),
        name="fox_prep",
    )(misc, bf)


def _causal_mask(tq):
    row = lax.broadcasted_iota(jnp.int32, (tq, tq), 0)
    col = lax.broadcasted_iota(jnp.int32, (tq, tq), 1)
    return col <= row, col < row


def _softmax_sweep(qs, k_of, v_of, bias_of, scale, qi, tq, ones_lane=None):
    nh = len(qs)
    incl, _ = _causal_mask(tq)

    def step(kb, carry, diag):
        ks = pl.multiple_of(kb * tq, tq)
        ss = [lax.dot_general(qs[h], k_of(h, ks), (((1,), (1,)), ((), ())), preferred_element_type=F32)
              for h in range(nh)]
        ps, stats = [], []
        for h in range(nh):
            m, l, _ = carry[h]
            s = ss[h]
            if scale is not None:
                s = s * scale
            b = bias_of(h, kb)
            if b is not None:
                s = s + b
            if diag:
                s = jnp.where(incl, s, -jnp.inf)
            m_new = jnp.maximum(m, jnp.max(s, axis=-1, keepdims=True))
            alpha = jnp.exp(m - m_new)
            p = jnp.exp(s - m_new)
            if ones_lane is None:
                l = alpha * l + jnp.sum(p, axis=-1, keepdims=True)
            stats.append((m_new, l, alpha))
            ps.append(p.astype(BF16))
        out = []
        for h in range(nh):
            m_new, l, alpha = stats[h]
            acc = alpha * carry[h][2] + jnp.dot(ps[h], v_of(h, ks), preferred_element_type=F32)
            out.append((m_new, l, acc))
        return tuple(out)

    dv = v_of(0, 0).shape[-1]
    l0 = jnp.zeros((tq, 1), F32) if ones_lane is None else None
    init = tuple((jnp.full((tq, 1), -jnp.inf, F32), l0, jnp.zeros((tq, dv), F32)) for _ in range(nh))
    carry = lax.fori_loop(0, qi, lambda kb, c: step(kb, c, False), init)
    carry = step(qi, carry, True)
    if ones_lane is None:
        return [acc / l for (_, l, acc) in carry]
    return [acc / acc[:, ones_lane(h):ones_lane(h) + 1] for h, (_, _, acc) in enumerate(carry)]


def _zero_fill_copies(fill_hbm, zeros, sem, grid):
    nq = grid[2]
    ngroups = grid[0] * grid[1]
    rows = zeros.shape[0]
    nblocks = fill_hbm.shape[0] // rows
    per_group = (nblocks + ngroups - 1) // ngroups
    work = nq * (nq + 1) // 2
    max_per_step = (per_group * nq + work - 1) // work + 1
    group = pl.program_id(0) * grid[1] + pl.program_id(1)
    qi = pl.program_id(2)
    first = per_group * (qi * (qi + 1) // 2) // work
    count = per_group * ((qi + 1) * (qi + 2) // 2) // work - first
    copies = []
    for j in range(max_per_step):
        block = group * per_group + first + j
        valid = jnp.logical_and(j < count, block < nblocks)
        at = jnp.minimum(block, nblocks - 1) * rows
        copies.append((valid, pltpu.make_async_copy(zeros, fill_hbm.at[pl.ds(at, rows)], sem)))
    return copies


def _zero_fill_start(fill, grid):
    if not fill:
        return []
    fill_hbm, zeros, sem = fill
    zeros[...] = jnp.zeros_like(zeros)
    copies = _zero_fill_copies(fill_hbm, zeros, sem, grid)
    for valid, cp in copies:
        @pl.when(valid)
        def _(cp=cp):
            cp.start()
    return copies


def _zero_fill_wait(copies):
    for valid, cp in copies:
        @pl.when(valid)
        def _(cp=cp):
            cp.wait()


def _zero_fill_specs(fill):
    if fill is None:
        return [], [], []
    assert fill.shape[0] % ZERO_ROWS == 0
    return ([fill], [pl.BlockSpec(memory_space=pl.ANY)],
            [pltpu.VMEM((ZERO_ROWS, fill.shape[1]), fill.dtype), pltpu.SemaphoreType.DMA(())])


def _fox_attn_kernel(q_ref, k_ref, v_ref, crow_ref, o_ref, *fill, grid):
    copies = _zero_fill_start(fill, grid)
    tq = q_ref.shape[0]
    qi = pl.program_id(2)
    q = q_ref[...] * jnp.asarray(HEAD_DIM ** -0.5, BF16)
    qs = [q[:, j * HEAD_DIM:(j + 1) * HEAD_DIM] for j in range(FOX_HEADS)]

    def k_of(h, ks):
        return k_ref[pl.ds(ks, tq), h * HEAD_DIM:(h + 1) * HEAD_DIM]

    first = lax.broadcasted_iota(jnp.int32, (tq, LANES), 1) < HEAD_DIM
    one = jnp.ones((tq, LANES), BF16)

    def v_of(h, ks):
        pair = v_ref[pl.ds(ks, tq), (h // 2) * LANES:(h // 2 + 1) * LANES]
        return jnp.where(first, pair, one) if h % 2 == 0 else jnp.where(first, one, pair)

    def bias_of(h, kb):
        return -crow_ref[kb][h:h + 1, :]

    outs = _softmax_sweep(qs, k_of, v_of, bias_of, None, qi, tq,
                          ones_lane=lambda h: HEAD_DIM if h % 2 == 0 else 0)
    o_ref[...] = jnp.concatenate([jnp.where(first, outs[h], outs[h + 1]) for h in range(0, FOX_HEADS, 2)],
                                 axis=-1)
    _zero_fill_wait(copies)


def _fox_attention(qkv, crow, batch, seq, fill=None):
    t = qkv.shape[0]
    tq = ATT_TILE
    nq = seq // tq
    npair = N_FOX // FOX_HEADS
    w = FOX_HEADS * HEAD_DIM
    grid = (batch, npair, nq)
    fill_shape, fill_specs, fill_scratch = _zero_fill_specs(fill)
    outs = pl.pallas_call(
        functools.partial(_fox_attn_kernel, grid=grid),
        out_shape=tuple([jax.ShapeDtypeStruct((t, D_FOX), F32)] + fill_shape),
        grid=grid,
        in_specs=[
            pl.BlockSpec((tq, w), lambda b, h, i: (b * nq + i, h)),
            pl.BlockSpec((seq, w), lambda b, h, i: (b, npair + h)),
            pl.BlockSpec((seq, w), lambda b, h, i: (b, 2 * npair + h)),
            pl.BlockSpec((None, None, nq, FOX_HEADS, tq), lambda b, h, i: (b, h, 0, 0, 0)),
        ],
        out_specs=tuple([pl.BlockSpec((tq, w), lambda b, h, i: (b * nq + i, h))] + fill_specs),
        scratch_shapes=fill_scratch,
        compiler_params=_params(("parallel", "parallel", "arbitrary")),
        name="fox_attn",
    )(qkv, qkv, qkv, crow)
    return outs if fill is not None else outs[0]


def _mla_attn_kernel(q_ref, k_ref, v_ref, o_ref):
    tq = q_ref.shape[0]
    qi = pl.program_id(2)
    dk = 2 * LANES
    scale = (MLA_NOPE + MLA_ROPE) ** -0.5
    qs = [q_ref[:, h * dk:(h + 1) * dk] for h in range(MLA_HEADS)]
    outs = _softmax_sweep(qs,
                          lambda h, ks: k_ref[pl.ds(ks, tq), h * dk:(h + 1) * dk],
                          lambda h, ks: v_ref[pl.ds(ks, tq), h * MLA_V:(h + 1) * MLA_V],
                          lambda h, kb: None, scale, qi, tq)
    o_ref[...] = jnp.concatenate(outs, axis=-1)


def _mla_attention(q, k, v, batch, seq):
    t = q.shape[0]
    tq = ATT_TILE
    nq = seq // tq
    dk = MLA_HEADS * 2 * LANES
    dv = MLA_HEADS * MLA_V
    return pl.pallas_call(
        _mla_attn_kernel,
        out_shape=jax.ShapeDtypeStruct((t, D_MLA), F32),
        grid=(batch, N_MLA // MLA_HEADS, nq),
        in_specs=[
            pl.BlockSpec((tq, dk), lambda b, h, i: (b * nq + i, h)),
            pl.BlockSpec((seq, dk), lambda b, h, i: (b, h)),
            pl.BlockSpec((seq, dv), lambda b, h, i: (b, h)),
        ],
        out_specs=pl.BlockSpec((tq, dv), lambda b, h, i: (b * nq + i, h)),
        compiler_params=_params(("parallel", "parallel", "arbitrary")),
        name="mla_attn",
    )(q, k, v)


def _sb_attn_kernel(q_ref, k_ref, v_ref, o_ref, *fill, grid):
    copies = _zero_fill_start(fill, grid)
    tq = q_ref.shape[0]
    qi = pl.program_id(2)
    q = q_ref[...] * jnp.asarray(HEAD_DIM ** -0.5, BF16)
    nh = SB_HEADS
    qs = [q[:, j * HEAD_DIM:(j + 1) * HEAD_DIM] for j in range(nh)]
    _, strict = _causal_mask(tq)
    row = lax.broadcasted_iota(jnp.int32, (tq, tq), 0)
    col = lax.broadcasted_iota(jnp.int32, (tq, tq), 1)
    later = (row > col).astype(BF16)

    def step(kb, carry, diag):
        ks = pl.multiple_of(kb * tq, tq)
        zs = [lax.dot_general(qs[h], k_ref[pl.ds(ks, tq), h * HEAD_DIM:(h + 1) * HEAD_DIM],
                              (((1,), (1,)), ((), ())), preferred_element_type=F32) for h in range(nh)]
        lss, lrs = [], []
        for h in range(nh):
            z = zs[h]
            ls = jnp.minimum(z, 0.0) - jnp.log(1.0 + jnp.exp(-jnp.abs(z)))
            lr = ls - z
            if diag:
                lr = jnp.where(strict, lr, 0.0)
            lss.append(ls)
            lrs.append(lr)
        afters = []
        for h in range(nh):
            hi = lrs[h].astype(BF16)
            lo = (lrs[h] - hi.astype(F32)).astype(BF16)
            afters.append((jnp.dot(hi, later, preferred_element_type=F32)
                           + jnp.dot(lo, later, preferred_element_type=F32)) + carry[h][0])
        out = []
        for h in range(nh):
            a = jnp.exp(lss[h] + afters[h])
            if diag:
                a = jnp.where(strict, a, 0.0)
            v = v_ref[pl.ds(ks, tq), h * HEAD_DIM:(h + 1) * HEAD_DIM]
            acc = carry[h][1] + jnp.dot(a.astype(BF16), v, preferred_element_type=F32)
            out.append((afters[h][:, 0:1] + lrs[h][:, 0:1], acc))
        return tuple(out)

    init = tuple((jnp.zeros((tq, 1), F32), jnp.zeros((tq, HEAD_DIM), F32)) for _ in range(nh))
    carry = step(qi, init, True)
    carry = lax.fori_loop(0, qi, lambda i, c: step(qi - 1 - i, c, False), carry)
    o_ref[...] = jnp.concatenate([acc for (_, acc) in carry], axis=-1)
    _zero_fill_wait(copies)


def _sb_attention(qkv, batch, seq, fill=None):
    t = qkv.shape[0]
    tq = ATT_TILE
    nq = seq // tq
    npair = N_SB // SB_HEADS
    w = SB_HEADS * HEAD_DIM
    base = 3 * D_FOX // w
    grid = (batch, npair, nq)
    fill_shape, fill_specs, fill_scratch = _zero_fill_specs(fill)
    outs = pl.pallas_call(
        functools.partial(_sb_attn_kernel, grid=grid),
        out_shape=tuple([jax.ShapeDtypeStruct((t, D_SB), F32)] + fill_shape),
        grid=grid,
        in_specs=[
            pl.BlockSpec((tq, w), lambda b, h, i: (b * nq + i, base + h)),
            pl.BlockSpec((seq, w), lambda b, h, i: (b, base + npair + h)),
            pl.BlockSpec((seq, w), lambda b, h, i: (b, base + 2 * npair + h)),
        ],
        out_specs=tuple([pl.BlockSpec((tq, w), lambda b, h, i: (b * nq + i, h))] + fill_specs),
        scratch_shapes=fill_scratch,
        compiler_params=_params(("parallel", "parallel", "arbitrary")),
        name="sb_attn",
    )(qkv, qkv, qkv)
    return outs if fill is not None else outs[0]


def _mla_prep_kernel(cq_ref, ckv_ref, kr1_ref, kr2_ref, pos_ref, gq_ref, gkv_ref, wq_ref, wkv_ref,
                     invf_ref, sgn_ref, q_out, k_out, v_out):
    ang = pos_ref[...].astype(F32) * invf_ref[...]
    cosv = jnp.cos(ang)
    sinv = jnp.sin(ang) * sgn_ref[...]
    qn = (_rms(cq_ref[...]) * gq_ref[...]).astype(BF16)
    qa = jnp.dot(qn, wq_ref[...], preferred_element_type=F32)
    for h in range(N_MLA):
        o = h * 3 * LANES
        pe = qa[:, o + LANES:o + 2 * LANES] * cosv + qa[:, o + 2 * LANES:o + 3 * LANES] * sinv
        q_out[:, h * 2 * LANES:h * 2 * LANES + LANES] = qa[:, o:o + LANES].astype(BF16)
        q_out[:, h * 2 * LANES + LANES:(h + 1) * 2 * LANES] = pe.astype(BF16)
    kvn = (_rms(ckv_ref[...]) * gkv_ref[...]).astype(BF16)
    kva = jnp.dot(kvn, wkv_ref[...], preferred_element_type=F32)
    kpe = (kr1_ref[...] * cosv + kr2_ref[...] * sinv).astype(BF16)
    for h in range(N_MLA):
        k_out[:, h * 2 * LANES:h * 2 * LANES + LANES] = kva[:, h * LANES:(h + 1) * LANES].astype(BF16)
        k_out[:, h * 2 * LANES + LANES:(h + 1) * 2 * LANES] = kpe
    v_out[...] = kva[:, N_MLA * MLA_NOPE:].astype(BF16)


def _mla_prep(misc, pos, gq, gkv, wq, wkv, tm=512):
    t = misc.shape[0]
    half = MLA_ROPE // 2
    inv_freq = ROPE_THETA ** (-jnp.arange(half, dtype=F32) / half)
    invf = jnp.tile(inv_freq, LANES // half).reshape(1, LANES)
    sgn = jnp.tile(jnp.concatenate([-jnp.ones((half,), F32), jnp.ones((half,), F32)]),
                   LANES // MLA_ROPE).reshape(1, LANES)
    const = lambda i: (0, 0)
    return pl.pallas_call(
        _mla_prep_kernel,
        out_shape=(jax.ShapeDtypeStruct((t, N_MLA * 2 * LANES), BF16),
                   jax.ShapeDtypeStruct((t, N_MLA * 2 * LANES), BF16),
                   jax.ShapeDtypeStruct((t, D_MLA), BF16)),
        grid=(t // tm,),
        in_specs=[
            pl.BlockSpec((tm, Q_LORA), lambda i: (i, MISC_CQ // Q_LORA)),
            pl.BlockSpec((tm, KV_LORA), lambda i: (i, MISC_CKV // KV_LORA)),
            pl.BlockSpec((tm, LANES), lambda i: (i, MISC_KR1 // LANES)),
            pl.BlockSpec((tm, LANES), lambda i: (i, MISC_KR2 // LANES)),
            pl.BlockSpec((tm, 1), lambda i: (i, 0)),
            pl.BlockSpec((1, Q_LORA), const),
            pl.BlockSpec((1, KV_LORA), const),
            pl.BlockSpec(wq.shape, const),
            pl.BlockSpec(wkv.shape, const),
            pl.BlockSpec((1, LANES), const),
            pl.BlockSpec((1, LANES), const),
        ],
        out_specs=(pl.BlockSpec((tm, N_MLA * 2 * LANES), lambda i: (i, 0)),
                   pl.BlockSpec((tm, N_MLA * 2 * LANES), lambda i: (i, 0)),
                   pl.BlockSpec((tm, D_MLA), lambda i: (i, 0))),
        compiler_params=_params(("parallel",)),
        name="mla_prep",
    )(misc, misc, misc, misc, pos, gq.reshape(1, -1), gkv.reshape(1, -1), wq, wkv, invf, sgn)


def _gelu(x):
    return 0.5 * x * (1.0 + lax.erf(x * np.float32(np.sqrt(0.5))))


def _sgu_kernel(u_ref, v_ref, lng_ref, lnb_ref, w_ref, bias_ref, o_ref):
    tm = u_ref.shape[0]
    v = _gelu(v_ref[...])
    mu = jnp.mean(v, axis=-1, keepdims=True)
    xc = v - mu
    var = jnp.mean(xc * xc, axis=-1, keepdims=True)
    vb = (xc * lax.rsqrt(var + LN_EPS) * lng_ref[...] + lnb_ref[...]).astype(BF16)
    row = lax.broadcasted_iota(jnp.int32, (CHUNK, CHUNK), 0)
    col = lax.broadcasted_iota(jnp.int32, (CHUNK, LANES), 1)
    tril = lax.broadcasted_iota(jnp.int32, (CHUNK, CHUNK), 1) <= row
    first = col < SGU_CH
    ws = [jnp.where(tril, w_ref[g], 0.0).astype(BF16) for g in range(N_SGU)]
    zero = jnp.zeros((CHUNK, LANES), BF16)
    for c in range(tm // CHUNK):
        rows = slice(c * CHUNK, (c + 1) * CHUNK)
        for p in range(N_SGU // 2):
            cols = slice(p * LANES, (p + 1) * LANES)
            vp = vb[rows, cols]
            mixed = (jnp.dot(ws[2 * p], jnp.where(first, vp, zero), preferred_element_type=F32)
                     + jnp.dot(ws[2 * p + 1], jnp.where(first, zero, vp), preferred_element_type=F32))
            o_ref[rows, cols] = _gelu(u_ref[rows, cols]) * (mixed + bias_ref[:, cols])


def _sgu(misc, ln_g, ln_b, w_s, b_s, tm=512):
    t = misc.shape[0]
    bias = jnp.repeat(b_s.T, SGU_CH, axis=1)
    const2 = lambda i: (0, 0)
    return pl.pallas_call(
        _sgu_kernel,
        out_shape=jax.ShapeDtypeStruct((t, D_SGU), F32),
        grid=(t // tm,),
        in_specs=[
            pl.BlockSpec((tm, D_SGU), lambda i: (i, MISC_U // D_SGU)),
            pl.BlockSpec((tm, D_SGU), lambda i: (i, MISC_V // D_SGU)),
            pl.BlockSpec((1, D_SGU), const2),
            pl.BlockSpec((1, D_SGU), const2),
            pl.BlockSpec((N_SGU, CHUNK, CHUNK), lambda i: (0, 0, 0)),
            pl.BlockSpec((CHUNK, D_SGU), const2),
        ],
        out_specs=pl.BlockSpec((tm, D_SGU), lambda i: (i, 0)),
        compiler_params=_params(("parallel",)),
        name="sgu",
    )(misc, misc, ln_g.reshape(1, -1), ln_b.reshape(1, -1), w_s, bias)


def _pack_bf16_pairs(hb):
    n = hb.shape[1] // 2
    bits = pltpu.bitcast(hb.astype(F32), jnp.uint32)
    return (bits[:, n:] & jnp.uint32(0xFFFF0000)) | (bits[:, :n] >> 16)


def _unpack_bf16_pairs(xu):
    lo = pltpu.bitcast(xu << 16, F32).astype(BF16)
    hi = pltpu.bitcast(xu & jnp.uint32(0xFFFF0000), F32).astype(BF16)
    return jnp.concatenate([lo, hi], axis=1)


def _outproj_router_kernel(of_ref, os_ref, om_ref, og_ref, x_ref, ong_ref, wo_ref, nfg_ref, rw_ref,
                           rb_ref, xo_ref, h2_ref, idx_ref, gate_ref, rank_ref, cnt_ref, carry_ref):
    tm = x_ref.shape[0]
    th = tm // 2
    halves = [slice(h * th, (h + 1) * th) for h in range(2)]

    @pl.when(pl.program_id(0) == 0)
    def _():
        carry_ref[...] = jnp.zeros_like(carry_ref)

    os_ = []
    for rows in halves:
        o = jnp.concatenate([_rms(r[rows, :]) for r in (of_ref, os_ref, om_ref, og_ref)], axis=-1)
        os_.append((o * ong_ref[...]).astype(BF16))
    xns = [x_ref[rows, :] + jnp.dot(o, wo_ref[...], preferred_element_type=F32) for rows, o in zip(halves, os_)]
    hbs = []
    for rows, xn in zip(halves, xns):
        xo_ref[rows, :] = xn
        hb = (_rms(xn) * nfg_ref[...]).astype(BF16)
        h2_ref[rows, :] = _pack_bf16_pairs(hb)
        hbs.append(hb)

    lane = lax.broadcasted_iota(jnp.int32, (th, LANES), 1)
    all_logits = [jnp.dot(hb, rw_ref[...], preferred_element_type=F32) + rb_ref[...] for hb in hbs]
    picks = []
    for logits in all_logits:
        vals = jnp.where(lane < N_EXPERTS, logits, -jnp.inf)
        sels, tops = [], []
        for _ in range(TOP_K):
            m = jnp.max(vals, axis=-1, keepdims=True)
            idx = jnp.min(jnp.where(vals == m, lane, LANES), axis=-1, keepdims=True)
            sel = lane == idx
            vals = jnp.where(sel, -jnp.inf, vals)
            sels.append(sel)
            tops.append((m, idx))
        es = [jnp.exp(m - tops[0][0]) for (m, _) in tops]
        den = es[0] + es[1] + es[2] + es[3]
        mh = jnp.where(sels[0] | sels[1] | sels[2] | sels[3], 1.0, 0.0)
        picks.append((sels, tops, es, den, mh))

    r = lax.broadcasted_iota(jnp.int32, (th, th), 0)
    c = lax.broadcasted_iota(jnp.int32, (th, th), 1)
    before = (c < r).astype(BF16)
    parts = [jnp.dot(before, p[4].astype(BF16), preferred_element_type=F32) for p in picks]
    total = carry_ref[...]
    for rows, part, (sels, tops, es, den, mh) in zip(halves, parts, picks):
        cnt = part + total
        total = cnt[th - 1:th, :] + mh[th - 1:th, :]
        idx_o = jnp.zeros((th, LANES), jnp.int32)
        gate_o = jnp.zeros((th, LANES), F32)
        rank_o = jnp.zeros((th, LANES), jnp.int32)
        for k in range(TOP_K):
            rank_k = jnp.sum(jnp.where(sels[k], cnt, 0.0), axis=-1, keepdims=True).astype(jnp.int32)
            idx_o = jnp.where(lane == k, tops[k][1], idx_o)
            gate_o = jnp.where(lane == k, es[k] / den, gate_o)
            rank_o = jnp.where(lane == k, rank_k, rank_o)
        idx_ref[rows, :] = idx_o
        gate_ref[rows, :] = gate_o
        rank_ref[rows, :] = rank_o
    carry_ref[...] = total
    cnt_ref[...] = jnp.broadcast_to(total, cnt_ref.shape).astype(jnp.int32)


def _outproj_router(o_fox, o_sb, o_mla, o_sgu, x, out_norm_g, w_o, norm_ffn_g, router_w, router_b, tm=512):
    t, d = x.shape
    dg = o_fox.shape[1]
    rw = jnp.zeros((d, LANES), BF16).at[:, :N_EXPERTS].set(router_w.astype(BF16))
    rb = jnp.zeros((1, LANES), F32).at[0, :N_EXPERTS].set(router_b)
    const = lambda i: (0, 0)
    grp = pl.BlockSpec((tm, dg), lambda i: (i, 0))
    lanes_out = pl.BlockSpec((tm, LANES), lambda i: (i, 0))
    return pl.pallas_call(
        _outproj_router_kernel,
        out_shape=(jax.ShapeDtypeStruct((t, d), F32),
                   jax.ShapeDtypeStruct((t, d // 2), jnp.uint32),
                   jax.ShapeDtypeStruct((t, LANES), jnp.int32),
                   jax.ShapeDtypeStruct((t, LANES), F32),
                   jax.ShapeDtypeStruct((t, LANES), jnp.int32),
                   jax.ShapeDtypeStruct((8, LANES), jnp.int32)),
        grid=(t // tm,),
        in_specs=[grp, grp, grp, grp,
                  pl.BlockSpec((tm, d), lambda i: (i, 0)),
                  pl.BlockSpec((1, 4 * dg), const),
                  pl.BlockSpec((4 * dg, d), const),
                  pl.BlockSpec((1, d), const),
                  pl.BlockSpec((d, LANES), const),
                  pl.BlockSpec((1, LANES), const)],
        out_specs=(pl.BlockSpec((tm, d), lambda i: (i, 0)),
                   pl.BlockSpec((tm, d // 2), lambda i: (i, 0)),
                   lanes_out, lanes_out, lanes_out,
                   pl.BlockSpec((8, LANES), const)),
        scratch_shapes=[pltpu.VMEM((1, LANES), F32)],
        compiler_params=_params(("arbitrary",)),
        name="outproj_router",
    )(o_fox, o_sb, o_mla, o_sgu, x, out_norm_g.reshape(1, -1), w_o, norm_ffn_g.reshape(1, -1), rw, rb)


def _scatter_rows_kernel(dest_hbm, x_ref, xs_init_hbm, xs_hbm, dest0, dest1, sems):
    del xs_init_hbm
    dest_smem = (dest0, dest1)
    i = pl.program_id(0)
    nsteps = pl.num_programs(0)
    tm = x_ref.shape[0]
    n = tm * TOP_K

    def idx_copy(step, sl):
        return pltpu.make_async_copy(dest_hbm.at[pl.ds(step * n, n)], dest_smem[sl], sems.at[sl])

    @pl.when(i == 0)
    def _():
        idx_copy(0, 0).start()

    for sl in range(2):
        @pl.when(i % 2 == sl)
        def _(sl=sl):
            @pl.when(i + 1 < nsteps)
            def _():
                idx_copy(i + 1, 1 - sl).start()

            idx_copy(i, sl).wait()

            def issue(t, _):
                for k in range(TOP_K):
                    pltpu.make_async_copy(x_ref.at[pl.ds(t, 1)],
                                          xs_hbm.at[pl.ds(dest_smem[sl][t * TOP_K + k], 1)], sems.at[2]).start()
                return 0

            lax.fori_loop(0, tm, issue, 0, unroll=2)

    for _ in range(TOP_K):
        pltpu.make_async_copy(x_ref, xs_hbm.at[pl.ds(0, tm)], sems.at[2]).wait()


def _scatter_rows(dest_flat, src, nrows, xs_init):
    t, w = src.shape
    tm = SCATTER_TM
    assert xs_init.shape == (nrows, w) and xs_init.dtype == src.dtype
    return pl.pallas_call(
        _scatter_rows_kernel,
        out_shape=jax.ShapeDtypeStruct((nrows, w), src.dtype),
        grid=(t // tm,),
        in_specs=[pl.BlockSpec(memory_space=pl.ANY),
                  pl.BlockSpec((tm, w), lambda i: (i, 0)),
                  pl.BlockSpec(memory_space=pl.ANY)],
        out_specs=pl.BlockSpec(memory_space=pl.ANY),
        scratch_shapes=[pltpu.SMEM((tm * TOP_K,), jnp.int32), pltpu.SMEM((tm * TOP_K,), jnp.int32),
                        pltpu.SemaphoreType.DMA((3,))],
        input_output_aliases={2: 0},
        compiler_params=_params(("arbitrary",)),
        name="scatter_rows",
    )(dest_flat, src, xs_init)


def _expert_kernel(sbe_ref, nval_ref, nused_ref, x_ref, wg_ref, bg_ref, wu_ref, bu_ref, wd_ref, bd_ref,
                   y_init_hbm, y_ref):
    del sbe_ref, nused_ref, y_init_hbm
    s = pl.program_id(0)
    f = pl.program_id(1)
    d = y_ref.shape[1]

    @pl.when(f == 0)
    def _():
        y_ref[...] = jnp.broadcast_to(bd_ref[...], y_ref.shape)

    nsub = (nval_ref[s] + EXP_SUB - 1) // EXP_SUB

    for n in range(1, EXP_ROWS // EXP_SUB + 1):
        @pl.when(nsub == n)
        def _(m=n * EXP_SUB):
            x = _unpack_bf16_pairs(x_ref[0:m, :])
            g = jnp.dot(x, wg_ref[...].astype(BF16), preferred_element_type=F32) + bg_ref[...]
            g = jnp.minimum(g, SWIGLU_LIMIT)
            u = jnp.dot(x, wu_ref[...].astype(BF16), preferred_element_type=F32) + bu_ref[...]
            u = jnp.clip(u, -SWIGLU_LIMIT, SWIGLU_LIMIT)
            a = (g * jax.nn.sigmoid(SWIGLU_ALPHA * g) * (u + 1.0)).astype(BF16)
            wd = wd_ref[...].astype(BF16)
            for c in range(d // EXP_DN):
                cols = slice(c * EXP_DN, (c + 1) * EXP_DN)
                y_ref[0:m, cols] += jnp.dot(a, wd[:, cols], preferred_element_type=F32)


def _experts(sb_expert, sb_nvalid, n_used, xs, layer, w_gate, b_gate, w_up, b_up, w_down, b_down, nsb_max, y_init):
    d = w_gate.shape[2]
    dff = w_gate.shape[3]
    assert d % EXP_DN == 0 and dff % EXP_FF == 0 and EXP_ROWS % EXP_SUB == 0
    assert y_init.shape == (nsb_max * EXP_ROWS, d) and y_init.dtype == F32
    nf = dff // EXP_FF

    def xmap(s, f, sbe, nval, nused):
        return (s, 0)

    def ff(s, f, nused):
        return f

    b_gate4 = b_gate.reshape(b_gate.shape[0], N_EXPERTS, 1, dff)
    b_up4 = b_up.reshape(b_up.shape[0], N_EXPERTS, 1, dff)
    b_down4 = b_down.reshape(b_down.shape[0], N_EXPERTS, 1, d)
    y_init_operand = 3 + 7
    return pl.pallas_call(
        _expert_kernel,
        out_shape=jax.ShapeDtypeStruct((nsb_max * EXP_ROWS, d), F32),
        grid_spec=pltpu.PrefetchScalarGridSpec(
            num_scalar_prefetch=3,
            grid=(n_used[0], nf),
            in_specs=[
                pl.BlockSpec((EXP_ROWS, d // 2), xmap),
                pl.BlockSpec((None, None, d, EXP_FF), lambda s, f, sbe, nval, nused: (layer, sbe[s], 0, ff(s, f, nused))),
                pl.BlockSpec((None, None, 1, EXP_FF), lambda s, f, sbe, nval, nused: (layer, sbe[s], 0, ff(s, f, nused))),
                pl.BlockSpec((None, None, d, EXP_FF), lambda s, f, sbe, nval, nused: (layer, sbe[s], 0, ff(s, f, nused))),
                pl.BlockSpec((None, None, 1, EXP_FF), lambda s, f, sbe, nval, nused: (layer, sbe[s], 0, ff(s, f, nused))),
                pl.BlockSpec((None, None, EXP_FF, d), lambda s, f, sbe, nval, nused: (layer, sbe[s], ff(s, f, nused), 0)),
                pl.BlockSpec((None, None, 1, d), lambda s, f, sbe, nval, nused: (layer, sbe[s], 0, 0)),
                pl.BlockSpec(memory_space=pl.ANY),
            ],
            out_specs=pl.BlockSpec((EXP_ROWS, d), xmap),
        ),
        input_output_aliases={y_init_operand: 0},
        compiler_params=_params(("arbitrary", "arbitrary")),
        name="experts",
    )(sb_expert, sb_nvalid, n_used, xs, w_gate, b_gate4, w_up, b_up4, w_down, b_down4, y_init)


def _combine_kernel(dest_hbm, y_hbm, x_ref, gate_ref, fg_ref, o_ref, dest0, dest1, ybuf0, ybuf1, sems, *,
                    final_norm):
    dest_smem = (dest0, dest1)
    ybuf = (ybuf0, ybuf1)
    i = pl.program_id(0)
    nsteps = pl.num_programs(0)
    tm = x_ref.shape[0]
    n = tm * TOP_K
    slot = i % 2
    nslot = 1 - slot

    def idx_copy(step, sl):
        return pltpu.make_async_copy(dest_hbm.at[pl.ds(step * n, n)], dest_smem[sl], sems.at[sl])

    def issue_rows(sl):
        def issue(t, _):
            for k in range(TOP_K):
                pltpu.make_async_copy(y_hbm.at[pl.ds(dest_smem[sl][t * TOP_K + k], 1)],
                                      ybuf[sl].at[pl.ds(k * tm + t, 1)], sems.at[2 + sl]).start()
            return 0

        lax.fori_loop(0, tm, issue, 0, unroll=2)

    @pl.when(i == 0)
    def _():
        first = idx_copy(0, 0)
        first.start()
        first.wait()
        issue_rows(0)

        @pl.when(nsteps > 1)
        def _():
            idx_copy(1, 1).start()

    for sl in range(2):
        @pl.when(jnp.logical_and(i + 1 < nsteps, nslot == sl))
        def _(sl=sl):
            idx_copy(i + 1, sl).wait()
            issue_rows(sl)

    for sl in range(2):
        @pl.when(slot == sl)
        def _(sl=sl):
            @pl.when(i + 2 < nsteps)
            def _():
                idx_copy(i + 2, sl).start()

            pltpu.make_async_copy(y_hbm.at[pl.ds(0, n)], ybuf[sl], sems.at[2 + sl]).wait()
            gate = gate_ref[...]
            acc = gate[:, 0:1] * ybuf[sl][0:tm]
            for k in range(1, TOP_K):
                acc = acc + gate[:, k:k + 1] * ybuf[sl][k * tm:(k + 1) * tm]
            out = x_ref[...] + acc
            if final_norm:
                out = _rms(out) * fg_ref[...]
            o_ref[...] = out


def _combine(dest_flat, y, x, gates, final_g, final_norm):
    t, d = x.shape
    tm = COMBINE_TM
    return pl.pallas_call(
        functools.partial(_combine_kernel, final_norm=final_norm),
        out_shape=jax.ShapeDtypeStruct((t, d), F32),
        grid=(t // tm,),
        in_specs=[
            pl.BlockSpec(memory_space=pl.ANY),
            pl.BlockSpec(memory_space=pl.ANY),
            pl.BlockSpec((tm, d), lambda i: (i, 0)),
            pl.BlockSpec((tm, LANES), lambda i: (i, 0)),
            pl.BlockSpec((1, d), lambda i: (0, 0)),
        ],
        out_specs=pl.BlockSpec((tm, d), lambda i: (i, 0)),
        scratch_shapes=[pltpu.SMEM((tm * TOP_K,), jnp.int32), pltpu.SMEM((tm * TOP_K,), jnp.int32),
                        pltpu.VMEM((TOP_K * tm, d), F32), pltpu.VMEM((TOP_K * tm, d), F32),
                        pltpu.SemaphoreType.DMA((4,))],
        compiler_params=_params(("arbitrary",)),
        name="combine",
    )(dest_flat, y, x, gates, final_g.reshape(1, d))


def _in_proj_weights(w_in):
    d = w_in.shape[0]
    o_gate = 3 * D_FOX
    o_sb = o_gate + N_FOX
    o_cq = o_sb + 3 * D_SB
    o_ckv = o_cq + Q_LORA
    o_kr = o_ckv + KV_LORA
    o_sgu = o_kr + MLA_ROPE
    half = MLA_ROPE // 2
    w_qkv = jnp.concatenate([w_in[:, :o_gate], w_in[:, o_sb:o_cq]], axis=1).astype(BF16)
    kr = w_in[:, o_kr:o_sgu]
    zpad = jnp.zeros((d, LANES - MLA_ROPE), w_in.dtype)
    w_misc = jnp.concatenate([
        w_in[:, o_sgu:o_sgu + 2 * D_SGU],
        w_in[:, o_cq:o_ckv],
        w_in[:, o_ckv:o_kr],
        kr, zpad,
        kr[:, half:], kr[:, :half], zpad,
        w_in[:, o_gate:o_sb], jnp.zeros((d, LANES - N_FOX), w_in.dtype),
    ], axis=1).astype(BF16)
    assert w_misc.shape[1] == MISC_W
    return w_qkv, w_misc


def _mla_weights(w_q_b, w_kv_b):
    half = MLA_ROPE // 2
    wq = w_q_b.reshape(Q_LORA, N_MLA, MLA_NOPE + MLA_ROPE)
    x1 = wq[:, :, MLA_NOPE:MLA_NOPE + half]
    x2 = wq[:, :, MLA_NOPE + half:]
    z = jnp.zeros((Q_LORA, N_MLA, LANES - MLA_ROPE), w_q_b.dtype)
    wq_r = jnp.concatenate([wq[:, :, :MLA_NOPE], x1, x2, z, x2, x1, z], axis=2)
    wq_r = wq_r.reshape(Q_LORA, N_MLA * 3 * LANES).astype(BF16)
    wkv = w_kv_b.reshape(KV_LORA, N_MLA, MLA_NOPE + MLA_V)
    wkv_r = jnp.concatenate([wkv[:, :, :MLA_NOPE].reshape(KV_LORA, -1),
                             wkv[:, :, MLA_NOPE:].reshape(KV_LORA, -1)], axis=1).astype(BF16)
    return wq_r, wkv_r


def _routing_tables(idx, rank, cnt, n_tokens):
    nsb_max = N_EXPERTS + (n_tokens * TOP_K) // EXP_ROWS
    counts = cnt[0, :N_EXPERTS]
    nsb_e = (counts + EXP_ROWS - 1) // EXP_ROWS
    sb_end = jnp.cumsum(nsb_e)
    sb_start = sb_end - nsb_e
    n_used = sb_end[-1]
    top_idx = idx[:, :TOP_K].reshape(-1)
    dest = (sb_start * EXP_ROWS)[top_idx] + rank[:, :TOP_K].reshape(-1)
    s_ids = jnp.arange(nsb_max, dtype=jnp.int32)
    sb_e = jnp.minimum(jnp.sum(s_ids[:, None] >= sb_end[None, :], axis=1), N_EXPERTS - 1).astype(jnp.int32)
    last_e = sb_e[jnp.maximum(n_used - 1, 0)]
    sb_e = jnp.where(s_ids < n_used, sb_e, last_e)
    nval = jnp.clip(counts[sb_e] - (s_ids - sb_start[sb_e]) * EXP_ROWS, 0, EXP_ROWS)
    nval = jnp.where(s_ids < n_used, nval, 0).astype(jnp.int32)
    return dest.astype(jnp.int32), sb_e, nval, n_used.reshape(1).astype(jnp.int32), nsb_max


def _layer(x, pos, l, p, final_g, final_norm, batch, seq, moe_bufs):
    t = x.shape[0]
    w_qkv, w_misc = _in_proj_weights(p["w_in"][l])
    qkv = _norm_matmul(x, p["norm_mix_g"][l], w_qkv, BF16, 512, 1536, "in_proj_qkv")
    misc = _norm_matmul(x, p["norm_mix_g"][l], w_misc, F32, 512, MISC_W, "in_proj_misc")

    crow = _fox_prep(misc, p["b_forget"][l], batch, seq)
    nkb = seq // ATT_TILE
    crow = crow.reshape(batch, nkb, N_FOX // FOX_HEADS, FOX_HEADS, ATT_TILE).transpose(0, 2, 1, 3, 4)
    if moe_bufs is None:
        nrows = (N_EXPERTS + (t * TOP_K) // EXP_ROWS) * EXP_ROWS
        o_fox, xs_prev = _fox_attention(qkv, crow, batch, seq,
                                        jax.ShapeDtypeStruct((nrows, x.shape[1] // 2), jnp.uint32))
        o_sb, y_prev = _sb_attention(qkv, batch, seq, jax.ShapeDtypeStruct((nrows, x.shape[1]), F32))
    else:
        xs_prev, y_prev = moe_bufs
        o_fox = _fox_attention(qkv, crow, batch, seq)
        o_sb = _sb_attention(qkv, batch, seq)
    wq_r, wkv_r = _mla_weights(p["mla_w_q_b"][l], p["mla_w_kv_b"][l])
    q_m, k_m, v_m = _mla_prep(misc, pos, p["mla_q_norm_g"][l], p["mla_kv_norm_g"][l], wq_r, wkv_r)
    o_mla = _mla_attention(q_m, k_m, v_m, batch, seq)
    o_sgu = _sgu(misc, p["sgu_ln_g"][l], p["sgu_ln_b"][l], p["sgu_w_s"][l], p["sgu_b_s"][l])

    x_new, h2p, idx, gates, rank, cnt = _outproj_router(
        o_fox, o_sb, o_mla, o_sgu, x, p["out_norm_g"][l], p["w_o"][l].astype(BF16),
        p["norm_ffn_g"][l], p["router_w"][l], p["router_b"][l])

    dest, sb_e, nval, n_used, nsb_max = _routing_tables(idx, rank, cnt, t)
    xs = _scatter_rows(dest, h2p, nsb_max * EXP_ROWS, xs_prev)
    y = _experts(sb_e, nval, n_used, xs, l, p["w_gate"], p["b_gate"], p["w_up"], p["b_up"],
                 p["w_down"], p["b_down"], nsb_max, y_prev)
    return _combine(dest, y, x_new, gates, final_g, final_norm), (xs, y)


def kernel(x, positions, norm_mix_g, w_in, b_forget, mla_q_norm_g, mla_kv_norm_g, mla_w_q_b, mla_w_kv_b,
           sgu_ln_g, sgu_ln_b, sgu_w_s, sgu_b_s, out_norm_g, w_o, norm_ffn_g, router_w, router_b, w_gate,
           b_gate, w_up, b_up, w_down, b_down, final_norm_g):
    batch, seq, d = x.shape
    depth = w_in.shape[0]
    p = dict(norm_mix_g=norm_mix_g, w_in=w_in, b_forget=b_forget, mla_q_norm_g=mla_q_norm_g,
             mla_kv_norm_g=mla_kv_norm_g, mla_w_q_b=mla_w_q_b, mla_w_kv_b=mla_w_kv_b, sgu_ln_g=sgu_ln_g,
             sgu_ln_b=sgu_ln_b, sgu_w_s=sgu_w_s, sgu_b_s=sgu_b_s, out_norm_g=out_norm_g, w_o=w_o,
             norm_ffn_g=norm_ffn_g, router_w=router_w, router_b=router_b, w_gate=w_gate, b_gate=b_gate,
             w_up=w_up, b_up=b_up, w_down=w_down, b_down=b_down)
    h = x.reshape(batch * seq, d)
    pos = positions.reshape(batch * seq, 1)
    moe_bufs = None
    for l in range(depth):
        h, moe_bufs = _layer(h, pos, l, p, final_norm_g, l == depth - 1, batch, seq, moe_bufs)
    return h.reshape(batch, seq, d)
```

```python
import functools

import numpy as np
import jax
import jax.numpy as jnp
from jax import lax
from jax.experimental import pallas as pl
from jax.experimental.pallas import tpu as pltpu

HEAD_DIM = 64
N_FOX = 8
N_SB = 8
N_MLA = 4
MLA_NOPE = 128
MLA_ROPE = 64
MLA_V = 128
Q_LORA = 512
KV_LORA = 256
N_SGU = 8
SGU_CH = 64
CHUNK = 128
ROPE_THETA = 10000.0
D_FOX = N_FOX * HEAD_DIM
D_SB = N_SB * HEAD_DIM
D_MLA = N_MLA * MLA_V
D_SGU = N_SGU * SGU_CH
N_EXPERTS = 32
TOP_K = 4
SWIGLU_LIMIT = 7.0
SWIGLU_ALPHA = 1.702
RMS_EPS = 1e-6
LN_EPS = 1e-5

LANES = 128
VMEM_LIMIT = 56 * 1024 * 1024

ATT_TILE = 256
FOX_HEADS = 8
MLA_HEADS = 4
SB_HEADS = 8
EXP_ROWS = 1536
EXP_SUB = 256
EXP_FF = 256
EXP_DN = 512
SCATTER_TM = 512
ZERO_ROWS = 512
COMBINE_TM = 256

F32 = jnp.float32
BF16 = jnp.bfloat16

MISC_U = 0
MISC_V = 512
MISC_CQ = 1024
MISC_CKV = 1536
MISC_KR1 = 1792
MISC_KR2 = 1920
MISC_GATE = 2048
MISC_W = 2176


def _params(sem, vmem=VMEM_LIMIT):
    return pltpu.CompilerParams(dimension_semantics=sem, vmem_limit_bytes=vmem)


def _split3(x):
    hi = x.astype(BF16)
    r1 = x - hi.astype(F32)
    mid = r1.astype(BF16)
    lo = (r1 - mid.astype(F32)).astype(BF16)
    return hi, mid, lo


def _log_sigmoid_pair(z):
    t = jnp.log1p(jnp.exp(-jnp.abs(z)))
    return jnp.minimum(z, 0.0) - t, jnp.minimum(-z, 0.0) - t


def _rms(x):
    return x * lax.rsqrt(jnp.mean(x * x, axis=-1, keepdims=True) + RMS_EPS)


def _norm_matmul_kernel(x_ref, g_ref, w_ref, o_ref, xn_ref):
    tm = x_ref.shape[0]

    @pl.when(pl.program_id(1) == 0)
    def _():
        for rows in (slice(0, tm // 2), slice(tm // 2, tm)):
            xn = (_rms(x_ref[rows, :]) * g_ref[...]).astype(BF16)
            xn_ref[rows, :] = xn
            o_ref[rows, :] = jnp.dot(xn, w_ref[...], preferred_element_type=F32).astype(o_ref.dtype)

    @pl.when(pl.program_id(1) != 0)
    def _():
        o_ref[...] = jnp.dot(xn_ref[...], w_ref[...], preferred_element_type=F32).astype(o_ref.dtype)


def _norm_matmul(x, g, w, out_dtype, tm, tn, name):
    t, d = x.shape
    n = w.shape[1]
    return pl.pallas_call(
        _norm_matmul_kernel,
        out_shape=jax.ShapeDtypeStruct((t, n), out_dtype),
        grid=(t // tm, n // tn),
        in_specs=[
            pl.BlockSpec((tm, d), lambda i, j: (i, 0)),
            pl.BlockSpec((1, d), lambda i, j: (0, 0)),
            pl.BlockSpec((d, tn), lambda i, j: (0, j)),
        ],
        out_specs=pl.BlockSpec((tm, tn), lambda i, j: (i, j)),
        scratch_shapes=[pltpu.VMEM((tm, d), BF16)],
        compiler_params=_params(("parallel", "arbitrary")),
        name=name,
    )(x, g.reshape(1, d), w)


def _fox_prep_kernel(gate_ref, bf_ref, crow_ref):
    s_len = gate_ref.shape[0]
    bl = ATT_TILE
    r = lax.broadcasted_iota(jnp.int32, (bl, bl), 0)
    c = lax.broadcasted_iota(jnp.int32, (bl, bl), 1)
    tri = (c <= r).astype(BF16)
    carry = jnp.zeros((1, LANES), F32)
    for i in range(s_len // bl):
        z = gate_ref[i * bl:(i + 1) * bl, :] + bf_ref[...]
        lf, _ = _log_sigmoid_pair(z)
        hi, mid, lo = _split3(lf)
        cs = (jnp.dot(tri, hi, preferred_element_type=F32)
              + jnp.dot(tri, mid, preferred_element_type=F32)
              + jnp.dot(tri, lo, preferred_element_type=F32)) + carry
        crow_ref[i] = cs.T[0:N_FOX, :]
        carry = cs[bl - 1:bl, :]


def _fox_prep(misc, b_forget, batch, seq):
    nkb = seq // ATT_TILE
    bf = jnp.zeros((1, LANES), F32).at[0, :N_FOX].set(b_forget)
    return pl.pallas_call(
        _fox_prep_kernel,
        out_shape=jax.ShapeDtypeStruct((batch, nkb, N_FOX, ATT_TILE), F32),
        grid=(batch,),
        in_specs=[
            pl.BlockSpec((seq, LANES), lambda b: (b, MISC_GATE // LANES)),
            pl.BlockSpec((1, LANES), lambda b: (0, 0)),
        ],
        out_specs=pl.BlockSpec((None, nkb, N_FOX, ATT_TILE), lambda b: (b, 0, 0, 0)),
        compiler_params=_params(("parallel",)),
        name="fox_prep",
    )(misc, bf)


def _causal_mask(tq):
    row = lax.broadcasted_iota(jnp.int32, (tq, tq), 0)
    col = lax.broadcasted_iota(jnp.int32, (tq, tq), 1)
    return col <= row, col < row


def _softmax_sweep(qs, k_of, v_of, bias_of, scale, qi, tq, ones_lane=None):
    nh = len(qs)
    incl, _ = _causal_mask(tq)

    def step(kb, carry, diag):
        ks = pl.multiple_of(kb * tq, tq)
        ss = [lax.dot_general(qs[h], k_of(h, ks), (((1,), (1,)), ((), ())), preferred_element_type=F32)
              for h in range(nh)]
        ps, stats = [], []
        for h in range(nh):
            m, l, _ = carry[h]
            s = ss[h]
            if scale is not None:
                s = s * scale
            b = bias_of(h, kb)
            if b is not None:
                s = s + b
            if diag:
                s = jnp.where(incl, s, -jnp.inf)
            m_new = jnp.maximum(m, jnp.max(s, axis=-1, keepdims=True))
            alpha = jnp.exp(m - m_new)
            p = jnp.exp(s - m_new)
            if ones_lane is None:
                l = alpha * l + jnp.sum(p, axis=-1, keepdims=True)
            stats.append((m_new, l, alpha))
            ps.append(p.astype(BF16))
        out = []
        for h in range(nh):
            m_new, l, alpha = stats[h]
            acc = alpha * carry[h][2] + jnp.dot(ps[h], v_of(h, ks), preferred_element_type=F32)
            out.append((m_new, l, acc))
        return tuple(out)

    dv = v_of(0, 0).shape[-1]
    l0 = jnp.zeros((tq, 1), F32) if ones_lane is None else None
    init = tuple((jnp.full((tq, 1), -jnp.inf, F32), l0, jnp.zeros((tq, dv), F32)) for _ in range(nh))
    carry = lax.fori_loop(0, qi, lambda kb, c: step(kb, c, False), init)
    carry = step(qi, carry, True)
    if ones_lane is None:
        return [acc / l for (_, l, acc) in carry]
    return [acc / acc[:, ones_lane(h):ones_lane(h) + 1] for h, (_, _, acc) in enumerate(carry)]


def _zero_fill_copies(fill_hbm, zeros, sem, grid):
    nq = grid[2]
    ngroups = grid[0] * grid[1]
    rows = zeros.shape[0]
    nblocks = fill_hbm.shape[0] // rows
    per_group = (nblocks + ngroups - 1) // ngroups
    work = nq * (nq + 1) // 2
    max_per_step = (per_group * nq + work - 1) // work + 1
    group = pl.program_id(0) * grid[1] + pl.program_id(1)
    qi = pl.program_id(2)
    first = per_group * (qi * (qi + 1) // 2) // work
    count = per_group * ((qi + 1) * (qi + 2) // 2) // work - first
    copies = []
    for j in range(max_per_step):
        block = group * per_group + first + j
        valid = jnp.logical_and(j < count, block < nblocks)
        at = jnp.minimum(block, nblocks - 1) * rows
        copies.append((valid, pltpu.make_async_copy(zeros, fill_hbm.at[pl.ds(at, rows)], sem)))
    return copies


def _zero_fill_start(fill, grid):
    if not fill:
        return []
    fill_hbm, zeros, sem = fill
    zeros[...] = jnp.zeros_like(zeros)
    copies = _zero_fill_copies(fill_hbm, zeros, sem, grid)
    for valid, cp in copies:
        @pl.when(valid)
        def _(cp=cp):
            cp.start()
    return copies


def _zero_fill_wait(copies):
    for valid, cp in copies:
        @pl.when(valid)
        def _(cp=cp):
            cp.wait()


def _zero_fill_specs(fill):
    if fill is None:
        return [], [], []
    assert fill.shape[0] % ZERO_ROWS == 0
    return ([fill], [pl.BlockSpec(memory_space=pl.ANY)],
            [pltpu.VMEM((ZERO_ROWS, fill.shape[1]), fill.dtype), pltpu.SemaphoreType.DMA(())])


def _fox_attn_kernel(q_ref, k_ref, v_ref, crow_ref, o_ref, *fill, grid):
    copies = _zero_fill_start(fill, grid)
    tq = q_ref.shape[0]
    qi = pl.program_id(2)
    q = q_ref[...] * jnp.asarray(HEAD_DIM ** -0.5, BF16)
    qs = [q[:, j * HEAD_DIM:(j + 1) * HEAD_DIM] for j in range(FOX_HEADS)]

    def k_of(h, ks):
        return k_ref[pl.ds(ks, tq), h * HEAD_DIM:(h + 1) * HEAD_DIM]

    first = lax.broadcasted_iota(jnp.int32, (tq, LANES), 1) < HEAD_DIM
    one = jnp.ones((tq, LANES), BF16)

    def v_of(h, ks):
        pair = v_ref[pl.ds(ks, tq), (h // 2) * LANES:(h // 2 + 1) * LANES]
        return jnp.where(first, pair, one) if h % 2 == 0 else jnp.where(first, one, pair)

    def bias_of(h, kb):
        return -crow_ref[kb][h:h + 1, :]

    outs = _softmax_sweep(qs, k_of, v_of, bias_of, None, qi, tq,
                          ones_lane=lambda h: HEAD_DIM if h % 2 == 0 else 0)
    o_ref[...] = jnp.concatenate([jnp.where(first, outs[h], outs[h + 1]) for h in range(0, FOX_HEADS, 2)],
                                 axis=-1)
    _zero_fill_wait(copies)


def _fox_attention(qkv, crow, batch, seq, fill=None):
    t = qkv.shape[0]
    tq = ATT_TILE
    nq = seq // tq
    npair = N_FOX // FOX_HEADS
    w = FOX_HEADS * HEAD_DIM
    grid = (batch, npair, nq)
    fill_shape, fill_specs, fill_scratch = _zero_fill_specs(fill)
    outs = pl.pallas_call(
        functools.partial(_fox_attn_kernel, grid=grid),
        out_shape=tuple([jax.ShapeDtypeStruct((t, D_FOX), F32)] + fill_shape),
        grid=grid,
        in_specs=[
            pl.BlockSpec((tq, w), lambda b, h, i: (b * nq + i, h)),
            pl.BlockSpec((seq, w), lambda b, h, i: (b, npair + h)),
            pl.BlockSpec((seq, w), lambda b, h, i: (b, 2 * npair + h)),
            pl.BlockSpec((None, None, nq, FOX_HEADS, tq), lambda b, h, i: (b, h, 0, 0, 0)),
        ],
        out_specs=tuple([pl.BlockSpec((tq, w), lambda b, h, i: (b * nq + i, h))] + fill_specs),
        scratch_shapes=fill_scratch,
        compiler_params=_params(("parallel", "parallel", "arbitrary")),
        name="fox_attn",
    )(qkv, qkv, qkv, crow)
    return outs if fill is not None else outs[0]


def _mla_attn_kernel(q_ref, k_ref, v_ref, o_ref):
    tq = q_ref.shape[0]
    qi = pl.program_id(2)
    dk = 2 * LANES
    scale = (MLA_NOPE + MLA_ROPE) ** -0.5
    qs = [q_ref[:, h * dk:(h + 1) * dk] for h in range(MLA_HEADS)]
    outs = _softmax_sweep(qs,
                          lambda h, ks: k_ref[pl.ds(ks, tq), h * dk:(h + 1) * dk],
                          lambda h, ks: v_ref[pl.ds(ks, tq), h * MLA_V:(h + 1) * MLA_V],
                          lambda h, kb: None, scale, qi, tq)
    o_ref[...] = jnp.concatenate(outs, axis=-1)


def _mla_attention(q, k, v, batch, seq):
    t = q.shape[0]
    tq = ATT_TILE
    nq = seq // tq
    dk = MLA_HEADS * 2 * LANES
    dv = MLA_HEADS * MLA_V
    return pl.pallas_call(
        _mla_attn_kernel,
        out_shape=jax.ShapeDtypeStruct((t, D_MLA), F32),
        grid=(batch, N_MLA // MLA_HEADS, nq),
        in_specs=[
            pl.BlockSpec((tq, dk), lambda b, h, i: (b * nq + i, h)),
            pl.BlockSpec((seq, dk), lambda b, h, i: (b, h)),
            pl.BlockSpec((seq, dv), lambda b, h, i: (b, h)),
        ],
        out_specs=pl.BlockSpec((tq, dv), lambda b, h, i: (b * nq + i, h)),
        compiler_params=_params(("parallel", "parallel", "arbitrary")),
        name="mla_attn",
    )(q, k, v)


def _sb_attn_kernel(q_ref, k_ref, v_ref, o_ref, *fill, grid):
    copies = _zero_fill_start(fill, grid)
    tq = q_ref.shape[0]
    qi = pl.program_id(2)
    q = q_ref[...] * jnp.asarray(HEAD_DIM ** -0.5, BF16)
    nh = SB_HEADS
    qs = [q[:, j * HEAD_DIM:(j + 1) * HEAD_DIM] for j in range(nh)]
    _, strict = _causal_mask(tq)
    row = lax.broadcasted_iota(jnp.int32, (tq, tq), 0)
    col = lax.broadcasted_iota(jnp.int32, (tq, tq), 1)
    later = (row > col).astype(BF16)

    def step(kb, carry, diag):
        ks = pl.multiple_of(kb * tq, tq)
        zs = [lax.dot_general(qs[h], k_ref[pl.ds(ks, tq), h * HEAD_DIM:(h + 1) * HEAD_DIM],
                              (((1,), (1,)), ((), ())), preferred_element_type=F32) for h in range(nh)]
        lss, lrs = [], []
        for h in range(nh):
            z = zs[h]
            ls = jnp.minimum(z, 0.0) - jnp.log(1.0 + jnp.exp(-jnp.abs(z)))
            lr = ls - z
            if diag:
                lr = jnp.where(strict, lr, 0.0)
            lss.append(ls)
            lrs.append(lr)
        afters = []
        for h in range(nh):
            hi = lrs[h].astype(BF16)
            lo = (lrs[h] - hi.astype(F32)).astype(BF16)
            afters.append((jnp.dot(hi, later, preferred_element_type=F32)
                           + jnp.dot(lo, later, preferred_element_type=F32)) + carry[h][0])
        out = []
        for h in range(nh):
            a = jnp.exp(lss[h] + afters[h])
            if diag:
                a = jnp.where(strict, a, 0.0)
            v = v_ref[pl.ds(ks, tq), h * HEAD_DIM:(h + 1) * HEAD_DIM]
            acc = carry[h][1] + jnp.dot(a.astype(BF16), v, preferred_element_type=F32)
            out.append((afters[h][:, 0:1] + lrs[h][:, 0:1], acc))
        return tuple(out)

    init = tuple((jnp.zeros((tq, 1), F32), jnp.zeros((tq, HEAD_DIM), F32)) for _ in range(nh))
    carry = step(qi, init, True)
    carry = lax.fori_loop(0, qi, lambda i, c: step(qi - 1 - i, c, False), carry)
    o_ref[...] = jnp.concatenate([acc for (_, acc) in carry], axis=-1)
    _zero_fill_wait(copies)


def _sb_attention(qkv, batch, seq, fill=None):
    t = qkv.shape[0]
    tq = ATT_TILE
    nq = seq // tq
    npair = N_SB // SB_HEADS
    w = SB_HEADS * HEAD_DIM
    base = 3 * D_FOX // w
    grid = (batch, npair, nq)
    fill_shape, fill_specs, fill_scratch = _zero_fill_specs(fill)
    outs = pl.pallas_call(
        functools.partial(_sb_attn_kernel, grid=grid),
        out_shape=tuple([jax.ShapeDtypeStruct((t, D_SB), F32)] + fill_shape),
        grid=grid,
        in_specs=[
            pl.BlockSpec((tq, w), lambda b, h, i: (b * nq + i, base + h)),
            pl.BlockSpec((seq, w), lambda b, h, i: (b, base + npair + h)),
            pl.BlockSpec((seq, w), lambda b, h, i: (b, base + 2 * npair + h)),
        ],
        out_specs=tuple([pl.BlockSpec((tq, w), lambda b, h, i: (b * nq + i, h))] + fill_specs),
        scratch_shapes=fill_scratch,
        compiler_params=_params(("parallel", "parallel", "arbitrary")),
        name="sb_attn",
    )(qkv, qkv, qkv)
    return outs if fill is not None else outs[0]


def _mla_prep_kernel(cq_ref, ckv_ref, kr1_ref, kr2_ref, pos_ref, gq_ref, gkv_ref, wq_ref, wkv_ref,
                     invf_ref, sgn_ref, q_out, k_out, v_out):
    ang = pos_ref[...].astype(F32) * invf_ref[...]
    cosv = jnp.cos(ang)
    sinv = jnp.sin(ang) * sgn_ref[...]
    qn = (_rms(cq_ref[...]) * gq_ref[...]).astype(BF16)
    qa = jnp.dot(qn, wq_ref[...], preferred_element_type=F32)
    for h in range(N_MLA):
        o = h * 3 * LANES
        pe = qa[:, o + LANES:o + 2 * LANES] * cosv + qa[:, o + 2 * LANES:o + 3 * LANES] * sinv
        q_out[:, h * 2 * LANES:h * 2 * LANES + LANES] = qa[:, o:o + LANES].astype(BF16)
        q_out[:, h * 2 * LANES + LANES:(h + 1) * 2 * LANES] = pe.astype(BF16)
    kvn = (_rms(ckv_ref[...]) * gkv_ref[...]).astype(BF16)
    kva = jnp.dot(kvn, wkv_ref[...], preferred_element_type=F32)
    kpe = (kr1_ref[...] * cosv + kr2_ref[...] * sinv).astype(BF16)
    for h in range(N_MLA):
        k_out[:, h * 2 * LANES:h * 2 * LANES + LANES] = kva[:, h * LANES:(h + 1) * LANES].astype(BF16)
        k_out[:, h * 2 * LANES + LANES:(h + 1) * 2 * LANES] = kpe
    v_out[...] = kva[:, N_MLA * MLA_NOPE:].astype(BF16)


def _mla_prep(misc, pos, gq, gkv, wq, wkv, tm=512):
    t = misc.shape[0]
    half = MLA_ROPE // 2
    inv_freq = ROPE_THETA ** (-jnp.arange(half, dtype=F32) / half)
    invf = jnp.tile(inv_freq, LANES // half).reshape(1, LANES)
    sgn = jnp.tile(jnp.concatenate([-jnp.ones((half,), F32), jnp.ones((half,), F32)]),
                   LANES // MLA_ROPE).reshape(1, LANES)
    const = lambda i: (0, 0)
    return pl.pallas_call(
        _mla_prep_kernel,
        out_shape=(jax.ShapeDtypeStruct((t, N_MLA * 2 * LANES), BF16),
                   jax.ShapeDtypeStruct((t, N_MLA * 2 * LANES), BF16),
                   jax.ShapeDtypeStruct((t, D_MLA), BF16)),
        grid=(t // tm,),
        in_specs=[
            pl.BlockSpec((tm, Q_LORA), lambda i: (i, MISC_CQ // Q_LORA)),
            pl.BlockSpec((tm, KV_LORA), lambda i: (i, MISC_CKV // KV_LORA)),
            pl.BlockSpec((tm, LANES), lambda i: (i, MISC_KR1 // LANES)),
            pl.BlockSpec((tm, LANES), lambda i: (i, MISC_KR2 // LANES)),
            pl.BlockSpec((tm, 1), lambda i: (i, 0)),
            pl.BlockSpec((1, Q_LORA), const),
            pl.BlockSpec((1, KV_LORA), const),
            pl.BlockSpec(wq.shape, const),
            pl.BlockSpec(wkv.shape, const),
            pl.BlockSpec((1, LANES), const),
            pl.BlockSpec((1, LANES), const),
        ],
        out_specs=(pl.BlockSpec((tm, N_MLA * 2 * LANES), lambda i: (i, 0)),
                   pl.BlockSpec((tm, N_MLA * 2 * LANES), lambda i: (i, 0)),
                   pl.BlockSpec((tm, D_MLA), lambda i: (i, 0))),
        compiler_params=_params(("parallel",)),
        name="mla_prep",
    )(misc, misc, misc, misc, pos, gq.reshape(1, -1), gkv.reshape(1, -1), wq, wkv, invf, sgn)


def _gelu(x):
    return 0.5 * x * (1.0 + lax.erf(x * np.float32(np.sqrt(0.5))))


def _sgu_kernel(u_ref, v_ref, lng_ref, lnb_ref, w_ref, bias_ref, o_ref):
    tm = u_ref.shape[0]
    v = _gelu(v_ref[...])
    mu = jnp.mean(v, axis=-1, keepdims=True)
    xc = v - mu
    var = jnp.mean(xc * xc, axis=-1, keepdims=True)
    vb = (xc * lax.rsqrt(var + LN_EPS) * lng_ref[...] + lnb_ref[...]).astype(BF16)
    row = lax.broadcasted_iota(jnp.int32, (CHUNK, CHUNK), 0)
    col = lax.broadcasted_iota(jnp.int32, (CHUNK, LANES), 1)
    tril = lax.broadcasted_iota(jnp.int32, (CHUNK, CHUNK), 1) <= row
    first = col < SGU_CH
    ws = [jnp.where(tril, w_ref[g], 0.0).astype(BF16) for g in range(N_SGU)]
    zero = jnp.zeros((CHUNK, LANES), BF16)
    for c in range(tm // CHUNK):
        rows = slice(c * CHUNK, (c + 1) * CHUNK)
        for p in range(N_SGU // 2):
            cols = slice(p * LANES, (p + 1) * LANES)
            vp = vb[rows, cols]
            mixed = (jnp.dot(ws[2 * p], jnp.where(first, vp, zero), preferred_element_type=F32)
                     + jnp.dot(ws[2 * p + 1], jnp.where(first, zero, vp), preferred_element_type=F32))
            o_ref[rows, cols] = _gelu(u_ref[rows, cols]) * (mixed + bias_ref[:, cols])


def _sgu(misc, ln_g, ln_b, w_s, b_s, tm=512):
    t = misc.shape[0]
    bias = jnp.repeat(b_s.T, SGU_CH, axis=1)
    const2 = lambda i: (0, 0)
    return pl.pallas_call(
        _sgu_kernel,
        out_shape=jax.ShapeDtypeStruct((t, D_SGU), F32),
        grid=(t // tm,),
        in_specs=[
            pl.BlockSpec((tm, D_SGU), lambda i: (i, MISC_U // D_SGU)),
            pl.BlockSpec((tm, D_SGU), lambda i: (i, MISC_V // D_SGU)),
            pl.BlockSpec((1, D_SGU), const2),
            pl.BlockSpec((1, D_SGU), const2),
            pl.BlockSpec((N_SGU, CHUNK, CHUNK), lambda i: (0, 0, 0)),
            pl.BlockSpec((CHUNK, D_SGU), const2),
        ],
        out_specs=pl.BlockSpec((tm, D_SGU), lambda i: (i, 0)),
        compiler_params=_params(("parallel",)),
        name="sgu",
    )(misc, misc, ln_g.reshape(1, -1), ln_b.reshape(1, -1), w_s, bias)


def _pack_bf16_pairs(hb):
    n = hb.shape[1] // 2
    bits = pltpu.bitcast(hb.astype(F32), jnp.uint32)
    return (bits[:, n:] & jnp.uint32(0xFFFF0000)) | (bits[:, :n] >> 16)


def _unpack_bf16_pairs(xu):
    lo = pltpu.bitcast(xu << 16, F32).astype(BF16)
    hi = pltpu.bitcast(xu & jnp.uint32(0xFFFF0000), F32).astype(BF16)
    return jnp.concatenate([lo, hi], axis=1)


def _outproj_router_kernel(of_ref, os_ref, om_ref, og_ref, x_ref, ong_ref, wo_ref, nfg_ref, rw_ref,
                           rb_ref, xo_ref, h2_ref, idx_ref, gate_ref, rank_ref, cnt_ref, carry_ref):
    tm = x_ref.shape[0]
    th = tm // 2
    halves = [slice(h * th, (h + 1) * th) for h in range(2)]

    @pl.when(pl.program_id(0) == 0)
    def _():
        carry_ref[...] = jnp.zeros_like(carry_ref)

    os_ = []
    for rows in halves:
        o = jnp.concatenate([_rms(r[rows, :]) for r in (of_ref, os_ref, om_ref, og_ref)], axis=-1)
        os_.append((o * ong_ref[...]).astype(BF16))
    xns = [x_ref[rows, :] + jnp.dot(o, wo_ref[...], preferred_element_type=F32) for rows, o in zip(halves, os_)]
    hbs = []
    for rows, xn in zip(halves, xns):
        xo_ref[rows, :] = xn
        hb = (_rms(xn) * nfg_ref[...]).astype(BF16)
        h2_ref[rows, :] = _pack_bf16_pairs(hb)
        hbs.append(hb)

    lane = lax.broadcasted_iota(jnp.int32, (th, LANES), 1)
    all_logits = [jnp.dot(hb, rw_ref[...], preferred_element_type=F32) + rb_ref[...] for hb in hbs]
    picks = []
    for logits in all_logits:
        vals = jnp.where(lane < N_EXPERTS, logits, -jnp.inf)
        sels, tops = [], []
        for _ in range(TOP_K):
            m = jnp.max(vals, axis=-1, keepdims=True)
            idx = jnp.min(jnp.where(vals == m, lane, LANES), axis=-1, keepdims=True)
            sel = lane == idx
            vals = jnp.where(sel, -jnp.inf, vals)
            sels.append(sel)
            tops.append((m, idx))
        es = [jnp.exp(m - tops[0][0]) for (m, _) in tops]
        den = es[0] + es[1] + es[2] + es[3]
        mh = jnp.where(sels[0] | sels[1] | sels[2] | sels[3], 1.0, 0.0)
        picks.append((sels, tops, es, den, mh))

    r = lax.broadcasted_iota(jnp.int32, (th, th), 0)
    c = lax.broadcasted_iota(jnp.int32, (th, th), 1)
    before = (c < r).astype(BF16)
    parts = [jnp.dot(before, p[4].astype(BF16), preferred_element_type=F32) for p in picks]
    total = carry_ref[...]
    for rows, part, (sels, tops, es, den, mh) in zip(halves, parts, picks):
        cnt = part + total
        total = cnt[th - 1:th, :] + mh[th - 1:th, :]
        idx_o = jnp.zeros((th, LANES), jnp.int32)
        gate_o = jnp.zeros((th, LANES), F32)
        rank_o = jnp.zeros((th, LANES), jnp.int32)
        for k in range(TOP_K):
            rank_k = jnp.sum(jnp.where(sels[k], cnt, 0.0), axis=-1, keepdims=True).astype(jnp.int32)
            idx_o = jnp.where(lane == k, tops[k][1], idx_o)
            gate_o = jnp.where(lane == k, es[k] / den, gate_o)
            rank_o = jnp.where(lane == k, rank_k, rank_o)
        idx_ref[rows, :] = idx_o
        gate_ref[rows, :] = gate_o
        rank_ref[rows, :] = rank_o
    carry_ref[...] = total
    cnt_ref[...] = jnp.broadcast_to(total, cnt_ref.shape).astype(jnp.int32)


def _outproj_router(o_fox, o_sb, o_mla, o_sgu, x, out_norm_g, w_o, norm_ffn_g, router_w, router_b, tm=512):
    t, d = x.shape
    dg = o_fox.shape[1]
    rw = jnp.zeros((d, LANES), BF16).at[:, :N_EXPERTS].set(router_w.astype(BF16))
    rb = jnp.zeros((1, LANES), F32).at[0, :N_EXPERTS].set(router_b)
    const = lambda i: (0, 0)
    grp = pl.BlockSpec((tm, dg), lambda i: (i, 0))
    lanes_out = pl.BlockSpec((tm, LANES), lambda i: (i, 0))
    return pl.pallas_call(
        _outproj_router_kernel,
        out_shape=(jax.ShapeDtypeStruct((t, d), F32),
                   jax.ShapeDtypeStruct((t, d // 2), jnp.uint32),
                   jax.ShapeDtypeStruct((t, LANES), jnp.int32),
                   jax.ShapeDtypeStruct((t, LANES), F32),
                   jax.ShapeDtypeStruct((t, LANES), jnp.int32),
                   jax.ShapeDtypeStruct((8, LANES), jnp.int32)),
        grid=(t // tm,),
        in_specs=[grp, grp, grp, grp,
                  pl.BlockSpec((tm, d), lambda i: (i, 0)),
                  pl.BlockSpec((1, 4 * dg), const),
                  pl.BlockSpec((4 * dg, d), const),
                  pl.BlockSpec((1, d), const),
                  pl.BlockSpec((d, LANES), const),
                  pl.BlockSpec((1, LANES), const)],
        out_specs=(pl.BlockSpec((tm, d), lambda i: (i, 0)),
                   pl.BlockSpec((tm, d // 2), lambda i: (i, 0)),
                   lanes_out, lanes_out, lanes_out,
                   pl.BlockSpec((8, LANES), const)),
        scratch_shapes=[pltpu.VMEM((1, LANES), F32)],
        compiler_params=_params(("arbitrary",)),
        name="outproj_router",
    )(o_fox, o_sb, o_mla, o_sgu, x, out_norm_g.reshape(1, -1), w_o, norm_ffn_g.reshape(1, -1), rw, rb)


def _scatter_rows_kernel(dest_hbm, x_ref, xs_init_hbm, xs_hbm, dest0, dest1, sems):
    del xs_init_hbm
    dest_smem = (dest0, dest1)
    i = pl.program_id(0)
    nsteps = pl.num_programs(0)
    tm = x_ref.shape[0]
    n = tm * TOP_K

    def idx_copy(step, sl):
        return pltpu.make_async_copy(dest_hbm.at[pl.ds(step * n, n)], dest_smem[sl], sems.at[sl])

    @pl.when(i == 0)
    def _():
        idx_copy(0, 0).start()

    for sl in range(2):
        @pl.when(i % 2 == sl)
        def _(sl=sl):
            @pl.when(i + 1 < nsteps)
            def _():
                idx_copy(i + 1, 1 - sl).start()

            idx_copy(i, sl).wait()

            def issue(t, _):
                for k in range(TOP_K):
                    pltpu.make_async_copy(x_ref.at[pl.ds(t, 1)],
                                          xs_hbm.at[pl.ds(dest_smem[sl][t * TOP_K + k], 1)], sems.at[2]).start()
                return 0

            lax.fori_loop(0, tm, issue, 0, unroll=8)

    for _ in range(TOP_K):
        pltpu.make_async_copy(x_ref, xs_hbm.at[pl.ds(0, tm)], sems.at[2]).wait()


def _scatter_rows(dest_flat, src, nrows, xs_init):
    t, w = src.shape
    tm = SCATTER_TM
    assert xs_init.shape == (nrows, w) and xs_init.dtype == src.dtype
    return pl.pallas_call(
        _scatter_rows_kernel,
        out_shape=jax.ShapeDtypeStruct((nrows, w), src.dtype),
        grid=(t // tm,),
        in_specs=[pl.BlockSpec(memory_space=pl.ANY),
                  pl.BlockSpec((tm, w), lambda i: (i, 0)),
                  pl.BlockSpec(memory_space=pl.ANY)],
        out_specs=pl.BlockSpec(memory_space=pl.ANY),
        scratch_shapes=[pltpu.SMEM((tm * TOP_K,), jnp.int32), pltpu.SMEM((tm * TOP_K,), jnp.int32),
                        pltpu.SemaphoreType.DMA((3,))],
        input_output_aliases={2: 0},
        compiler_params=_params(("arbitrary",)),
        name="scatter_rows",
    )(dest_flat, src, xs_init)


def _expert_kernel(sbe_ref, nval_ref, nused_ref, x_ref, wg_ref, bg_ref, wu_ref, bu_ref, wd_ref, bd_ref,
                   y_init_hbm, y_ref):
    del sbe_ref, nused_ref, y_init_hbm
    s = pl.program_id(0)
    f = pl.program_id(1)
    d = y_ref.shape[1]

    @pl.when(f == 0)
    def _():
        y_ref[...] = jnp.broadcast_to(bd_ref[...], y_ref.shape)

    nsub = (nval_ref[s] + EXP_SUB - 1) // EXP_SUB

    for n in range(1, EXP_ROWS // EXP_SUB + 1):
        @pl.when(nsub == n)
        def _(m=n * EXP_SUB):
            x = _unpack_bf16_pairs(x_ref[0:m, :])
            g = jnp.dot(x, wg_ref[...].astype(BF16), preferred_element_type=F32) + bg_ref[...]
            g = jnp.minimum(g, SWIGLU_LIMIT)
            u = jnp.dot(x, wu_ref[...].astype(BF16), preferred_element_type=F32) + bu_ref[...]
            u = jnp.clip(u, -SWIGLU_LIMIT, SWIGLU_LIMIT)
            a = (g * jax.nn.sigmoid(SWIGLU_ALPHA * g) * (u + 1.0)).astype(BF16)
            wd = wd_ref[...].astype(BF16)
            for c in range(d // EXP_DN):
                cols = slice(c * EXP_DN, (c + 1) * EXP_DN)
                y_ref[0:m, cols] += jnp.dot(a, wd[:, cols], preferred_element_type=F32)


def _experts(sb_expert, sb_nvalid, n_used, xs, layer, w_gate, b_gate, w_up, b_up, w_down, b_down, nsb_max, y_init):
    d = w_gate.shape[2]
    dff = w_gate.shape[3]
    assert d % EXP_DN == 0 and dff % EXP_FF == 0 and EXP_ROWS % EXP_SUB == 0
    assert y_init.shape == (nsb_max * EXP_ROWS, d) and y_init.dtype == F32
    nf = dff // EXP_FF

    def xmap(s, f, sbe, nval, nused):
        return (s, 0)

    def ff(s, f, nused):
        return f

    b_gate4 = b_gate.reshape(b_gate.shape[0], N_EXPERTS, 1, dff)
    b_up4 = b_up.reshape(b_up.shape[0], N_EXPERTS, 1, dff)
    b_down4 = b_down.reshape(b_down.shape[0], N_EXPERTS, 1, d)
    y_init_operand = 3 + 7
    return pl.pallas_call(
        _expert_kernel,
        out_shape=jax.ShapeDtypeStruct((nsb_max * EXP_ROWS, d), F32),
        grid_spec=pltpu.PrefetchScalarGridSpec(
            num_scalar_prefetch=3,
            grid=(n_used[0], nf),
            in_specs=[
                pl.BlockSpec((EXP_ROWS, d // 2), xmap),
                pl.BlockSpec((None, None, d, EXP_FF), lambda s, f, sbe, nval, nused: (layer, sbe[s], 0, ff(s, f, nused))),
                pl.BlockSpec((None, None, 1, EXP_FF), lambda s, f, sbe, nval, nused: (layer, sbe[s], 0, ff(s, f, nused))),
                pl.BlockSpec((None, None, d, EXP_FF), lambda s, f, sbe, nval, nused: (layer, sbe[s], 0, ff(s, f, nused))),
                pl.BlockSpec((None, None, 1, EXP_FF), lambda s, f, sbe, nval, nused: (layer, sbe[s], 0, ff(s, f, nused))),
                pl.BlockSpec((None, None, EXP_FF, d), lambda s, f, sbe, nval, nused: (layer, sbe[s], ff(s, f, nused), 0)),
                pl.BlockSpec((None, None, 1, d), lambda s, f, sbe, nval, nused: (layer, sbe[s], 0, 0)),
                pl.BlockSpec(memory_space=pl.ANY),
            ],
            out_specs=pl.BlockSpec((EXP_ROWS, d), xmap),
        ),
        input_output_aliases={y_init_operand: 0},
        compiler_params=_params(("arbitrary", "arbitrary")),
        name="experts",
    )(sb_expert, sb_nvalid, n_used, xs, w_gate, b_gate4, w_up, b_up4, w_down, b_down4, y_init)


def _combine_kernel(dest_hbm, y_hbm, x_ref, gate_ref, fg_ref, o_ref, dest0, dest1, ybuf0, ybuf1, sems, *,
                    final_norm):
    dest_smem = (dest0, dest1)
    ybuf = (ybuf0, ybuf1)
    i = pl.program_id(0)
    nsteps = pl.num_programs(0)
    tm = x_ref.shape[0]
    n = tm * TOP_K
    slot = i % 2
    nslot = 1 - slot

    def idx_copy(step, sl):
        return pltpu.make_async_copy(dest_hbm.at[pl.ds(step * n, n)], dest_smem[sl], sems.at[sl])

    def issue_rows(sl):
        def issue(t, _):
            for k in range(TOP_K):
                pltpu.make_async_copy(y_hbm.at[pl.ds(dest_smem[sl][t * TOP_K + k], 1)],
                                      ybuf[sl].at[pl.ds(k * tm + t, 1)], sems.at[2 + sl]).start()
            return 0

        lax.fori_loop(0, tm, issue, 0, unroll=4)

    @pl.when(i == 0)
    def _():
        first = idx_copy(0, 0)
        first.start()
        first.wait()
        issue_rows(0)

        @pl.when(nsteps > 1)
        def _():
            idx_copy(1, 1).start()

    for sl in range(2):
        @pl.when(jnp.logical_and(i + 1 < nsteps, nslot == sl))
        def _(sl=sl):
            idx_copy(i + 1, sl).wait()
            issue_rows(sl)

    for sl in range(2):
        @pl.when(slot == sl)
        def _(sl=sl):
            @pl.when(i + 2 < nsteps)
            def _():
                idx_copy(i + 2, sl).start()

            pltpu.make_async_copy(y_hbm.at[pl.ds(0, n)], ybuf[sl], sems.at[2 + sl]).wait()
            gate = gate_ref[...]
            acc = gate[:, 0:1] * ybuf[sl][0:tm]
            for k in range(1, TOP_K):
                acc = acc + gate[:, k:k + 1] * ybuf[sl][k * tm:(k + 1) * tm]
            out = x_ref[...] + acc
            if final_norm:
                out = _rms(out) * fg_ref[...]
            o_ref[...] = out


def _combine(dest_flat, y, x, gates, final_g, final_norm):
    t, d = x.shape
    tm = COMBINE_TM
    return pl.pallas_call(
        functools.partial(_combine_kernel, final_norm=final_norm),
        out_shape=jax.ShapeDtypeStruct((t, d), F32),
        grid=(t // tm,),
        in_specs=[
            pl.BlockSpec(memory_space=pl.ANY),
            pl.BlockSpec(memory_space=pl.ANY),
            pl.BlockSpec((tm, d), lambda i: (i, 0)),
            pl.BlockSpec((tm, LANES), lambda i: (i, 0)),
            pl.BlockSpec((1, d), lambda i: (0, 0)),
        ],
        out_specs=pl.BlockSpec((tm, d), lambda i: (i, 0)),
        scratch_shapes=[pltpu.SMEM((tm * TOP_K,), jnp.int32), pltpu.SMEM((tm * TOP_K,), jnp.int32),
                        pltpu.VMEM((TOP_K * tm, d), F32), pltpu.VMEM((TOP_K * tm, d), F32),
                        pltpu.SemaphoreType.DMA((4,))],
        compiler_params=_params(("arbitrary",)),
        name="combine",
    )(dest_flat, y, x, gates, final_g.reshape(1, d))


def _in_proj_weights(w_in):
    d = w_in.shape[0]
    o_gate = 3 * D_FOX
    o_sb = o_gate + N_FOX
    o_cq = o_sb + 3 * D_SB
    o_ckv = o_cq + Q_LORA
    o_kr = o_ckv + KV_LORA
    o_sgu = o_kr + MLA_ROPE
    half = MLA_ROPE // 2
    w_qkv = jnp.concatenate([w_in[:, :o_gate], w_in[:, o_sb:o_cq]], axis=1).astype(BF16)
    kr = w_in[:, o_kr:o_sgu]
    zpad = jnp.zeros((d, LANES - MLA_ROPE), w_in.dtype)
    w_misc = jnp.concatenate([
        w_in[:, o_sgu:o_sgu + 2 * D_SGU],
        w_in[:, o_cq:o_ckv],
        w_in[:, o_ckv:o_kr],
        kr, zpad,
        kr[:, half:], kr[:, :half], zpad,
        w_in[:, o_gate:o_sb], jnp.zeros((d, LANES - N_FOX), w_in.dtype),
    ], axis=1).astype(BF16)
    assert w_misc.shape[1] == MISC_W
    return w_qkv, w_misc


def _mla_weights(w_q_b, w_kv_b):
    half = MLA_ROPE // 2
    wq = w_q_b.reshape(Q_LORA, N_MLA, MLA_NOPE + MLA_ROPE)
    x1 = wq[:, :, MLA_NOPE:MLA_NOPE + half]
    x2 = wq[:, :, MLA_NOPE + half:]
    z = jnp.zeros((Q_LORA, N_MLA, LANES - MLA_ROPE), w_q_b.dtype)
    wq_r = jnp.concatenate([wq[:, :, :MLA_NOPE], x1, x2, z, x2, x1, z], axis=2)
    wq_r = wq_r.reshape(Q_LORA, N_MLA * 3 * LANES).astype(BF16)
    wkv = w_kv_b.reshape(KV_LORA, N_MLA, MLA_NOPE + MLA_V)
    wkv_r = jnp.concatenate([wkv[:, :, :MLA_NOPE].reshape(KV_LORA, -1),
                             wkv[:, :, MLA_NOPE:].reshape(KV_LORA, -1)], axis=1).astype(BF16)
    return wq_r, wkv_r


def _routing_tables(idx, rank, cnt, n_tokens):
    nsb_max = N_EXPERTS + (n_tokens * TOP_K) // EXP_ROWS
    counts = cnt[0, :N_EXPERTS]
    nsb_e = (counts + EXP_ROWS - 1) // EXP_ROWS
    sb_end = jnp.cumsum(nsb_e)
    sb_start = sb_end - nsb_e
    n_used = sb_end[-1]
    top_idx = idx[:, :TOP_K].reshape(-1)
    dest = (sb_start * EXP_ROWS)[top_idx] + rank[:, :TOP_K].reshape(-1)
    s_ids = jnp.arange(nsb_max, dtype=jnp.int32)
    sb_e = jnp.minimum(jnp.sum(s_ids[:, None] >= sb_end[None, :], axis=1), N_EXPERTS - 1).astype(jnp.int32)
    last_e = sb_e[jnp.maximum(n_used - 1, 0)]
    sb_e = jnp.where(s_ids < n_used, sb_e, last_e)
    nval = jnp.clip(counts[sb_e] - (s_ids - sb_start[sb_e]) * EXP_ROWS, 0, EXP_ROWS)
    nval = jnp.where(s_ids < n_used, nval, 0).astype(jnp.int32)
    return dest.astype(jnp.int32), sb_e, nval, n_used.reshape(1).astype(jnp.int32), nsb_max


def _layer(x, pos, l, p, final_g, final_norm, batch, seq, moe_bufs):
    t = x.shape[0]
    w_qkv, w_misc = _in_proj_weights(p["w_in"][l])
    qkv = _norm_matmul(x, p["norm_mix_g"][l], w_qkv, BF16, 512, 1536, "in_proj_qkv")
    misc = _norm_matmul(x, p["norm_mix_g"][l], w_misc, F32, 512, MISC_W, "in_proj_misc")

    crow = _fox_prep(misc, p["b_forget"][l], batch, seq)
    nkb = seq // ATT_TILE
    crow = crow.reshape(batch, nkb, N_FOX // FOX_HEADS, FOX_HEADS, ATT_TILE).transpose(0, 2, 1, 3, 4)
    if moe_bufs is None:
        nrows = (N_EXPERTS + (t * TOP_K) // EXP_ROWS) * EXP_ROWS
        o_fox, xs_prev = _fox_attention(qkv, crow, batch, seq,
                                        jax.ShapeDtypeStruct((nrows, x.shape[1] // 2), jnp.uint32))
        o_sb, y_prev = _sb_attention(qkv, batch, seq, jax.ShapeDtypeStruct((nrows, x.shape[1]), F32))
    else:
        xs_prev, y_prev = moe_bufs
        o_fox = _fox_attention(qkv, crow, batch, seq)
        o_sb = _sb_attention(qkv, batch, seq)
    wq_r, wkv_r = _mla_weights(p["mla_w_q_b"][l], p["mla_w_kv_b"][l])
    q_m, k_m, v_m = _mla_prep(misc, pos, p["mla_q_norm_g"][l], p["mla_kv_norm_g"][l], wq_r, wkv_r)
    o_mla = _mla_attention(q_m, k_m, v_m, batch, seq)
    o_sgu = _sgu(misc, p["sgu_ln_g"][l], p["sgu_ln_b"][l], p["sgu_w_s"][l], p["sgu_b_s"][l])

    x_new, h2p, idx, gates, rank, cnt = _outproj_router(
        o_fox, o_sb, o_mla, o_sgu, x, p["out_norm_g"][l], p["w_o"][l].astype(BF16),
        p["norm_ffn_g"][l], p["router_w"][l], p["router_b"][l])

    dest, sb_e, nval, n_used, nsb_max = _routing_tables(idx, rank, cnt, t)
    xs = _scatter_rows(dest, h2p, nsb_max * EXP_ROWS, xs_prev)
    y = _experts(sb_e, nval, n_used, xs, l, p["w_gate"], p["b_gate"], p["w_up"], p["b_up"],
                 p["w_down"], p["b_down"], nsb_max, y_prev)
    return _combine(dest, y, x_new, gates, final_g, final_norm), (xs, y)


def kernel(x, positions, norm_mix_g, w_in, b_forget, mla_q_norm_g, mla_kv_norm_g, mla_w_q_b, mla_w_kv_b,
           sgu_ln_g, sgu_ln_b, sgu_w_s, sgu_b_s, out_norm_g, w_o, norm_ffn_g, router_w, router_b, w_gate,
           b_gate, w_up, b_up, w_down, b_down, final_norm_g):
    batch, seq, d = x.shape
    depth = w_in.shape[0]
    p = dict(norm_mix_g=norm_mix_g, w_in=w_in, b_forget=b_forget, mla_q_norm_g=mla_q_norm_g,
             mla_kv_norm_g=mla_kv_norm_g, mla_w_q_b=mla_w_q_b, mla_w_kv_b=mla_w_kv_b, sgu_ln_g=sgu_ln_g,
             sgu_ln_b=sgu_ln_b, sgu_w_s=sgu_w_s, sgu_b_s=sgu_b_s, out_norm_g=out_norm_g, w_o=w_o,
             norm_ffn_g=norm_ffn_g, router_w=router_w, router_b=router_b, w_gate=w_gate, b_gate=b_gate,
             w_up=w_up, b_up=b_up, w_down=w_down, b_down=b_down)
    h = x.reshape(batch * seq, d)
    pos = positions.reshape(batch * seq, 1)
    moe_bufs = None
    for l in range(depth):
        h, moe_bufs = _layer(h, pos, l, p, final_norm_g, l == depth - 1, batch, seq, moe_bufs)
    return h.reshape(batch, seq, d)
```

```python
import functools

import numpy as np
import jax
import jax.numpy as jnp
from jax import lax
from jax.experimental import pallas as pl
from jax.experimental.pallas import tpu as pltpu

HEAD_DIM = 64
N_FOX = 8
N_SB = 8
N_MLA = 4
MLA_NOPE = 128
MLA_ROPE = 64
MLA_V = 128
Q_LORA = 512
KV_LORA = 256
N_SGU = 8
SGU_CH = 64
CHUNK = 128
ROPE_THETA = 10000.0
D_FOX = N_FOX * HEAD_DIM
D_SB = N_SB * HEAD_DIM
D_MLA = N_MLA * MLA_V
D_SGU = N_SGU * SGU_CH
N_EXPERTS = 32
TOP_K = 4
SWIGLU_LIMIT = 7.0
SWIGLU_ALPHA = 1.702
RMS_EPS = 1e-6
LN_EPS = 1e-5

LANES = 128
VMEM_LIMIT = 56 * 1024 * 1024

ATT_TILE = 256
FOX_HEADS = 8
MLA_HEADS = 4
SB_HEADS = 8
EXP_ROWS = 1536
EXP_SUB = 256
EXP_FF = 256
EXP_DN = 512
SCATTER_TM = 512
ZERO_ROWS = 512
COMBINE_TM = 256

F32 = jnp.float32
BF16 = jnp.bfloat16

MISC_U = 0
MISC_V = 512
MISC_CQ = 1024
MISC_CKV = 1536
MISC_KR1 = 1792
MISC_KR2 = 1920
MISC_GATE = 2048
MISC_W = 2176


def _params(sem, vmem=VMEM_LIMIT):
    return pltpu.CompilerParams(dimension_semantics=sem, vmem_limit_bytes=vmem)


def _split3(x):
    hi = x.astype(BF16)
    r1 = x - hi.astype(F32)
    mid = r1.astype(BF16)
    lo = (r1 - mid.astype(F32)).astype(BF16)
    return hi, mid, lo


def _log_sigmoid_pair(z):
    t = jnp.log1p(jnp.exp(-jnp.abs(z)))
    return jnp.minimum(z, 0.0) - t, jnp.minimum(-z, 0.0) - t


def _rms(x):
    return x * lax.rsqrt(jnp.mean(x * x, axis=-1, keepdims=True) + RMS_EPS)


def _norm_matmul_kernel(x_ref, g_ref, w_ref, o_ref, xn_ref):
    tm = x_ref.shape[0]

    @pl.when(pl.program_id(1) == 0)
    def _():
        for rows in (slice(0, tm // 2), slice(tm // 2, tm)):
            xn = (_rms(x_ref[rows, :]) * g_ref[...]).astype(BF16)
            xn_ref[rows, :] = xn
            o_ref[rows, :] = jnp.dot(xn, w_ref[...], preferred_element_type=F32).astype(o_ref.dtype)

    @pl.when(pl.program_id(1) != 0)
    def _():
        o_ref[...] = jnp.dot(xn_ref[...], w_ref[...], preferred_element_type=F32).astype(o_ref.dtype)


def _norm_matmul(x, g, w, out_dtype, tm, tn, name):
    t, d = x.shape
    n = w.shape[1]
    return pl.pallas_call(
        _norm_matmul_kernel,
        out_shape=jax.ShapeDtypeStruct((t, n), out_dtype),
        grid=(t // tm, n // tn),
        in_specs=[
            pl.BlockSpec((tm, d), lambda i, j: (i, 0)),
            pl.BlockSpec((1, d), lambda i, j: (0, 0)),
            pl.BlockSpec((d, tn), lambda i, j: (0, j)),
        ],
        out_specs=pl.BlockSpec((tm, tn), lambda i, j: (i, j)),
        scratch_shapes=[pltpu.VMEM((tm, d), BF16)],
        compiler_params=_params(("parallel", "arbitrary")),
        name=name,
    )(x, g.reshape(1, d), w)


def _fox_prep_kernel(gate_ref, bf_ref, crow_ref):
    s_len = gate_ref.shape[0]
    bl = ATT_TILE
    r = lax.broadcasted_iota(jnp.int32, (bl, bl), 0)
    c = lax.broadcasted_iota(jnp.int32, (bl, bl), 1)
    tri = (c <= r).astype(BF16)
    carry = jnp.zeros((1, LANES), F32)
    for i in range(s_len // bl):
        z = gate_ref[i * bl:(i + 1) * bl, :] + bf_ref[...]
        lf, _ = _log_sigmoid_pair(z)
        hi, mid, lo = _split3(lf)
        cs = (jnp.dot(tri, hi, preferred_element_type=F32)
              + jnp.dot(tri, mid, preferred_element_type=F32)
              + jnp.dot(tri, lo, preferred_element_type=F32)) + carry
        crow_ref[i] = cs.T[0:N_FOX, :]
        carry = cs[bl - 1:bl, :]


def _fox_prep(misc, b_forget, batch, seq):
    nkb = seq // ATT_TILE
    bf = jnp.zeros((1, LANES), F32).at[0, :N_FOX].set(b_forget)
    return pl.pallas_call(
        _fox_prep_kernel,
        out_shape=jax.ShapeDtypeStruct((batch, nkb, N_FOX, ATT_TILE), F32),
        grid=(batch,),
        in_specs=[
            pl.BlockSpec((seq, LANES), lambda b: (b, MISC_GATE // LANES)),
            pl.BlockSpec((1, LANES), lambda b: (0, 0)),
        ],
        out_specs=pl.BlockSpec((None, nkb, N_FOX, ATT_TILE), lambda b: (b, 0, 0, 0)),
        compiler_params=_params(("parallel",)),
        name="fox_prep",
    )(misc, bf)


def _causal_mask(tq):
    row = lax.broadcasted_iota(jnp.int32, (tq, tq), 0)
    col = lax.broadcasted_iota(jnp.int32, (tq, tq), 1)
    return col <= row, col < row


def _softmax_sweep(qs, k_of, v_of, bias_of, scale, qi, tq, ones_lane=None):
    nh = len(qs)
    incl, _ = _causal_mask(tq)

    def step(kb, carry, diag):
        ks = pl.multiple_of(kb * tq, tq)
        ss = [lax.dot_general(qs[h], k_of(h, ks), (((1,), (1,)), ((), ())), preferred_element_type=F32)
              for h in range(nh)]
        ps, stats = [], []
        for h in range(nh):
            m, l, _ = carry[h]
            s = ss[h]
            if scale is not None:
                s = s * scale
            b = bias_of(h, kb)
            if b is not None:
                s = s + b
            if diag:
                s = jnp.where(incl, s, -jnp.inf)
            m_new = jnp.maximum(m, jnp.max(s, axis=-1, keepdims=True))
            alpha = jnp.exp(m - m_new)
            p = jnp.exp(s - m_new)
            if ones_lane is None:
                l = alpha * l + jnp.sum(p, axis=-1, keepdims=True)
            stats.append((m_new, l, alpha))
            ps.append(p.astype(BF16))
        out = []
        for h in range(nh):
            m_new, l, alpha = stats[h]
            acc = alpha * carry[h][2] + jnp.dot(ps[h], v_of(h, ks), preferred_element_type=F32)
            out.append((m_new, l, acc))
        return tuple(out)

    dv = v_of(0, 0).shape[-1]
    l0 = jnp.zeros((tq, 1), F32) if ones_lane is None else None
    init = tuple((jnp.full((tq, 1), -jnp.inf, F32), l0, jnp.zeros((tq, dv), F32)) for _ in range(nh))
    carry = lax.fori_loop(0, qi, lambda kb, c: step(kb, c, False), init)
    carry = step(qi, carry, True)
    if ones_lane is None:
        return [acc / l for (_, l, acc) in carry]
    return [acc / acc[:, ones_lane(h):ones_lane(h) + 1] for h, (_, _, acc) in enumerate(carry)]


def _zero_fill_copies(fill_hbm, zeros, sem, grid):
    nq = grid[2]
    ngroups = grid[0] * grid[1]
    rows = zeros.shape[0]
    nblocks = fill_hbm.shape[0] // rows
    per_group = (nblocks + ngroups - 1) // ngroups
    work = nq * (nq + 1) // 2
    max_per_step = (per_group * nq + work - 1) // work + 1
    group = pl.program_id(0) * grid[1] + pl.program_id(1)
    qi = pl.program_id(2)
    first = per_group * (qi * (qi + 1) // 2) // work
    count = per_group * ((qi + 1) * (qi + 2) // 2) // work - first
    copies = []
    for j in range(max_per_step):
        block = group * per_group + first + j
        valid = jnp.logical_and(j < count, block < nblocks)
        at = jnp.minimum(block, nblocks - 1) * rows
        copies.append((valid, pltpu.make_async_copy(zeros, fill_hbm.at[pl.ds(at, rows)], sem)))
    return copies


def _zero_fill_start(fill, grid):
    if not fill:
        return []
    fill_hbm, zeros, sem = fill
    zeros[...] = jnp.zeros_like(zeros)
    copies = _zero_fill_copies(fill_hbm, zeros, sem, grid)
    for valid, cp in copies:
        @pl.when(valid)
        def _(cp=cp):
            cp.start()
    return copies


def _zero_fill_wait(copies):
    for valid, cp in copies:
        @pl.when(valid)
        def _(cp=cp):
            cp.wait()


def _zero_fill_specs(fill):
    if fill is None:
        return [], [], []
    assert fill.shape[0] % ZERO_ROWS == 0
    return ([fill], [pl.BlockSpec(memory_space=pl.ANY)],
            [pltpu.VMEM((ZERO_ROWS, fill.shape[1]), fill.dtype), pltpu.SemaphoreType.DMA(())])


def _fox_attn_kernel(q_ref, k_ref, v_ref, crow_ref, o_ref, *fill, grid):
    copies = _zero_fill_start(fill, grid)
    tq = q_ref.shape[0]
    qi = pl.program_id(2)
    q = q_ref[...] * jnp.asarray(HEAD_DIM ** -0.5, BF16)
    qs = [q[:, j * HEAD_DIM:(j + 1) * HEAD_DIM] for j in range(FOX_HEADS)]

    def k_of(h, ks):
        return k_ref[pl.ds(ks, tq), h * HEAD_DIM:(h + 1) * HEAD_DIM]

    first = lax.broadcasted_iota(jnp.int32, (tq, LANES), 1) < HEAD_DIM
    one = jnp.ones((tq, LANES), BF16)

    def v_of(h, ks):
        pair = v_ref[pl.ds(ks, tq), (h // 2) * LANES:(h // 2 + 1) * LANES]
        return jnp.where(first, pair, one) if h % 2 == 0 else jnp.where(first, one, pair)

    def bias_of(h, kb):
        return -crow_ref[kb][h:h + 1, :]

    outs = _softmax_sweep(qs, k_of, v_of, bias_of, None, qi, tq,
                          ones_lane=lambda h: HEAD_DIM if h % 2 == 0 else 0)
    o_ref[...] = jnp.concatenate([jnp.where(first, outs[h], outs[h + 1]) for h in range(0, FOX_HEADS, 2)],
                                 axis=-1)
    _zero_fill_wait(copies)


def _fox_attention(qkv, crow, batch, seq, fill=None):
    t = qkv.shape[0]
    tq = ATT_TILE
    nq = seq // tq
    npair = N_FOX // FOX_HEADS
    w = FOX_HEADS * HEAD_DIM
    grid = (batch, npair, nq)
    fill_shape, fill_specs, fill_scratch = _zero_fill_specs(fill)
    outs = pl.pallas_call(
        functools.partial(_fox_attn_kernel, grid=grid),
        out_shape=tuple([jax.ShapeDtypeStruct((t, D_FOX), F32)] + fill_shape),
        grid=grid,
        in_specs=[
            pl.BlockSpec((tq, w), lambda b, h, i: (b * nq + i, h)),
            pl.BlockSpec((seq, w), lambda b, h, i: (b, npair + h)),
            pl.BlockSpec((seq, w), lambda b, h, i: (b, 2 * npair + h)),
            pl.BlockSpec((None, None, nq, FOX_HEADS, tq), lambda b, h, i: (b, h, 0, 0, 0)),
        ],
        out_specs=tuple([pl.BlockSpec((tq, w), lambda b, h, i: (b * nq + i, h))] + fill_specs),
        scratch_shapes=fill_scratch,
        compiler_params=_params(("parallel", "parallel", "arbitrary")),
        name="fox_attn",
    )(qkv, qkv, qkv, crow)
    return outs if fill is not None else outs[0]


def _mla_attn_kernel(q_ref, k_ref, v_ref, o_ref):
    tq = q_ref.shape[0]
    qi = pl.program_id(2)
    dk = 2 * LANES
    scale = (MLA_NOPE + MLA_ROPE) ** -0.5
    qs = [q_ref[:, h * dk:(h + 1) * dk] for h in range(MLA_HEADS)]
    outs = _softmax_sweep(qs,
                          lambda h, ks: k_ref[pl.ds(ks, tq), h * dk:(h + 1) * dk],
                          lambda h, ks: v_ref[pl.ds(ks, tq), h * MLA_V:(h + 1) * MLA_V],
                          lambda h, kb: None, scale, qi, tq)
    o_ref[...] = jnp.concatenate(outs, axis=-1)


def _mla_attention(q, k, v, batch, seq):
    t = q.shape[0]
    tq = ATT_TILE
    nq = seq // tq
    dk = MLA_HEADS * 2 * LANES
    dv = MLA_HEADS * MLA_V
    return pl.pallas_call(
        _mla_attn_kernel,
        out_shape=jax.ShapeDtypeStruct((t, D_MLA), F32),
        grid=(batch, N_MLA // MLA_HEADS, nq),
        in_specs=[
            pl.BlockSpec((tq, dk), lambda b, h, i: (b * nq + i, h)),
            pl.BlockSpec((seq, dk), lambda b, h, i: (b, h)),
            pl.BlockSpec((seq, dv), lambda b, h, i: (b, h)),
        ],
        out_specs=pl.BlockSpec((tq, dv), lambda b, h, i: (b * nq + i, h)),
        compiler_params=_params(("parallel", "parallel", "arbitrary")),
        name="mla_attn",
    )(q, k, v)


def _sb_attn_kernel(q_ref, k_ref, v_ref, o_ref, *fill, grid):
    copies = _zero_fill_start(fill, grid)
    tq = q_ref.shape[0]
    qi = pl.program_id(2)
    q = q_ref[...] * jnp.asarray(HEAD_DIM ** -0.5, BF16)
    nh = SB_HEADS
    qs = [q[:, j * HEAD_DIM:(j + 1) * HEAD_DIM] for j in range(nh)]
    _, strict = _causal_mask(tq)
    row = lax.broadcasted_iota(jnp.int32, (tq, tq), 0)
    col = lax.broadcasted_iota(jnp.int32, (tq, tq), 1)
    later = (row > col).astype(BF16)

    def step(kb, carry, diag):
        ks = pl.multiple_of(kb * tq, tq)
        zs = [lax.dot_general(qs[h], k_ref[pl.ds(ks, tq), h * HEAD_DIM:(h + 1) * HEAD_DIM],
                              (((1,), (1,)), ((), ())), preferred_element_type=F32) for h in range(nh)]
        lss, lrs = [], []
        for h in range(nh):
            z = zs[h]
            ls = jnp.minimum(z, 0.0) - jnp.log(1.0 + jnp.exp(-jnp.abs(z)))
            lr = ls - z
            if diag:
                lr = jnp.where(strict, lr, 0.0)
            lss.append(ls)
            lrs.append(lr)
        afters = []
        for h in range(nh):
            hi = lrs[h].astype(BF16)
            lo = (lrs[h] - hi.astype(F32)).astype(BF16)
            afters.append((jnp.dot(hi, later, preferred_element_type=F32)
                           + jnp.dot(lo, later, preferred_element_type=F32)) + carry[h][0])
        out = []
        for h in range(nh):
            a = jnp.exp(lss[h] + afters[h])
            if diag:
                a = jnp.where(strict, a, 0.0)
            v = v_ref[pl.ds(ks, tq), h * HEAD_DIM:(h + 1) * HEAD_DIM]
            acc = carry[h][1] + jnp.dot(a.astype(BF16), v, preferred_element_type=F32)
            out.append((afters[h][:, 0:1] + lrs[h][:, 0:1], acc))
        return tuple(out)

    init = tuple((jnp.zeros((tq, 1), F32), jnp.zeros((tq, HEAD_DIM), F32)) for _ in range(nh))
    carry = step(qi, init, True)
    carry = lax.fori_loop(0, qi, lambda i, c: step(qi - 1 - i, c, False), carry)
    o_ref[...] = jnp.concatenate([acc for (_, acc) in carry], axis=-1)
    _zero_fill_wait(copies)


def _sb_attention(qkv, batch, seq, fill=None):
    t = qkv.shape[0]
    tq = ATT_TILE
    nq = seq // tq
    npair = N_SB // SB_HEADS
    w = SB_HEADS * HEAD_DIM
    base = 3 * D_FOX // w
    grid = (batch, npair, nq)
    fill_shape, fill_specs, fill_scratch = _zero_fill_specs(fill)
    outs = pl.pallas_call(
        functools.partial(_sb_attn_kernel, grid=grid),
        out_shape=tuple([jax.ShapeDtypeStruct((t, D_SB), F32)] + fill_shape),
        grid=grid,
        in_specs=[
            pl.BlockSpec((tq, w), lambda b, h, i: (b * nq + i, base + h)),
            pl.BlockSpec((seq, w), lambda b, h, i: (b, base + npair + h)),
            pl.BlockSpec((seq, w), lambda b, h, i: (b, base + 2 * npair + h)),
        ],
        out_specs=tuple([pl.BlockSpec((tq, w), lambda b, h, i: (b * nq + i, h))] + fill_specs),
        scratch_shapes=fill_scratch,
        compiler_params=_params(("parallel", "parallel", "arbitrary")),
        name="sb_attn",
    )(qkv, qkv, qkv)
    return outs if fill is not None else outs[0]


def _mla_prep_kernel(cq_ref, ckv_ref, kr1_ref, kr2_ref, pos_ref, gq_ref, gkv_ref, wq_ref, wkv_ref,
                     invf_ref, sgn_ref, q_out, k_out, v_out):
    ang = pos_ref[...].astype(F32) * invf_ref[...]
    cosv = jnp.cos(ang)
    sinv = jnp.sin(ang) * sgn_ref[...]
    qn = (_rms(cq_ref[...]) * gq_ref[...]).astype(BF16)
    qa = jnp.dot(qn, wq_ref[...], preferred_element_type=F32)
    for h in range(N_MLA):
        o = h * 3 * LANES
        pe = qa[:, o + LANES:o + 2 * LANES] * cosv + qa[:, o + 2 * LANES:o + 3 * LANES] * sinv
        q_out[:, h * 2 * LANES:h * 2 * LANES + LANES] = qa[:, o:o + LANES].astype(BF16)
        q_out[:, h * 2 * LANES + LANES:(h + 1) * 2 * LANES] = pe.astype(BF16)
    kvn = (_rms(ckv_ref[...]) * gkv_ref[...]).astype(BF16)
    kva = jnp.dot(kvn, wkv_ref[...], preferred_element_type=F32)
    kpe = (kr1_ref[...] * cosv + kr2_ref[...] * sinv).astype(BF16)
    for h in range(N_MLA):
        k_out[:, h * 2 * LANES:h * 2 * LANES + LANES] = kva[:, h * LANES:(h + 1) * LANES].astype(BF16)
        k_out[:, h * 2 * LANES + LANES:(h + 1) * 2 * LANES] = kpe
    v_out[...] = kva[:, N_MLA * MLA_NOPE:].astype(BF16)


def _mla_prep(misc, pos, gq, gkv, wq, wkv, tm=512):
    t = misc.shape[0]
    half = MLA_ROPE // 2
    inv_freq = ROPE_THETA ** (-jnp.arange(half, dtype=F32) / half)
    invf = jnp.tile(inv_freq, LANES // half).reshape(1, LANES)
    sgn = jnp.tile(jnp.concatenate([-jnp.ones((half,), F32), jnp.ones((half,), F32)]),
                   LANES // MLA_ROPE).reshape(1, LANES)
    const = lambda i: (0, 0)
    return pl.pallas_call(
        _mla_prep_kernel,
        out_shape=(jax.ShapeDtypeStruct((t, N_MLA * 2 * LANES), BF16),
                   jax.ShapeDtypeStruct((t, N_MLA * 2 * LANES), BF16),
                   jax.ShapeDtypeStruct((t, D_MLA), BF16)),
        grid=(t // tm,),
        in_specs=[
            pl.BlockSpec((tm, Q_LORA), lambda i: (i, MISC_CQ // Q_LORA)),
            pl.BlockSpec((tm, KV_LORA), lambda i: (i, MISC_CKV // KV_LORA)),
            pl.BlockSpec((tm, LANES), lambda i: (i, MISC_KR1 // LANES)),
            pl.BlockSpec((tm, LANES), lambda i: (i, MISC_KR2 // LANES)),
            pl.BlockSpec((tm, 1), lambda i: (i, 0)),
            pl.BlockSpec((1, Q_LORA), const),
            pl.BlockSpec((1, KV_LORA), const),
            pl.BlockSpec(wq.shape, const),
            pl.BlockSpec(wkv.shape, const),
            pl.BlockSpec((1, LANES), const),
            pl.BlockSpec((1, LANES), const),
        ],
        out_specs=(pl.BlockSpec((tm, N_MLA * 2 * LANES), lambda i: (i, 0)),
                   pl.BlockSpec((tm, N_MLA * 2 * LANES), lambda i: (i, 0)),
                   pl.BlockSpec((tm, D_MLA), lambda i: (i, 0))),
        compiler_params=_params(("parallel",)),
        name="mla_prep",
    )(misc, misc, misc, misc, pos, gq.reshape(1, -1), gkv.reshape(1, -1), wq, wkv, invf, sgn)


def _gelu(x):
    return 0.5 * x * (1.0 + lax.erf(x * np.float32(np.sqrt(0.5))))


def _sgu_kernel(u_ref, v_ref, lng_ref, lnb_ref, w_ref, bias_ref, o_ref):
    tm = u_ref.shape[0]
    v = _gelu(v_ref[...])
    mu = jnp.mean(v, axis=-1, keepdims=True)
    xc = v - mu
    var = jnp.mean(xc * xc, axis=-1, keepdims=True)
    vb = (xc * lax.rsqrt(var + LN_EPS) * lng_ref[...] + lnb_ref[...]).astype(BF16)
    row = lax.broadcasted_iota(jnp.int32, (CHUNK, CHUNK), 0)
    col = lax.broadcasted_iota(jnp.int32, (CHUNK, LANES), 1)
    tril = lax.broadcasted_iota(jnp.int32, (CHUNK, CHUNK), 1) <= row
    first = col < SGU_CH
    ws = [jnp.where(tril, w_ref[g], 0.0).astype(BF16) for g in range(N_SGU)]
    zero = jnp.zeros((CHUNK, LANES), BF16)
    for c in range(tm // CHUNK):
        rows = slice(c * CHUNK, (c + 1) * CHUNK)
        for p in range(N_SGU // 2):
            cols = slice(p * LANES, (p + 1) * LANES)
            vp = vb[rows, cols]
            mixed = (jnp.dot(ws[2 * p], jnp.where(first, vp, zero), preferred_element_type=F32)
                     + jnp.dot(ws[2 * p + 1], jnp.where(first, zero, vp), preferred_element_type=F32))
            o_ref[rows, cols] = _gelu(u_ref[rows, cols]) * (mixed + bias_ref[:, cols])


def _sgu(misc, ln_g, ln_b, w_s, b_s, tm=512):
    t = misc.shape[0]
    bias = jnp.repeat(b_s.T, SGU_CH, axis=1)
    const2 = lambda i: (0, 0)
    return pl.pallas_call(
        _sgu_kernel,
        out_shape=jax.ShapeDtypeStruct((t, D_SGU), F32),
        grid=(t // tm,),
        in_specs=[
            pl.BlockSpec((tm, D_SGU), lambda i: (i, MISC_U // D_SGU)),
            pl.BlockSpec((tm, D_SGU), lambda i: (i, MISC_V // D_SGU)),
            pl.BlockSpec((1, D_SGU), const2),
            pl.BlockSpec((1, D_SGU), const2),
            pl.BlockSpec((N_SGU, CHUNK, CHUNK), lambda i: (0, 0, 0)),
            pl.BlockSpec((CHUNK, D_SGU), const2),
        ],
        out_specs=pl.BlockSpec((tm, D_SGU), lambda i: (i, 0)),
        compiler_params=_params(("parallel",)),
        name="sgu",
    )(misc, misc, ln_g.reshape(1, -1), ln_b.reshape(1, -1), w_s, bias)


def _pack_bf16_pairs(hb):
    n = hb.shape[1] // 2
    bits = pltpu.bitcast(hb.astype(F32), jnp.uint32)
    return (bits[:, n:] & jnp.uint32(0xFFFF0000)) | (bits[:, :n] >> 16)


def _unpack_bf16_pairs(xu):
    lo = pltpu.bitcast(xu << 16, F32).astype(BF16)
    hi = pltpu.bitcast(xu & jnp.uint32(0xFFFF0000), F32).astype(BF16)
    return jnp.concatenate([lo, hi], axis=1)


def _outproj_router_kernel(of_ref, os_ref, om_ref, og_ref, x_ref, ong_ref, wo_ref, nfg_ref, rw_ref,
                           rb_ref, xo_ref, h2_ref, idx_ref, gate_ref, rank_ref, cnt_ref, carry_ref):
    tm = x_ref.shape[0]
    th = tm // 2
    halves = [slice(h * th, (h + 1) * th) for h in range(2)]

    @pl.when(pl.program_id(0) == 0)
    def _():
        carry_ref[...] = jnp.zeros_like(carry_ref)

    os_ = []
    for rows in halves:
        o = jnp.concatenate([_rms(r[rows, :]) for r in (of_ref, os_ref, om_ref, og_ref)], axis=-1)
        os_.append((o * ong_ref[...]).astype(BF16))
    xns = [x_ref[rows, :] + jnp.dot(o, wo_ref[...], preferred_element_type=F32) for rows, o in zip(halves, os_)]
    hbs = []
    for rows, xn in zip(halves, xns):
        xo_ref[rows, :] = xn
        hb = (_rms(xn) * nfg_ref[...]).astype(BF16)
        h2_ref[rows, :] = _pack_bf16_pairs(hb)
        hbs.append(hb)

    lane = lax.broadcasted_iota(jnp.int32, (th, LANES), 1)
    all_logits = [jnp.dot(hb, rw_ref[...], preferred_element_type=F32) + rb_ref[...] for hb in hbs]
    picks = []
    for logits in all_logits:
        vals = jnp.where(lane < N_EXPERTS, logits, -jnp.inf)
        sels, tops = [], []
        for _ in range(TOP_K):
            m = jnp.max(vals, axis=-1, keepdims=True)
            idx = jnp.min(jnp.where(vals == m, lane, LANES), axis=-1, keepdims=True)
            sel = lane == idx
            vals = jnp.where(sel, -jnp.inf, vals)
            sels.append(sel)
            tops.append((m, idx))
        es = [jnp.exp(m - tops[0][0]) for (m, _) in tops]
        den = es[0] + es[1] + es[2] + es[3]
        mh = jnp.where(sels[0] | sels[1] | sels[2] | sels[3], 1.0, 0.0)
        picks.append((sels, tops, es, den, mh))

    r = lax.broadcasted_iota(jnp.int32, (th, th), 0)
    c = lax.broadcasted_iota(jnp.int32, (th, th), 1)
    before = (c < r).astype(BF16)
    parts = [jnp.dot(before, p[4].astype(BF16), preferred_element_type=F32) for p in picks]
    total = carry_ref[...]
    for rows, part, (sels, tops, es, den, mh) in zip(halves, parts, picks):
        cnt = part + total
        total = cnt[th - 1:th, :] + mh[th - 1:th, :]
        idx_o = jnp.zeros((th, LANES), jnp.int32)
        gate_o = jnp.zeros((th, LANES), F32)
        rank_o = jnp.zeros((th, LANES), jnp.int32)
        for k in range(TOP_K):
            rank_k = jnp.sum(jnp.where(sels[k], cnt, 0.0), axis=-1, keepdims=True).astype(jnp.int32)
            idx_o = jnp.where(lane == k, tops[k][1], idx_o)
            gate_o = jnp.where(lane == k, es[k] / den, gate_o)
            rank_o = jnp.where(lane == k, rank_k, rank_o)
        idx_ref[rows, :] = idx_o
        gate_ref[rows, :] = gate_o
        rank_ref[rows, :] = rank_o
    carry_ref[...] = total
    cnt_ref[...] = jnp.broadcast_to(total, cnt_ref.shape).astype(jnp.int32)


def _outproj_router(o_fox, o_sb, o_mla, o_sgu, x, out_norm_g, w_o, norm_ffn_g, router_w, router_b, tm=512):
    t, d = x.shape
    dg = o_fox.shape[1]
    rw = jnp.zeros((d, LANES), BF16).at[:, :N_EXPERTS].set(router_w.astype(BF16))
    rb = jnp.zeros((1, LANES), F32).at[0, :N_EXPERTS].set(router_b)
    const = lambda i: (0, 0)
    grp = pl.BlockSpec((tm, dg), lambda i: (i, 0))
    lanes_out = pl.BlockSpec((tm, LANES), lambda i: (i, 0))
    return pl.pallas_call(
        _outproj_router_kernel,
        out_shape=(jax.ShapeDtypeStruct((t, d), F32),
                   jax.ShapeDtypeStruct((t, d // 2), jnp.uint32),
                   jax.ShapeDtypeStruct((t, LANES), jnp.int32),
                   jax.ShapeDtypeStruct((t, LANES), F32),
                   jax.ShapeDtypeStruct((t, LANES), jnp.int32),
                   jax.ShapeDtypeStruct((8, LANES), jnp.int32)),
        grid=(t // tm,),
        in_specs=[grp, grp, grp, grp,
                  pl.BlockSpec((tm, d), lambda i: (i, 0)),
                  pl.BlockSpec((1, 4 * dg), const),
                  pl.BlockSpec((4 * dg, d), const),
                  pl.BlockSpec((1, d), const),
                  pl.BlockSpec((d, LANES), const),
                  pl.BlockSpec((1, LANES), const)],
        out_specs=(pl.BlockSpec((tm, d), lambda i: (i, 0)),
                   pl.BlockSpec((tm, d // 2), lambda i: (i, 0)),
                   lanes_out, lanes_out, lanes_out,
                   pl.BlockSpec((8, LANES), const)),
        scratch_shapes=[pltpu.VMEM((1, LANES), F32)],
        compiler_params=_params(("arbitrary",)),
        name="outproj_router",
    )(o_fox, o_sb, o_mla, o_sgu, x, out_norm_g.reshape(1, -1), w_o, norm_ffn_g.reshape(1, -1), rw, rb)


def _scatter_rows_kernel(dest_hbm, x_ref, xs_init_hbm, xs_hbm, dest0, dest1, sems):
    del xs_init_hbm
    dest_smem = (dest0, dest1)
    i = pl.program_id(0)
    nsteps = pl.num_programs(0)
    tm = x_ref.shape[0]
    n = tm * TOP_K

    def idx_copy(step, sl):
        return pltpu.make_async_copy(dest_hbm.at[pl.ds(step * n, n)], dest_smem[sl], sems.at[sl])

    @pl.when(i == 0)
    def _():
        idx_copy(0, 0).start()

    for sl in range(2):
        @pl.when(i % 2 == sl)
        def _(sl=sl):
            @pl.when(i + 1 < nsteps)
            def _():
                idx_copy(i + 1, 1 - sl).start()

            idx_copy(i, sl).wait()

            def issue(t, _):
                for k in range(TOP_K):
                    pltpu.make_async_copy(x_ref.at[pl.ds(t, 1)],
                                          xs_hbm.at[pl.ds(dest_smem[sl][t * TOP_K + k], 1)],
                                          sems.at[2]).start(priority=k % 2)
                return 0

            lax.fori_loop(0, tm, issue, 0, unroll=8)

    for _ in range(TOP_K):
        pltpu.make_async_copy(x_ref, xs_hbm.at[pl.ds(0, tm)], sems.at[2]).wait()


def _scatter_rows(dest_flat, src, nrows, xs_init):
    t, w = src.shape
    tm = SCATTER_TM
    assert xs_init.shape == (nrows, w) and xs_init.dtype == src.dtype
    return pl.pallas_call(
        _scatter_rows_kernel,
        out_shape=jax.ShapeDtypeStruct((nrows, w), src.dtype),
        grid=(t // tm,),
        in_specs=[pl.BlockSpec(memory_space=pl.ANY),
                  pl.BlockSpec((tm, w), lambda i: (i, 0)),
                  pl.BlockSpec(memory_space=pl.ANY)],
        out_specs=pl.BlockSpec(memory_space=pl.ANY),
        scratch_shapes=[pltpu.SMEM((tm * TOP_K,), jnp.int32), pltpu.SMEM((tm * TOP_K,), jnp.int32),
                        pltpu.SemaphoreType.DMA((3,))],
        input_output_aliases={2: 0},
        compiler_params=_params(("arbitrary",)),
        name="scatter_rows",
    )(dest_flat, src, xs_init)


def _expert_kernel(sbe_ref, nval_ref, nused_ref, x_ref, wg_ref, bg_ref, wu_ref, bu_ref, wd_ref, bd_ref,
                   y_init_hbm, y_ref):
    del sbe_ref, nused_ref, y_init_hbm
    s = pl.program_id(0)
    f = pl.program_id(1)
    d = y_ref.shape[1]

    @pl.when(f == 0)
    def _():
        y_ref[...] = jnp.broadcast_to(bd_ref[...], y_ref.shape)

    nsub = (nval_ref[s] + EXP_SUB - 1) // EXP_SUB

    for n in range(1, EXP_ROWS // EXP_SUB + 1):
        @pl.when(nsub == n)
        def _(m=n * EXP_SUB):
            x = _unpack_bf16_pairs(x_ref[0:m, :])
            g = jnp.dot(x, wg_ref[...].astype(BF16), preferred_element_type=F32) + bg_ref[...]
            g = jnp.minimum(g, SWIGLU_LIMIT)
            u = jnp.dot(x, wu_ref[...].astype(BF16), preferred_element_type=F32) + bu_ref[...]
            u = jnp.clip(u, -SWIGLU_LIMIT, SWIGLU_LIMIT)
            a = (g * jax.nn.sigmoid(SWIGLU_ALPHA * g) * (u + 1.0)).astype(BF16)
            wd = wd_ref[...].astype(BF16)
            for c in range(d // EXP_DN):
                cols = slice(c * EXP_DN, (c + 1) * EXP_DN)
                y_ref[0:m, cols] += jnp.dot(a, wd[:, cols], preferred_element_type=F32)


def _experts(sb_expert, sb_nvalid, n_used, xs, layer, w_gate, b_gate, w_up, b_up, w_down, b_down, nsb_max, y_init):
    d = w_gate.shape[2]
    dff = w_gate.shape[3]
    assert d % EXP_DN == 0 and dff % EXP_FF == 0 and EXP_ROWS % EXP_SUB == 0
    assert y_init.shape == (nsb_max * EXP_ROWS, d) and y_init.dtype == F32
    nf = dff // EXP_FF

    def xmap(s, f, sbe, nval, nused):
        return (s, 0)

    def ff(s, f, nused):
        return f

    b_gate4 = b_gate.reshape(b_gate.shape[0], N_EXPERTS, 1, dff)
    b_up4 = b_up.reshape(b_up.shape[0], N_EXPERTS, 1, dff)
    b_down4 = b_down.reshape(b_down.shape[0], N_EXPERTS, 1, d)
    y_init_operand = 3 + 7
    return pl.pallas_call(
        _expert_kernel,
        out_shape=jax.ShapeDtypeStruct((nsb_max * EXP_ROWS, d), F32),
        grid_spec=pltpu.PrefetchScalarGridSpec(
            num_scalar_prefetch=3,
            grid=(n_used[0], nf),
            in_specs=[
                pl.BlockSpec((EXP_ROWS, d // 2), xmap),
                pl.BlockSpec((None, None, d, EXP_FF), lambda s, f, sbe, nval, nused: (layer, sbe[s], 0, ff(s, f, nused))),
                pl.BlockSpec((None, None, 1, EXP_FF), lambda s, f, sbe, nval, nused: (layer, sbe[s], 0, ff(s, f, nused))),
                pl.BlockSpec((None, None, d, EXP_FF), lambda s, f, sbe, nval, nused: (layer, sbe[s], 0, ff(s, f, nused))),
                pl.BlockSpec((None, None, 1, EXP_FF), lambda s, f, sbe, nval, nused: (layer, sbe[s], 0, ff(s, f, nused))),
                pl.BlockSpec((None, None, EXP_FF, d), lambda s, f, sbe, nval, nused: (layer, sbe[s], ff(s, f, nused), 0)),
                pl.BlockSpec((None, None, 1, d), lambda s, f, sbe, nval, nused: (layer, sbe[s], 0, 0)),
                pl.BlockSpec(memory_space=pl.ANY),
            ],
            out_specs=pl.BlockSpec((EXP_ROWS, d), xmap),
        ),
        input_output_aliases={y_init_operand: 0},
        compiler_params=_params(("arbitrary", "arbitrary")),
        name="experts",
    )(sb_expert, sb_nvalid, n_used, xs, w_gate, b_gate4, w_up, b_up4, w_down, b_down4, y_init)


def _combine_kernel(dest_hbm, y_hbm, x_ref, gate_ref, fg_ref, o_ref, dest0, dest1, ybuf0, ybuf1, sems, *,
                    final_norm):
    dest_smem = (dest0, dest1)
    ybuf = (ybuf0, ybuf1)
    i = pl.program_id(0)
    nsteps = pl.num_programs(0)
    tm = x_ref.shape[0]
    n = tm * TOP_K
    slot = i % 2
    nslot = 1 - slot

    def idx_copy(step, sl):
        return pltpu.make_async_copy(dest_hbm.at[pl.ds(step * n, n)], dest_smem[sl], sems.at[sl])

    def issue_rows(sl):
        def issue(t, _):
            for k in range(TOP_K):
                pltpu.make_async_copy(y_hbm.at[pl.ds(dest_smem[sl][t * TOP_K + k], 1)],
                                      ybuf[sl].at[pl.ds(k * tm + t, 1)],
                                      sems.at[2 + sl]).start(priority=k % 2)
            return 0

        lax.fori_loop(0, tm, issue, 0, unroll=4)

    @pl.when(i == 0)
    def _():
        first = idx_copy(0, 0)
        first.start()
        first.wait()
        issue_rows(0)

        @pl.when(nsteps > 1)
        def _():
            idx_copy(1, 1).start()

    for sl in range(2):
        @pl.when(jnp.logical_and(i + 1 < nsteps, nslot == sl))
        def _(sl=sl):
            idx_copy(i + 1, sl).wait()
            issue_rows(sl)

    for sl in range(2):
        @pl.when(slot == sl)
        def _(sl=sl):
            @pl.when(i + 2 < nsteps)
            def _():
                idx_copy(i + 2, sl).start()

            pltpu.make_async_copy(y_hbm.at[pl.ds(0, n)], ybuf[sl], sems.at[2 + sl]).wait()
            gate = gate_ref[...]
            acc = gate[:, 0:1] * ybuf[sl][0:tm]
            for k in range(1, TOP_K):
                acc = acc + gate[:, k:k + 1] * ybuf[sl][k * tm:(k + 1) * tm]
            out = x_ref[...] + acc
            if final_norm:
                out = _rms(out) * fg_ref[...]
            o_ref[...] = out


def _combine(dest_flat, y, x, gates, final_g, final_norm):
    t, d = x.shape
    tm = COMBINE_TM
    return pl.pallas_call(
        functools.partial(_combine_kernel, final_norm=final_norm),
        out_shape=jax.ShapeDtypeStruct((t, d), F32),
        grid=(t // tm,),
        in_specs=[
            pl.BlockSpec(memory_space=pl.ANY),
            pl.BlockSpec(memory_space=pl.ANY),
            pl.BlockSpec((tm, d), lambda i: (i, 0)),
            pl.BlockSpec((tm, LANES), lambda i: (i, 0)),
            pl.BlockSpec((1, d), lambda i: (0, 0)),
        ],
        out_specs=pl.BlockSpec((tm, d), lambda i: (i, 0)),
        scratch_shapes=[pltpu.SMEM((tm * TOP_K,), jnp.int32), pltpu.SMEM((tm * TOP_K,), jnp.int32),
                        pltpu.VMEM((TOP_K * tm, d), F32), pltpu.VMEM((TOP_K * tm, d), F32),
                        pltpu.SemaphoreType.DMA((4,))],
        compiler_params=_params(("arbitrary",)),
        name="combine",
    )(dest_flat, y, x, gates, final_g.reshape(1, d))


def _in_proj_weights(w_in):
    d = w_in.shape[0]
    o_gate = 3 * D_FOX
    o_sb = o_gate + N_FOX
    o_cq = o_sb + 3 * D_SB
    o_ckv = o_cq + Q_LORA
    o_kr = o_ckv + KV_LORA
    o_sgu = o_kr + MLA_ROPE
    half = MLA_ROPE // 2
    w_qkv = jnp.concatenate([w_in[:, :o_gate], w_in[:, o_sb:o_cq]], axis=1).astype(BF16)
    kr = w_in[:, o_kr:o_sgu]
    zpad = jnp.zeros((d, LANES - MLA_ROPE), w_in.dtype)
    w_misc = jnp.concatenate([
        w_in[:, o_sgu:o_sgu + 2 * D_SGU],
        w_in[:, o_cq:o_ckv],
        w_in[:, o_ckv:o_kr],
        kr, zpad,
        kr[:, half:], kr[:, :half], zpad,
        w_in[:, o_gate:o_sb], jnp.zeros((d, LANES - N_FOX), w_in.dtype),
    ], axis=1).astype(BF16)
    assert w_misc.shape[1] == MISC_W
    return w_qkv, w_misc


def _mla_weights(w_q_b, w_kv_b):
    half = MLA_ROPE // 2
    wq = w_q_b.reshape(Q_LORA, N_MLA, MLA_NOPE + MLA_ROPE)
    x1 = wq[:, :, MLA_NOPE:MLA_NOPE + half]
    x2 = wq[:, :, MLA_NOPE + half:]
    z = jnp.zeros((Q_LORA, N_MLA, LANES - MLA_ROPE), w_q_b.dtype)
    wq_r = jnp.concatenate([wq[:, :, :MLA_NOPE], x1, x2, z, x2, x1, z], axis=2)
    wq_r = wq_r.reshape(Q_LORA, N_MLA * 3 * LANES).astype(BF16)
    wkv = w_kv_b.reshape(KV_LORA, N_MLA, MLA_NOPE + MLA_V)
    wkv_r = jnp.concatenate([wkv[:, :, :MLA_NOPE].reshape(KV_LORA, -1),
                             wkv[:, :, MLA_NOPE:].reshape(KV_LORA, -1)], axis=1).astype(BF16)
    return wq_r, wkv_r


def _routing_tables(idx, rank, cnt, n_tokens):
    nsb_max = N_EXPERTS + (n_tokens * TOP_K) // EXP_ROWS
    counts = cnt[0, :N_EXPERTS]
    nsb_e = (counts + EXP_ROWS - 1) // EXP_ROWS
    sb_end = jnp.cumsum(nsb_e)
    sb_start = sb_end - nsb_e
    n_used = sb_end[-1]
    top_idx = idx[:, :TOP_K].reshape(-1)
    dest = (sb_start * EXP_ROWS)[top_idx] + rank[:, :TOP_K].reshape(-1)
    s_ids = jnp.arange(nsb_max, dtype=jnp.int32)
    sb_e = jnp.minimum(jnp.sum(s_ids[:, None] >= sb_end[None, :], axis=1), N_EXPERTS - 1).astype(jnp.int32)
    last_e = sb_e[jnp.maximum(n_used - 1, 0)]
    sb_e = jnp.where(s_ids < n_used, sb_e, last_e)
    nval = jnp.clip(counts[sb_e] - (s_ids - sb_start[sb_e]) * EXP_ROWS, 0, EXP_ROWS)
    nval = jnp.where(s_ids < n_used, nval, 0).astype(jnp.int32)
    return dest.astype(jnp.int32), sb_e, nval, n_used.reshape(1).astype(jnp.int32), nsb_max


def _layer(x, pos, l, p, final_g, final_norm, batch, seq, moe_bufs):
    t = x.shape[0]
    w_qkv, w_misc = _in_proj_weights(p["w_in"][l])
    qkv = _norm_matmul(x, p["norm_mix_g"][l], w_qkv, BF16, 512, 1536, "in_proj_qkv")
    misc = _norm_matmul(x, p["norm_mix_g"][l], w_misc, F32, 512, MISC_W, "in_proj_misc")

    crow = _fox_prep(misc, p["b_forget"][l], batch, seq)
    nkb = seq // ATT_TILE
    crow = crow.reshape(batch, nkb, N_FOX // FOX_HEADS, FOX_HEADS, ATT_TILE).transpose(0, 2, 1, 3, 4)
    if moe_bufs is None:
        nrows = (N_EXPERTS + (t * TOP_K) // EXP_ROWS) * EXP_ROWS
        o_fox, xs_prev = _fox_attention(qkv, crow, batch, seq,
                                        jax.ShapeDtypeStruct((nrows, x.shape[1] // 2), jnp.uint32))
        o_sb, y_prev = _sb_attention(qkv, batch, seq, jax.ShapeDtypeStruct((nrows, x.shape[1]), F32))
    else:
        xs_prev, y_prev = moe_bufs
        o_fox = _fox_attention(qkv, crow, batch, seq)
        o_sb = _sb_attention(qkv, batch, seq)
    wq_r, wkv_r = _mla_weights(p["mla_w_q_b"][l], p["mla_w_kv_b"][l])
    q_m, k_m, v_m = _mla_prep(misc, pos, p["mla_q_norm_g"][l], p["mla_kv_norm_g"][l], wq_r, wkv_r)
    o_mla = _mla_attention(q_m, k_m, v_m, batch, seq)
    o_sgu = _sgu(misc, p["sgu_ln_g"][l], p["sgu_ln_b"][l], p["sgu_w_s"][l], p["sgu_b_s"][l])

    x_new, h2p, idx, gates, rank, cnt = _outproj_router(
        o_fox, o_sb, o_mla, o_sgu, x, p["out_norm_g"][l], p["w_o"][l].astype(BF16),
        p["norm_ffn_g"][l], p["router_w"][l], p["router_b"][l])

    dest, sb_e, nval, n_used, nsb_max = _routing_tables(idx, rank, cnt, t)
    xs = _scatter_rows(dest, h2p, nsb_max * EXP_ROWS, xs_prev)
    y = _experts(sb_e, nval, n_used, xs, l, p["w_gate"], p["b_gate"], p["w_up"], p["b_up"],
                 p["w_down"], p["b_down"], nsb_max, y_prev)
    return _combine(dest, y, x_new, gates, final_g, final_norm), (xs, y)


def kernel(x, positions, norm_mix_g, w_in, b_forget, mla_q_norm_g, mla_kv_norm_g, mla_w_q_b, mla_w_kv_b,
           sgu_ln_g, sgu_ln_b, sgu_w_s, sgu_b_s, out_norm_g, w_o, norm_ffn_g, router_w, router_b, w_gate,
           b_gate, w_up, b_up, w_down, b_down, final_norm_g):
    batch, seq, d = x.shape
    depth = w_in.shape[0]
    p = dict(norm_mix_g=norm_mix_g, w_in=w_in, b_forget=b_forget, mla_q_norm_g=mla_q_norm_g,
             mla_kv_norm_g=mla_kv_norm_g, mla_w_q_b=mla_w_q_b, mla_w_kv_b=mla_w_kv_b, sgu_ln_g=sgu_ln_g,
             sgu_ln_b=sgu_ln_b, sgu_w_s=sgu_w_s, sgu_b_s=sgu_b_s, out_norm_g=out_norm_g, w_o=w_o,
             norm_ffn_g=norm_ffn_g, router_w=router_w, router_b=router_b, w_gate=w_gate, b_gate=b_gate,
             w_up=w_up, b_up=b_up, w_down=w_down, b_down=b_down)
    h = x.reshape(batch * seq, d)
    pos = positions.reshape(batch * seq, 1)
    moe_bufs = None
    for l in range(depth):
        h, moe_bufs = _layer(h, pos, l, p, final_norm_g, l == depth - 1, batch, seq, moe_bufs)
    return h.reshape(batch, seq, d)
```
